```python
import math
import jax
import jax.numpy as jnp
from jax import lax
import numpy as np


D_MODEL = 1024
BATCH = 4
SEQ = 8192
DEPTH = 2

CHUNK = 64
Q_BLOCK = 128
ROPE_THETA = 10000.0
RMS_EPS = 1e-6

A_HEADS = D_MODEL // 128
A_HEAD_DIM = 64
D_A = A_HEADS * A_HEAD_DIM
IDX_HEADS = 8
IDX_DIM = 32
TOPK_MAX = 256

B_HEADS = D_MODEL // 256
B_HEAD_DIM = 64
D_B = B_HEADS * 2 * B_HEAD_DIM

D_MIX = D_A + D_B
D_FF = -(-8 * D_MODEL // (3 * 256)) * 256

IN_SPLITS = (D_A, D_A, D_A, IDX_HEADS * IDX_DIM, IDX_DIM, IDX_HEADS, D_B, D_B, D_B)
D_IN = D_A * 3 + IDX_HEADS * IDX_DIM + IDX_DIM + IDX_HEADS + D_B * 3

kernel_name = "hybrid_dsa_diffattn_swiglu_trunk"


def rms_norm(x, g):
    x32 = x.astype(jnp.float32)
    r = lax.rsqrt(jnp.mean(x32 * x32, axis=-1, keepdims=True) + RMS_EPS)
    return (x32 * r * g.astype(jnp.float32)).astype(x.dtype)


def rope_tables(seq_len, dim):
    pos = jnp.arange(seq_len, dtype=jnp.float32)
    inv = ROPE_THETA ** (-jnp.arange(0, dim, 2, dtype=jnp.float32) / dim)
    ang = pos[:, None] * inv[None, :]
    return jnp.cos(ang), jnp.sin(ang)


def apply_rope(x, cos, sin):
    half = x.shape[-1] // 2
    x32 = x.astype(jnp.float32)
    x1, x2 = x32[..., :half], x32[..., half:]
    c = cos[None, :, None, :]
    s = sin[None, :, None, :]
    return jnp.concatenate([x1 * c - x2 * s, x2 * c + x1 * s], axis=-1).astype(x.dtype)


def to_blocks(a):
    b, s = a.shape[:2]
    return jnp.swapaxes(a.reshape(b, s // Q_BLOCK, Q_BLOCK, *a.shape[2:]), 0, 1)


def from_blocks(a):
    nb, b = a.shape[:2]
    return jnp.swapaxes(a, 0, 1).reshape(b, nb * a.shape[2], *a.shape[3:])


def dsa_mixer(q, k, v, iq, ik, iw):
    s_len = q.shape[1]
    topk = min(TOPK_MAX, s_len // 4)
    key_chunk = jnp.arange(s_len) // CHUNK
    scale = A_HEAD_DIM ** -0.5
    gather = jax.vmap(lambda a, i: a[i])

    def block(args):
        bi, qb, iqb, iwb = args
        q_chunk = (bi * Q_BLOCK + jnp.arange(Q_BLOCK)) // CHUNK
        admissible = key_chunk[None, :] <= q_chunk[:, None]
        dots = jnp.einsum('bqhd,bsd->bqhs', iqb, ik,
                          preferred_element_type=jnp.float32) * (IDX_DIM ** -0.5)
        w = iwb.astype(jnp.float32) * (IDX_HEADS ** -0.5)
        score = jnp.einsum('bqhs,bqh->bqs', jax.nn.relu(dots), w)
        score = jnp.where(admissible[None], score, -jnp.inf)
        _, idx = lax.top_k(score, topk)
        k_sel = gather(k, idx)
        v_sel = gather(v, idx)
        valid = (idx // CHUNK) <= q_chunk[None, :, None]
        logits = jnp.einsum('bqhd,bqkhd->bhqk', qb, k_sel,
                            preferred_element_type=jnp.float32) * scale
        logits = jnp.where(valid[:, None], logits, -jnp.inf)
        p = jax.nn.softmax(logits, axis=-1)
        return jnp.einsum('bhqk,bqkhd->bqhd', p.astype(v.dtype), v_sel)

    nb = s_len // Q_BLOCK
    out = lax.map(block, (jnp.arange(nb), to_blocks(q), to_blocks(iq), to_blocks(iw)))
    return from_blocks(out)


def diff_mixer(q, k, v, lam, lam_init, g_subln):
    s_len = q.shape[1]
    key_chunk = jnp.arange(s_len) // CHUNK
    scale = B_HEAD_DIM ** -0.5

    def block(args):
        bi, qb = args
        q_chunk = (bi * Q_BLOCK + jnp.arange(Q_BLOCK)) // CHUNK
        admissible = key_chunk[None, :] <= q_chunk[:, None]
        logits = jnp.einsum('bqhcd,bshcd->bchqs', qb, k,
                            preferred_element_type=jnp.float32) * scale
        logits = jnp.where(admissible[None, None, None], logits, -jnp.inf)
        p = jax.nn.softmax(logits, axis=-1)
        attn = p[:, 0] - lam * p[:, 1]
        return jnp.einsum('bhqs,bshe->bqhe', attn.astype(v.dtype), v)

    nb = s_len // Q_BLOCK
    out = from_blocks(lax.map(block, (jnp.arange(nb), to_blocks(q))))
    return rms_norm(out, g_subln) * (1.0 - lam_init)


def setup_inputs(seed: int = 0) -> dict:
    key = jax.random.key(seed)
    ks = jax.random.split(key, 14)
    f32 = jnp.float32
    nrm = lambda k, shape, fan_in: jax.random.normal(k, shape, f32) * (fan_in ** -0.5)
    gain = lambda k, shape: 1.0 + 0.02 * jax.random.normal(k, shape, f32)
    return {
        "x": jax.random.normal(ks[0], (BATCH, SEQ, D_MODEL), f32),
        "w_in": nrm(ks[1], (DEPTH, D_MODEL, D_IN), D_MODEL),
        "w_out": nrm(ks[2], (DEPTH, D_MIX, D_MODEL), D_MIX),
        "g_mix": gain(ks[3], (DEPTH, D_MODEL)),
        "lam_q1": 0.1 * jax.random.normal(ks[4], (DEPTH, B_HEAD_DIM), f32),
        "lam_k1": 0.1 * jax.random.normal(ks[5], (DEPTH, B_HEAD_DIM), f32),
        "lam_q2": 0.1 * jax.random.normal(ks[6], (DEPTH, B_HEAD_DIM), f32),
        "lam_k2": 0.1 * jax.random.normal(ks[7], (DEPTH, B_HEAD_DIM), f32),
        "g_subln": gain(ks[8], (DEPTH, 2 * B_HEAD_DIM)),
        "g_ffn": gain(ks[9], (DEPTH, D_MODEL)),
        "w_gate": nrm(ks[10], (DEPTH, D_MODEL, D_FF), D_MODEL),
        "w_up": nrm(ks[11], (DEPTH, D_MODEL, D_FF), D_MODEL),
        "w_down": nrm(ks[12], (DEPTH, D_FF, D_MODEL), D_FF),
        "g_final": gain(ks[13], (D_MODEL,)),
    }


def reference(x, w_in, w_out, g_mix, lam_q1, lam_k1, lam_q2, lam_k2, g_subln,
              g_ffn, w_gate, w_up, w_down, g_final):
    b_, s_len, _ = x.shape
    cos_a, sin_a = rope_tables(s_len, A_HEAD_DIM)
    cos_i, sin_i = rope_tables(s_len, IDX_DIM)
    cos_b, sin_b = rope_tables(s_len, B_HEAD_DIM)
    split_points = [int(o) for o in np.cumsum(IN_SPLITS)[:-1]]

    for layer in range(DEPTH):
        h = rms_norm(x, g_mix[layer])
        proj = jnp.einsum('bsd,dc->bsc', h, w_in[layer])
        qa, ka, va, iq, ik, iw, qb, kb, vb = jnp.split(proj, split_points, axis=-1)

        qa = apply_rope(qa.reshape(b_, s_len, A_HEADS, A_HEAD_DIM), cos_a, sin_a)
        ka = apply_rope(ka.reshape(b_, s_len, A_HEADS, A_HEAD_DIM), cos_a, sin_a)
        va = va.reshape(b_, s_len, A_HEADS, A_HEAD_DIM)
        iq = apply_rope(iq.reshape(b_, s_len, IDX_HEADS, IDX_DIM), cos_i, sin_i)
        ik = apply_rope(ik[:, :, None, :], cos_i, sin_i)[:, :, 0, :]
        out_a = dsa_mixer(qa, ka, va, iq, ik, iw).reshape(b_, s_len, D_A)

        qb = apply_rope(qb.reshape(b_, s_len, 2 * B_HEADS, B_HEAD_DIM), cos_b, sin_b)
        kb = apply_rope(kb.reshape(b_, s_len, 2 * B_HEADS, B_HEAD_DIM), cos_b, sin_b)
        qb = qb.reshape(b_, s_len, B_HEADS, 2, B_HEAD_DIM)
        kb = kb.reshape(b_, s_len, B_HEADS, 2, B_HEAD_DIM)
        vb = vb.reshape(b_, s_len, B_HEADS, 2 * B_HEAD_DIM)
        lam_init = 0.8 - 0.6 * math.exp(-0.3 * layer)
        lam = (jnp.exp(jnp.sum(lam_q1[layer].astype(jnp.float32) * lam_k1[layer].astype(jnp.float32)))
               - jnp.exp(jnp.sum(lam_q2[layer].astype(jnp.float32) * lam_k2[layer].astype(jnp.float32)))
               + lam_init)
        out_b = diff_mixer(qb, kb, vb, lam, lam_init, g_subln[layer]).reshape(b_, s_len, D_B)

        mixed = jnp.concatenate([out_a, out_b], axis=-1)
        x = x + jnp.einsum('bsc,cd->bsd', mixed, w_out[layer])

        h2 = rms_norm(x, g_ffn[layer])
        gate = jnp.einsum('bsd,df->bsf', h2, w_gate[layer])
        up = jnp.einsum('bsd,df->bsf', h2, w_up[layer])
        x = x + jnp.einsum('bsf,fd->bsd', jax.nn.silu(gate) * up, w_down[layer])

    return rms_norm(x, g_final)
```

```python
import functools
import math

import jax
import jax.numpy as jnp
import numpy as np
from jax import lax
from jax.experimental import pallas as pl
from jax.experimental.pallas import tpu as pltpu

F32 = jnp.float32
BF16 = jnp.bfloat16
I32 = jnp.int32

CHUNK = 64
ROPE_THETA = 10000.0
RMS_EPS = 1e-6
A_HEADS = 8
A_HEAD_DIM = 64
D_A = A_HEADS * A_HEAD_DIM
IDX_HEADS = 8
IDX_DIM = 32
TOPK_MAX = 256
B_HEADS = 4
B_HEAD_DIM = 64
D_B = B_HEADS * 2 * B_HEAD_DIM

LANES = 128
VMEM_LIMIT = 56 * 1024 * 1024

LOG2E = 1.4426950408889634
NEG = -1e30
NEG_BITS = int(np.float32(NEG).view(np.int32))
KEY_MIN = -(2 ** 31)

_OFF_QA, _OFF_KA, _OFF_VA = 0, 512, 1024
_OFF_IQ, _OFF_IK, _OFF_IW = 1536, 1792, 1920
_OFF_QB, _OFF_KB, _OFF_VB = 2048, 2560, 3072
_W_CAT = 3584

_NT = (((1,), (1,)), ((), ()))


def _rms(x, g):
    r = lax.rsqrt(jnp.mean(x * x, axis=-1, keepdims=True) + RMS_EPS)
    return x * r * g


def _proj_kernel(x_ref, g_ref, w_ref, cos_a_ref, sin_a_ref, cos_i_ref, sin_i_ref,
                 qa_ref, ka_ref, va_ref, iq_ref, ik_ref, iw_ref, qb_ref, kb_ref, vb_ref):
    h = _rms(x_ref[...], g_ref[...]).astype(BF16)
    tm = h.shape[0]
    lane = lax.broadcasted_iota(I32, (tm, LANES), 1)

    def proj(off, width):
        return jnp.dot(h, w_ref[:, off:off + width], preferred_element_type=F32)

    def rope_tile(y, cos, sin_signed, half):
        first = (lane & (2 * half - 1)) < half
        rot = jnp.where(first, pltpu.roll(y, LANES - half, 1), pltpu.roll(y, half, 1))
        return y * cos + rot * sin_signed

    def rope_store(out_ref, off, width, cos_ref, sin_ref, half, scale):
        y = proj(off, width)
        cos, sin = cos_ref[...], sin_ref[...]
        for t in range(width // LANES):
            r = rope_tile(y[:, t * LANES:(t + 1) * LANES], cos, sin, half)
            if scale != 1.0:
                r = r * scale
            out_ref[:, t * LANES:(t + 1) * LANES] = r.astype(out_ref.dtype)

    half_a, half_i = A_HEAD_DIM // 2, IDX_DIM // 2
    rope_store(qa_ref, _OFF_QA, D_A, cos_a_ref, sin_a_ref, half_a, A_HEAD_DIM ** -0.5 * LOG2E)
    rope_store(ka_ref, _OFF_KA, D_A, cos_a_ref, sin_a_ref, half_a, 1.0)
    va_ref[...] = proj(_OFF_VA, D_A).astype(va_ref.dtype)
    rope_store(iq_ref, _OFF_IQ, IDX_HEADS * IDX_DIM, cos_i_ref, sin_i_ref, half_i, 1.0)
    rope_store(ik_ref, _OFF_IK, LANES, cos_i_ref, sin_i_ref, half_i, 1.0)
    iw_ref[...] = proj(_OFF_IW, LANES) * (IDX_HEADS ** -0.5 * IDX_DIM ** -0.5)
    rope_store(qb_ref, _OFF_QB, D_B, cos_a_ref, sin_a_ref, half_a, B_HEAD_DIM ** -0.5 * LOG2E)
    rope_store(kb_ref, _OFF_KB, D_B, cos_a_ref, sin_a_ref, half_a, 1.0)
    vb_ref[...] = proj(_OFF_VB, D_B).astype(vb_ref.dtype)


def _proj_call(x, g, w_cat, tabs, tm):
    b, s, d = x.shape
    cos_a, sin_a, cos_i, sin_i = tabs
    row = lambda width: pl.BlockSpec((None, tm, width), lambda bi, i: (bi, i, 0))
    tab = pl.BlockSpec((tm, LANES), lambda bi, i: (i, 0))
    const = lambda shape: pl.BlockSpec(shape, lambda bi, i: (0, 0), pipeline_mode=pl.Buffered(1))
    widths = (D_A, D_A, D_A, IDX_HEADS * IDX_DIM, LANES, LANES, D_B, D_B, D_B)
    dtypes = (BF16, BF16, BF16, BF16, BF16, F32, BF16, BF16, BF16)
    return pl.pallas_call(
        _proj_kernel,
        grid=(b, s // tm),
        in_specs=[row(d), const((1, d)), const((d, _W_CAT)), tab, tab, tab, tab],
        out_specs=[row(w) for w in widths],
        out_shape=[jax.ShapeDtypeStruct((b, s, w), dt) for w, dt in zip(widths, dtypes)],
        compiler_params=pltpu.CompilerParams(
            dimension_semantics=("arbitrary", "arbitrary"), vmem_limit_bytes=VMEM_LIMIT),
        name="proj",
    )(x, g, w_cat, cos_a, sin_a, cos_i, sin_i)


def _online_softmax_step(s, v_t, m_ref, l_ref, acc_ref):
    m_prev = m_ref[...]
    m_new = jnp.maximum(m_prev, jnp.max(s, axis=1, keepdims=True))
    alpha = jnp.exp2(m_prev - m_new)
    p = jnp.exp2(s - m_new)
    l_ref[...] = alpha * l_ref[...] + jnp.sum(p, axis=1, keepdims=True)
    acc_ref[...] = alpha * acc_ref[...] + jnp.dot(p.astype(BF16), v_t, preferred_element_type=F32)
    m_ref[...] = m_new


def _split_lane_halves(tile_f32, lane):
    lo = jnp.where(lane < LANES // 2, tile_f32, 0.0)
    hi = jnp.where(lane >= LANES // 2, tile_f32, 0.0)
    return jnp.concatenate([lo, hi], axis=0).astype(BF16)


def _dsa_kernel(q_ref, k_ref, v_ref, iq_ref, ik_ref, iw_ref, o_ref,
                key_ref, iqm_ref, wb_ref, mn_ref, mx_ref, jcut_ref, m_ref, l_ref, acc_ref,
                *, tq, topk):
    tk = tq
    nch = tk // LANES
    rb = 128
    j = pl.program_id(1)
    q0 = j * tq
    n_tiles = j + 1
    lane = lax.broadcasted_iota(I32, (tq, LANES), 1)
    lane_rb = lax.broadcasted_iota(I32, (rb, LANES), 1)
    row = lax.broadcasted_iota(I32, (tq, 1), 0)
    q_end = (((q0 + row) >> 6) + 1) << 6
    n_adm = q_end.astype(F32)
    k_eff = jnp.minimum(n_adm, float(topk))

    iw = iw_ref[...]
    for h in range(IDX_HEADS):
        g = h % 4
        tile = iq_ref[:, (h // 4) * LANES:(h // 4 + 1) * LANES].astype(F32)
        keep = (lane >= g * IDX_DIM) & (lane < (g + 1) * IDX_DIM)
        iqm_ref[h * tq:(h + 1) * tq, :] = jnp.where(keep, tile, 0.0).astype(BF16)
        wb_ref[h * tq:(h + 1) * tq, :] = jnp.broadcast_to(iw[:, h:h + 1], (tq, LANES))

    mn_ref[...] = jnp.full((tq, LANES), 2 ** 31 - 1, I32)
    mx_ref[...] = jnp.full((tq, LANES), KEY_MIN, I32)

    def index_tile(kt, masked):
        r0 = pl.multiple_of(kt * tk, tk)
        d = lax.dot_general(iqm_ref[...], ik_ref[pl.ds(r0, tk), :], _NT,
                            preferred_element_type=F32)
        for c in range(nch):
            for r in range(tq // rb):
                rs = slice(r * rb, (r + 1) * rb)
                a = jnp.zeros((rb, LANES), F32)
                for h in range(IDX_HEADS):
                    hs = slice(h * tq + r * rb, h * tq + (r + 1) * rb)
                    a = a + wb_ref[hs, :] * jnp.maximum(d[hs, c * LANES:(c + 1) * LANES], 0.0)
                bits = lax.bitcast_convert_type(a, I32)
                key = bits ^ ((bits >> 31) & 0x7FFFFFFF)
                key_lo = key
                if masked:
                    adm = (kt * tk + c * LANES + lane_rb) < q_end[rs]
                    key_lo = jnp.where(adm, key, 2 ** 31 - 1)
                    key = jnp.where(adm, key, KEY_MIN)
                mn_ref[rs, :] = jnp.minimum(mn_ref[rs, :], key_lo)
                mx_ref[rs, :] = jnp.maximum(mx_ref[rs, :], key)
                key_ref[kt * nch + c, rs, :] = key

    def full_tile(kt, carry):
        index_tile(kt, False)
        return carry

    lax.fori_loop(0, j, full_tile, 0)
    index_tile(j, True)

    def count_pass(pred):
        outs = []
        for r in range(tq // rb):
            rs = slice(r * rb, (r + 1) * rb)

            def body(kt, acc):
                for c in range(nch):
                    hit = pred(key_ref[kt * nch + c, rs, :], kt * tk + c * LANES, rs)
                    acc = acc + jnp.where(hit, 1.0, 0.0)
                return acc

            acc = lax.fori_loop(0, n_tiles, body, jnp.zeros((rb, LANES), F32))
            outs.append(jnp.sum(acc, axis=1, keepdims=True))
        return jnp.concatenate(outs, axis=0)

    def count_ge(t):
        tb = jnp.broadcast_to(t, (tq, LANES))
        return count_pass(lambda keys, kidx0, rs: keys >= tb[rs])

    lo0 = jnp.min(mn_ref[...], axis=1, keepdims=True)
    hi0 = jnp.max(mx_ref[...], axis=1, keepdims=True)
    c_hi0 = jnp.zeros((tq, 1), F32)

    def bis_cond(st):
        lo, hi, c_lo, c_above = st
        open_rows = jnp.where((c_lo > k_eff) & (lo < hi), 1.0, 0.0)
        return jnp.max(open_rows) > 0.0

    def bis_body(st):
        lo, hi, c_lo, c_above = st
        mid = (lo >> 1) + (hi >> 1) + ((lo | hi) & 1)
        active = (c_lo > k_eff) & (lo < hi)
        c_mid = count_ge(mid)
        go_up = active & (c_mid >= k_eff)
        go_dn = active & (c_mid < k_eff)
        lo = jnp.where(go_up, mid, lo)
        c_lo = jnp.where(go_up, c_mid, c_lo)
        hi = jnp.where(go_dn, mid - 1, hi)
        c_above = jnp.where(go_dn, c_mid, c_above)
        return lo, hi, c_lo, c_above

    thr, _, c_thr, c_above = lax.while_loop(bis_cond, bis_body, (lo0, hi0, n_adm, c_hi0))
    thr_b = jnp.broadcast_to(thr, (tq, LANES))
    need = k_eff - c_above
    has_tie = c_thr > k_eff
    jcut_ref[...] = jnp.full((tq, LANES), 2.0 ** 30, F32)

    @pl.when(jnp.max(jnp.where(has_tie, 1.0, 0.0)) > 0.0)
    def _():
        def jb_body(_, st):
            jl, jh = st
            jm = jnp.floor((jl + jh) * 0.5)
            jm_b = jnp.broadcast_to(jm, (tq, LANES))

            def pred(keys, kidx0, rs):
                kidx = (kidx0 + lane_rb).astype(F32)
                return (keys == thr_b[rs]) & (kidx < jm_b[rs])

            ok = count_pass(pred) >= need
            return jnp.where(ok, jl, jm), jnp.where(ok, jm, jh)

        n_keys = (n_tiles * tk).astype(F32)
        jl0 = jnp.zeros((tq, 1), F32)
        jh0 = jnp.zeros((tq, 1), F32) + n_keys
        steps = int(math.ceil(math.log2(key_ref.shape[0] * LANES))) + 1
        _, jh = lax.fori_loop(0, steps, jb_body, (jl0, jh0))
        jcut_ref[...] = jnp.broadcast_to(jnp.where(has_tie, jh, 2.0 ** 30), (tq, LANES))

    def bias_tile(kt, carry):
        for c in range(nch):
            keys = key_ref[kt * nch + c]
            kidx = (kt * tk + c * LANES + lane).astype(F32)
            sel = (keys > thr_b) | ((keys == thr_b) & (kidx < jcut_ref[...]))
            key_ref[kt * nch + c] = jnp.where(sel, 0, NEG_BITS)
        return carry

    lax.fori_loop(0, n_tiles, bias_tile, 0)

    for hp in range(A_HEADS // 2):
        ls = slice(hp * LANES, (hp + 1) * LANES)
        q_st = _split_lane_halves(q_ref[:, ls].astype(F32), lane)
        m_ref[...] = jnp.full(m_ref.shape, NEG, F32)
        l_ref[...] = jnp.zeros(l_ref.shape, F32)
        acc_ref[...] = jnp.zeros(acc_ref.shape, F32)

        def att_tile(kt, carry):
            r0 = pl.multiple_of(kt * tk, tk)
            s = lax.dot_general(q_st, k_ref[pl.ds(r0, tk), ls], _NT, preferred_element_type=F32)
            bias = jnp.concatenate(
                [lax.bitcast_convert_type(key_ref[kt * nch + c], F32) for c in range(nch)], axis=1)
            s = jnp.concatenate([s[:tq] + bias, s[tq:] + bias], axis=0)
            _online_softmax_step(s, v_ref[pl.ds(r0, tk), ls], m_ref, l_ref, acc_ref)
            return carry

        lax.fori_loop(0, n_tiles, att_tile, 0)
        o = acc_ref[...] / l_ref[...]
        o_ref[:, ls] = jnp.where(lane < LANES // 2, o[:tq], o[tq:]).astype(o_ref.dtype)


def _dsa_call(qa, ka, va, iq, ik, iw, tq):
    b, s, _ = qa.shape
    topk = min(TOPK_MAX, s // 4)
    qrow = lambda width: pl.BlockSpec((None, tq, width), lambda bi, i: (bi, i, 0))
    seq = lambda width: pl.BlockSpec((None, s, width), lambda bi, i: (bi, 0, 0),
                                     pipeline_mode=pl.Buffered(1))
    return pl.pallas_call(
        functools.partial(_dsa_kernel, tq=tq, topk=topk),
        grid=(b, s // tq),
        in_specs=[qrow(D_A), seq(D_A), seq(D_A), qrow(IDX_HEADS * IDX_DIM), seq(LANES), qrow(LANES)],
        out_specs=qrow(D_A),
        out_shape=jax.ShapeDtypeStruct((b, s, D_A), BF16),
        scratch_shapes=[
            pltpu.VMEM((s // LANES, tq, LANES), I32),
            pltpu.VMEM((IDX_HEADS * tq, LANES), BF16),
            pltpu.VMEM((IDX_HEADS * tq, LANES), F32),
            pltpu.VMEM((tq, LANES), I32),
            pltpu.VMEM((tq, LANES), I32),
            pltpu.VMEM((tq, LANES), F32),
            pltpu.VMEM((2 * tq, 1), F32),
            pltpu.VMEM((2 * tq, 1), F32),
            pltpu.VMEM((2 * tq, LANES), F32),
        ],
        compiler_params=pltpu.CompilerParams(
            dimension_semantics=("arbitrary", "arbitrary"), vmem_limit_bytes=VMEM_LIMIT),
        name="dsa",
    )(qa, ka, va, iq, ik, iw)


def _diff_kernel(q_ref, k_ref, v_ref, lam_ref, g_ref, o_ref, m_ref, l_ref, acc_ref,
                 *, tq, lam_init):
    tk = tq
    j = pl.program_id(2)
    q0 = j * tq
    lane = lax.broadcasted_iota(I32, (tq, LANES), 1)
    col = lax.broadcasted_iota(I32, (tq, tk), 1)
    row = lax.broadcasted_iota(I32, (tq, 1), 0)
    q_end = (((q0 + row) >> 6) + 1) << 6

    q_st = _split_lane_halves(q_ref[...].astype(F32), lane)
    m_ref[...] = jnp.full(m_ref.shape, NEG, F32)
    l_ref[...] = jnp.zeros(l_ref.shape, F32)
    acc_ref[...] = jnp.zeros(acc_ref.shape, F32)

    def att_tile(kt, masked):
        r0 = pl.multiple_of(kt * tk, tk)
        s = lax.dot_general(q_st, k_ref[pl.ds(r0, tk), :], _NT, preferred_element_type=F32)
        if masked:
            adm = (kt * tk + col) < q_end
            s = jnp.concatenate([jnp.where(adm, s[:tq], NEG), jnp.where(adm, s[tq:], NEG)], axis=0)
        _online_softmax_step(s, v_ref[pl.ds(r0, tk), :], m_ref, l_ref, acc_ref)

    def full_tile(kt, carry):
        att_tile(kt, False)
        return carry

    lax.fori_loop(0, j, full_tile, 0)
    att_tile(j, True)

    lam_p = lam_ref[...]
    lam = (jnp.exp(jnp.sum(lam_p[0:1] * lam_p[1:2], axis=1, keepdims=True))
           - jnp.exp(jnp.sum(lam_p[2:3] * lam_p[3:4], axis=1, keepdims=True)) + lam_init)
    o = acc_ref[...] / l_ref[...]
    o = o[:tq] - lam * o[tq:]
    o_ref[...] = (_rms(o, g_ref[...]) * (1.0 - lam_init)).astype(o_ref.dtype)


def _diff_call(qb, kb, vb, lam_p, g_subln, lam_init, tq):
    b, s, _ = qb.shape
    return pl.pallas_call(
        functools.partial(_diff_kernel, tq=tq, lam_init=lam_init),
        grid=(b, B_HEADS, s // tq),
        in_specs=[
            pl.BlockSpec((None, tq, LANES), lambda bi, h, i: (bi, i, h)),
            pl.BlockSpec((None, s, LANES), lambda bi, h, i: (bi, 0, h)),
            pl.BlockSpec((None, s, LANES), lambda bi, h, i: (bi, 0, h)),
            pl.BlockSpec((4, B_HEAD_DIM), lambda bi, h, i: (0, 0)),
            pl.BlockSpec((1, LANES), lambda bi, h, i: (0, 0)),
        ],
        out_specs=pl.BlockSpec((None, tq, LANES), lambda bi, h, i: (bi, i, h)),
        out_shape=jax.ShapeDtypeStruct((b, s, D_B), BF16),
        scratch_shapes=[
            pltpu.VMEM((2 * tq, 1), F32),
            pltpu.VMEM((2 * tq, 1), F32),
            pltpu.VMEM((2 * tq, LANES), F32),
        ],
        compiler_params=pltpu.CompilerParams(
            dimension_semantics=("arbitrary", "arbitrary", "arbitrary"),
            vmem_limit_bytes=VMEM_LIMIT),
        name="diff",
    )(qb, kb, vb, lam_p, g_subln)


def _ffn_kernel(x_ref, a_ref, b_ref, wo_ref, g_ref, wg_ref, wu_ref, wd_ref, gf_ref, o_ref,
                *, ff_chunk, final_norm):
    d_a = a_ref.shape[1]
    x1 = (x_ref[...]
          + jnp.dot(a_ref[...], wo_ref[:d_a, :], preferred_element_type=F32)
          + jnp.dot(b_ref[...], wo_ref[d_a:, :], preferred_element_type=F32))
    h2 = _rms(x1, g_ref[...]).astype(BF16)
    y = None
    for c in range(wg_ref.shape[1] // ff_chunk):
        cs = slice(c * ff_chunk, (c + 1) * ff_chunk)
        gate = jnp.dot(h2, wg_ref[:, cs], preferred_element_type=F32)
        up = jnp.dot(h2, wu_ref[:, cs], preferred_element_type=F32)
        act = (gate / (1.0 + jnp.exp(-gate)) * up).astype(BF16)
        down = jnp.dot(act, wd_ref[cs, :], preferred_element_type=F32)
        y = down if y is None else y + down
    x2 = x1 + y
    if final_norm:
        x2 = _rms(x2, gf_ref[...])
    o_ref[...] = x2


def _ffn_call(x, out_a, out_b, wo, g, wg, wu, wd, g_final, final_norm, tm, ff_chunk):
    t, d = x.shape
    d_ff = wg.shape[1]
    row = lambda width: pl.BlockSpec((tm, width), lambda i: (i, 0))
    const = lambda shape: pl.BlockSpec(shape, lambda i: (0, 0), pipeline_mode=pl.Buffered(1))
    return pl.pallas_call(
        functools.partial(_ffn_kernel, ff_chunk=ff_chunk, final_norm=final_norm),
        grid=(t // tm,),
        in_specs=[row(d), row(out_a.shape[1]), row(out_b.shape[1]), const(wo.shape), const((1, d)),
                  const((d, d_ff)), const((d, d_ff)), const((d_ff, d)), const((1, d))],
        out_specs=row(d),
        out_shape=jax.ShapeDtypeStruct((t, d), F32),
        compiler_params=pltpu.CompilerParams(
            dimension_semantics=("arbitrary",), vmem_limit_bytes=VMEM_LIMIT),
        name="ffn",
    )(x, out_a, out_b, wo, g, wg, wu, wd, g_final)


def _rope_tables(seq_len, dim):
    pos = jnp.arange(seq_len, dtype=F32)
    inv = ROPE_THETA ** (-jnp.arange(0, dim, 2, dtype=F32) / dim)
    ang = pos[:, None] * inv[None, :]
    cos, sin = jnp.cos(ang), jnp.sin(ang)
    reps = LANES // dim
    return (jnp.tile(jnp.concatenate([cos, cos], axis=1), (1, reps)),
            jnp.tile(jnp.concatenate([-sin, sin], axis=1), (1, reps)))


def _pack_w_in(w):
    splits = (D_A, D_A, D_A, IDX_HEADS * IDX_DIM, IDX_DIM, IDX_HEADS, D_B, D_B, D_B)
    offs = [0]
    for sz in splits:
        offs.append(offs[-1] + sz)
    qa, ka, va, iq, ik, iw, qb, kb, vb = (w[:, offs[i]:offs[i + 1]] for i in range(len(splits)))
    ik4 = jnp.tile(ik, (1, LANES // IDX_DIM))
    iw_p = jnp.pad(iw, ((0, 0), (0, LANES - IDX_HEADS)))
    return jnp.concatenate([qa, ka, va, iq, ik4, iw_p, qb, kb, vb], axis=1).astype(BF16)


def kernel(x, w_in, w_out, g_mix, lam_q1, lam_k1, lam_q2, lam_k2, g_subln, g_ffn, w_gate, w_up,
           w_down, g_final):
    b, s, d = x.shape
    depth = w_in.shape[0]
    tm_proj, tq, tm_ffn = 512, 256, 512
    d_ff = w_gate.shape[-1]
    ff_chunk = d_ff // 2
    assert s % tm_proj == 0 and s % tq == 0 and (b * s) % tm_ffn == 0 and ff_chunk % LANES == 0

    tabs = _rope_tables(s, A_HEAD_DIM) + _rope_tables(s, IDX_DIM)
    for layer in range(depth):
        qa, ka, va, iq, ik, iw, qb, kb, vb = _proj_call(
            x, g_mix[layer][None, :], _pack_w_in(w_in[layer]), tabs, tm_proj)
        out_a = _dsa_call(qa, ka, va, iq, ik, iw, tq)
        lam_init = 0.8 - 0.6 * math.exp(-0.3 * layer)
        lam_p = jnp.stack([lam_q1[layer], lam_k1[layer], lam_q2[layer], lam_k2[layer]]).astype(F32)
        out_b = _diff_call(qb, kb, vb, lam_p, g_subln[layer][None, :], lam_init, tq)
        x = _ffn_call(
            x.reshape(b * s, d), out_a.reshape(b * s, D_A), out_b.reshape(b * s, D_B),
            w_out[layer].astype(BF16), g_ffn[layer][None, :], w_gate[layer].astype(BF16),
            w_up[layer].astype(BF16), w_down[layer].astype(BF16), g_final[None, :],
            layer == depth - 1, tm_ffn, ff_chunk).reshape(b, s, d)
    return x
```

```python
import functools
import math

import jax
import jax.numpy as jnp
import numpy as np
from jax import lax
from jax.experimental import pallas as pl
from jax.experimental.pallas import tpu as pltpu

F32 = jnp.float32
BF16 = jnp.bfloat16
I32 = jnp.int32

CHUNK = 64
ROPE_THETA = 10000.0
RMS_EPS = 1e-6
A_HEADS = 8
A_HEAD_DIM = 64
D_A = A_HEADS * A_HEAD_DIM
IDX_HEADS = 8
IDX_DIM = 32
TOPK_MAX = 256
B_HEADS = 4
B_HEAD_DIM = 64
D_B = B_HEADS * 2 * B_HEAD_DIM

LANES = 128
VMEM_LIMIT = 56 * 1024 * 1024

LOG2E = 1.4426950408889634
NEG = -1e30
NEG_BITS = int(np.float32(NEG).view(np.int32))
KEY_MIN = -(2 ** 31)

_OFF_QA, _OFF_KA, _OFF_VA = 0, 512, 1024
_OFF_IQ, _OFF_IK, _OFF_IW = 1536, 1792, 1920
_OFF_QB, _OFF_KB, _OFF_VB = 2048, 2560, 3072
_W_CAT = 3584

_NT = (((1,), (1,)), ((), ()))


def _rms(x, g):
    r = lax.rsqrt(jnp.mean(x * x, axis=-1, keepdims=True) + RMS_EPS)
    return x * r * g


def _proj_kernel(x_ref, g_ref, w_ref, cos_a_ref, sin_a_ref, cos_i_ref, sin_i_ref,
                 qa_ref, ka_ref, va_ref, iq_ref, ik_ref, iw_ref, qb_ref, kb_ref, vb_ref):
    h = _rms(x_ref[...], g_ref[...]).astype(BF16)
    tm = h.shape[0]
    lane = lax.broadcasted_iota(I32, (tm, LANES), 1)

    def proj(off, width):
        return jnp.dot(h, w_ref[:, off:off + width], preferred_element_type=F32)

    def rope_tile(y, cos, sin_signed, half):
        first = (lane & (2 * half - 1)) < half
        rot = jnp.where(first, pltpu.roll(y, LANES - half, 1), pltpu.roll(y, half, 1))
        return y * cos + rot * sin_signed

    def rope_store(out_ref, off, width, cos_ref, sin_ref, half, scale):
        y = proj(off, width)
        cos, sin = cos_ref[...], sin_ref[...]
        for t in range(width // LANES):
            r = rope_tile(y[:, t * LANES:(t + 1) * LANES], cos, sin, half)
            if scale != 1.0:
                r = r * scale
            out_ref[:, t * LANES:(t + 1) * LANES] = r.astype(out_ref.dtype)

    half_a, half_i = A_HEAD_DIM // 2, IDX_DIM // 2
    rope_store(qa_ref, _OFF_QA, D_A, cos_a_ref, sin_a_ref, half_a, A_HEAD_DIM ** -0.5 * LOG2E)
    rope_store(ka_ref, _OFF_KA, D_A, cos_a_ref, sin_a_ref, half_a, 1.0)
    va_ref[...] = proj(_OFF_VA, D_A).astype(va_ref.dtype)
    rope_store(iq_ref, _OFF_IQ, IDX_HEADS * IDX_DIM, cos_i_ref, sin_i_ref, half_i, 1.0)
    rope_store(ik_ref, _OFF_IK, LANES, cos_i_ref, sin_i_ref, half_i, 1.0)
    iw_ref[...] = proj(_OFF_IW, LANES) * (IDX_HEADS ** -0.5 * IDX_DIM ** -0.5)
    rope_store(qb_ref, _OFF_QB, D_B, cos_a_ref, sin_a_ref, half_a, B_HEAD_DIM ** -0.5 * LOG2E)
    rope_store(kb_ref, _OFF_KB, D_B, cos_a_ref, sin_a_ref, half_a, 1.0)
    vb_ref[...] = proj(_OFF_VB, D_B).astype(vb_ref.dtype)


def _proj_call(x, g, w_cat, tabs, tm):
    b, s, d = x.shape
    cos_a, sin_a, cos_i, sin_i = tabs
    row = lambda width: pl.BlockSpec((None, tm, width), lambda bi, i: (bi, i, 0))
    tab = pl.BlockSpec((tm, LANES), lambda bi, i: (i, 0))
    const = lambda shape: pl.BlockSpec(shape, lambda bi, i: (0, 0), pipeline_mode=pl.Buffered(1))
    widths = (D_A, D_A, D_A, IDX_HEADS * IDX_DIM, LANES, LANES, D_B, D_B, D_B)
    dtypes = (BF16, BF16, BF16, BF16, BF16, F32, BF16, BF16, BF16)
    return pl.pallas_call(
        _proj_kernel,
        grid=(b, s // tm),
        in_specs=[row(d), const((1, d)), const((d, _W_CAT)), tab, tab, tab, tab],
        out_specs=[row(w) for w in widths],
        out_shape=[jax.ShapeDtypeStruct((b, s, w), dt) for w, dt in zip(widths, dtypes)],
        compiler_params=pltpu.CompilerParams(
            dimension_semantics=("arbitrary", "arbitrary"), vmem_limit_bytes=VMEM_LIMIT),
        name="proj",
    )(x, g, w_cat, cos_a, sin_a, cos_i, sin_i)


def _online_softmax_step(s, v_t, m_ref, l_ref, acc_ref):
    chunks = [s[:, c * LANES:(c + 1) * LANES] for c in range(s.shape[1] // LANES)]
    m_prev = m_ref[...]
    m_cur = functools.reduce(jnp.maximum, chunks)
    m_new = jnp.maximum(m_prev, jnp.max(m_cur, axis=1, keepdims=True))
    alpha = jnp.exp2(m_prev - m_new)
    p = [jnp.exp2(c - m_new) for c in chunks]
    l_ref[...] = alpha * l_ref[...] + functools.reduce(jnp.add, p)
    pv = jnp.dot(jnp.concatenate(p, axis=1).astype(BF16), v_t, preferred_element_type=F32)
    acc_ref[...] = alpha * acc_ref[...] + pv
    m_ref[...] = m_new


def _softmax_finish(l_ref, acc_ref):
    return acc_ref[...] / jnp.sum(l_ref[...], axis=1, keepdims=True)


def _split_lane_halves(tile_f32, lane):
    lo = jnp.where(lane < LANES // 2, tile_f32, 0.0)
    hi = jnp.where(lane >= LANES // 2, tile_f32, 0.0)
    return jnp.concatenate([lo, hi], axis=0).astype(BF16)


def _dsa_kernel(q_ref, k_ref, v_ref, iq_ref, ik_ref, iw_ref, o_ref,
                key_ref, iqm_ref, wb_ref, mn_ref, mx_ref, thr_ref, jcut_ref, m_ref, l_ref, acc_ref,
                *, tq, topk):
    tk = tq
    nch = tk // LANES
    tki = 256
    rb = 128
    j = pl.program_id(1)
    q0 = j * tq
    n_tiles = j + 1
    lane = lax.broadcasted_iota(I32, (tq, LANES), 1)
    lane_rb = lax.broadcasted_iota(I32, (rb, LANES), 1)
    row = lax.broadcasted_iota(I32, (tq, 1), 0)
    q_end = (((q0 + row) >> 6) + 1) << 6
    n_adm = q_end.astype(F32)
    k_eff = jnp.minimum(n_adm, float(topk))

    iw = iw_ref[...]
    for h in range(IDX_HEADS):
        g = h % 4
        tile = iq_ref[:, (h // 4) * LANES:(h // 4 + 1) * LANES].astype(F32)
        keep = (lane >= g * IDX_DIM) & (lane < (g + 1) * IDX_DIM)
        iqm_ref[h * tq:(h + 1) * tq, :] = jnp.where(keep, tile, 0.0).astype(BF16)
        wb_ref[h * tq:(h + 1) * tq, :] = jnp.broadcast_to(iw[:, h:h + 1], (tq, LANES))

    mn_ref[...] = jnp.full((tq, LANES), 2 ** 31 - 1, I32)
    mx_ref[...] = jnp.full((tq, LANES), KEY_MIN, I32)

    def index_tile(it, masked):
        r0 = pl.multiple_of(it * tki, tki)
        d = lax.dot_general(iqm_ref[...], ik_ref[pl.ds(r0, tki), :], _NT,
                            preferred_element_type=F32)
        for c in range(tki // LANES):
            for r in range(tq // rb):
                rs = slice(r * rb, (r + 1) * rb)
                a = jnp.zeros((rb, LANES), F32)
                for h in range(IDX_HEADS):
                    hs = slice(h * tq + r * rb, h * tq + (r + 1) * rb)
                    a = a + wb_ref[hs, :] * jnp.maximum(d[hs, c * LANES:(c + 1) * LANES], 0.0)
                bits = lax.bitcast_convert_type(a, I32)
                key = bits ^ ((bits >> 31) & 0x7FFFFFFF)
                key_lo = key
                if masked:
                    adm = (it * tki + c * LANES + lane_rb) < q_end[rs]
                    key_lo = jnp.where(adm, key, 2 ** 31 - 1)
                    key = jnp.where(adm, key, KEY_MIN)
                mn_ref[rs, :] = jnp.minimum(mn_ref[rs, :], key_lo)
                mx_ref[rs, :] = jnp.maximum(mx_ref[rs, :], key)
                key_ref[it * (tki // LANES) + c, rs, :] = key

    def full_tile(it, carry):
        index_tile(it, False)
        return carry

    n_full = q0 // tki
    lax.fori_loop(0, n_full, full_tile, 0)
    for t in range(tq // tki):
        index_tile(n_full + t, True)

    def count_pass(per_block):
        outs = []
        for r in range(tq // rb):
            rs = slice(r * rb, (r + 1) * rb)
            pred = per_block(rs)

            def body(kt, acc):
                for c in range(nch):
                    hit = pred(key_ref[kt * nch + c, rs, :], kt * tk + c * LANES)
                    acc = acc + jnp.where(hit, 1.0, 0.0)
                return acc

            acc = lax.fori_loop(0, n_tiles, body, jnp.zeros((rb, LANES), F32))
            outs.append(jnp.sum(acc, axis=1, keepdims=True))
        return jnp.concatenate(outs, axis=0)

    def count_ge(t):
        thr_ref[...] = jnp.broadcast_to(t, (tq, LANES))

        def per_block(rs):
            tb = thr_ref[rs, :]
            return lambda keys, kidx0: keys >= tb

        return count_pass(per_block)

    def flip(k):
        return k ^ ((k >> 31) & 0x7FFFFFFF)

    lo0 = jnp.min(mn_ref[...], axis=1, keepdims=True)
    hi0 = jnp.max(mx_ref[...], axis=1, keepdims=True)
    c_hi0 = jnp.zeros((tq, 1), F32)
    value_steps = 24

    def bis_cond(st):
        it, lo, hi, c_lo, c_above = st
        open_rows = jnp.where((c_lo > k_eff) & (lo < hi), 1.0, 0.0)
        return jnp.max(open_rows) > 0.0

    def bis_body(st):
        it, lo, hi, c_lo, c_above = st
        mid_k = (lo >> 1) + (hi >> 1) + ((lo | hi) & 1)
        v_lo = lax.bitcast_convert_type(flip(lo), F32)
        v_hi = lax.bitcast_convert_type(flip(hi), F32)
        mid_v = flip(lax.bitcast_convert_type(0.5 * v_lo + 0.5 * v_hi, I32))
        mid_v = jnp.minimum(jnp.maximum(mid_v, lo + 1), hi)
        mid = jnp.where(it < value_steps, mid_v, mid_k)
        active = (c_lo > k_eff) & (lo < hi)
        c_mid = count_ge(mid)
        go_up = active & (c_mid >= k_eff)
        go_dn = active & (c_mid < k_eff)
        lo = jnp.where(go_up, mid, lo)
        c_lo = jnp.where(go_up, c_mid, c_lo)
        hi = jnp.where(go_dn, mid - 1, hi)
        c_above = jnp.where(go_dn, c_mid, c_above)
        return it + 1, lo, hi, c_lo, c_above

    _, thr, _, c_thr, c_above = lax.while_loop(
        bis_cond, bis_body, (jnp.int32(0), lo0, hi0, n_adm, c_hi0))
    thr_ref[...] = jnp.broadcast_to(thr, (tq, LANES))
    need = k_eff - c_above
    has_tie = c_thr > k_eff
    jcut_ref[...] = jnp.full((tq, LANES), 2.0 ** 30, F32)

    @pl.when(jnp.max(jnp.where(has_tie, 1.0, 0.0)) > 0.0)
    def _():
        def jb_body(_, st):
            jl, jh = st
            jm = jnp.floor((jl + jh) * 0.5)
            jcut_ref[...] = jnp.broadcast_to(jm, (tq, LANES))

            def per_block(rs):
                tb, jb = thr_ref[rs, :], jcut_ref[rs, :]
                return lambda keys, kidx0: (keys == tb) & ((kidx0 + lane_rb).astype(F32) < jb)

            ok = count_pass(per_block) >= need
            return jnp.where(ok, jl, jm), jnp.where(ok, jm, jh)

        n_keys = (n_tiles * tk).astype(F32)
        jl0 = jnp.zeros((tq, 1), F32)
        jh0 = jnp.zeros((tq, 1), F32) + n_keys
        steps = int(math.ceil(math.log2(key_ref.shape[0] * LANES))) + 1
        _, jh = lax.fori_loop(0, steps, jb_body, (jl0, jh0))
        jcut_ref[...] = jnp.broadcast_to(jnp.where(has_tie, jh, 2.0 ** 30), (tq, LANES))

    def bias_tile(kt, carry):
        for c in range(nch):
            keys = key_ref[kt * nch + c]
            kidx = (kt * tk + c * LANES + lane).astype(F32)
            thr_b = thr_ref[...]
            sel = (keys > thr_b) | ((keys == thr_b) & (kidx < jcut_ref[...]))
            key_ref[kt * nch + c] = jnp.where(sel, 0, NEG_BITS)
        return carry

    lax.fori_loop(0, n_tiles, bias_tile, 0)

    for hp in range(A_HEADS // 2):
        ls = slice(hp * LANES, (hp + 1) * LANES)
        q_st = _split_lane_halves(q_ref[:, ls].astype(F32), lane)
        m_ref[...] = jnp.full(m_ref.shape, NEG, F32)
        l_ref[...] = jnp.zeros(l_ref.shape, F32)
        acc_ref[...] = jnp.zeros(acc_ref.shape, F32)

        def att_tile(kt, carry):
            r0 = pl.multiple_of(kt * tk, tk)
            s = lax.dot_general(q_st, k_ref[pl.ds(r0, tk), ls], _NT, preferred_element_type=F32)
            bias = jnp.concatenate(
                [lax.bitcast_convert_type(key_ref[kt * nch + c], F32) for c in range(nch)], axis=1)
            s = jnp.concatenate([s[:tq] + bias, s[tq:] + bias], axis=0)
            _online_softmax_step(s, v_ref[pl.ds(r0, tk), ls], m_ref, l_ref, acc_ref)
            return carry

        lax.fori_loop(0, n_tiles, att_tile, 0)
        o = _softmax_finish(l_ref, acc_ref)
        o_ref[:, ls] = jnp.where(lane < LANES // 2, o[:tq], o[tq:]).astype(o_ref.dtype)


def _dsa_call(qa, ka, va, iq, ik, iw, tq):
    b, s, _ = qa.shape
    topk = min(TOPK_MAX, s // 4)
    qrow = lambda width: pl.BlockSpec((None, tq, width), lambda bi, i: (bi, i, 0))
    seq = lambda width: pl.BlockSpec((None, s, width), lambda bi, i: (bi, 0, 0),
                                     pipeline_mode=pl.Buffered(1))
    return pl.pallas_call(
        functools.partial(_dsa_kernel, tq=tq, topk=topk),
        grid=(b, s // tq),
        in_specs=[qrow(D_A), seq(D_A), seq(D_A), qrow(IDX_HEADS * IDX_DIM), seq(LANES), qrow(LANES)],
        out_specs=qrow(D_A),
        out_shape=jax.ShapeDtypeStruct((b, s, D_A), BF16),
        scratch_shapes=[
            pltpu.VMEM((s // LANES, tq, LANES), I32),
            pltpu.VMEM((IDX_HEADS * tq, LANES), BF16),
            pltpu.VMEM((IDX_HEADS * tq, LANES), F32),
            pltpu.VMEM((tq, LANES), I32),
            pltpu.VMEM((tq, LANES), I32),
            pltpu.VMEM((tq, LANES), I32),
            pltpu.VMEM((tq, LANES), F32),
            pltpu.VMEM((2 * tq, LANES), F32),
            pltpu.VMEM((2 * tq, LANES), F32),
            pltpu.VMEM((2 * tq, LANES), F32),
        ],
        compiler_params=pltpu.CompilerParams(
            dimension_semantics=("arbitrary", "arbitrary"), vmem_limit_bytes=VMEM_LIMIT),
        name="dsa",
    )(qa, ka, va, iq, ik, iw)


def _diff_kernel(q_ref, k_ref, v_ref, lam_ref, g_ref, o_ref, m_ref, l_ref, acc_ref,
                 *, tq, lam_init):
    tk = tq
    j = pl.program_id(2)
    q0 = j * tq
    lane = lax.broadcasted_iota(I32, (tq, LANES), 1)
    col = lax.broadcasted_iota(I32, (tq, tk), 1)
    row = lax.broadcasted_iota(I32, (tq, 1), 0)
    q_end = (((q0 + row) >> 6) + 1) << 6

    q_st = _split_lane_halves(q_ref[...].astype(F32), lane)
    m_ref[...] = jnp.full(m_ref.shape, NEG, F32)
    l_ref[...] = jnp.zeros(l_ref.shape, F32)
    acc_ref[...] = jnp.zeros(acc_ref.shape, F32)

    def att_tile(kt, masked):
        r0 = pl.multiple_of(kt * tk, tk)
        s = lax.dot_general(q_st, k_ref[pl.ds(r0, tk), :], _NT, preferred_element_type=F32)
        if masked:
            adm = (kt * tk + col) < q_end
            s = jnp.concatenate([jnp.where(adm, s[:tq], NEG), jnp.where(adm, s[tq:], NEG)], axis=0)
        _online_softmax_step(s, v_ref[pl.ds(r0, tk), :], m_ref, l_ref, acc_ref)

    def full_tile(kt, carry):
        att_tile(kt, False)
        return carry

    lax.fori_loop(0, j, full_tile, 0)
    att_tile(j, True)

    lam_p = lam_ref[...]
    lam = (jnp.exp(jnp.sum(lam_p[0:1] * lam_p[1:2], axis=1, keepdims=True))
           - jnp.exp(jnp.sum(lam_p[2:3] * lam_p[3:4], axis=1, keepdims=True)) + lam_init)
    o = _softmax_finish(l_ref, acc_ref)
    o = o[:tq] - lam * o[tq:]
    o_ref[...] = (_rms(o, g_ref[...]) * (1.0 - lam_init)).astype(o_ref.dtype)


def _diff_call(qb, kb, vb, lam_p, g_subln, lam_init, tq):
    b, s, _ = qb.shape
    return pl.pallas_call(
        functools.partial(_diff_kernel, tq=tq, lam_init=lam_init),
        grid=(b, B_HEADS, s // tq),
        in_specs=[
            pl.BlockSpec((None, tq, LANES), lambda bi, h, i: (bi, i, h)),
            pl.BlockSpec((None, s, LANES), lambda bi, h, i: (bi, 0, h)),
            pl.BlockSpec((None, s, LANES), lambda bi, h, i: (bi, 0, h)),
            pl.BlockSpec((4, B_HEAD_DIM), lambda bi, h, i: (0, 0)),
            pl.BlockSpec((1, LANES), lambda bi, h, i: (0, 0)),
        ],
        out_specs=pl.BlockSpec((None, tq, LANES), lambda bi, h, i: (bi, i, h)),
        out_shape=jax.ShapeDtypeStruct((b, s, D_B), BF16),
        scratch_shapes=[
            pltpu.VMEM((2 * tq, LANES), F32),
            pltpu.VMEM((2 * tq, LANES), F32),
            pltpu.VMEM((2 * tq, LANES), F32),
        ],
        compiler_params=pltpu.CompilerParams(
            dimension_semantics=("arbitrary", "arbitrary", "arbitrary"),
            vmem_limit_bytes=VMEM_LIMIT),
        name="diff",
    )(qb, kb, vb, lam_p, g_subln)


def _ffn_kernel(x_ref, a_ref, b_ref, wo_ref, g_ref, wg_ref, wu_ref, wd_ref, gf_ref, o_ref,
                *, ff_chunk, final_norm):
    d_a = a_ref.shape[1]
    x1 = (x_ref[...]
          + jnp.dot(a_ref[...], wo_ref[:d_a, :], preferred_element_type=F32)
          + jnp.dot(b_ref[...], wo_ref[d_a:, :], preferred_element_type=F32))
    h2 = _rms(x1, g_ref[...]).astype(BF16)
    y = None
    for c in range(wg_ref.shape[1] // ff_chunk):
        cs = slice(c * ff_chunk, (c + 1) * ff_chunk)
        gate = jnp.dot(h2, wg_ref[:, cs], preferred_element_type=F32)
        up = jnp.dot(h2, wu_ref[:, cs], preferred_element_type=F32)
        act = (gate / (1.0 + jnp.exp(-gate)) * up).astype(BF16)
        down = jnp.dot(act, wd_ref[cs, :], preferred_element_type=F32)
        y = down if y is None else y + down
    x2 = x1 + y
    if final_norm:
        x2 = _rms(x2, gf_ref[...])
    o_ref[...] = x2


def _ffn_call(x, out_a, out_b, wo, g, wg, wu, wd, g_final, final_norm, tm, ff_chunk):
    t, d = x.shape
    d_ff = wg.shape[1]
    row = lambda width: pl.BlockSpec((tm, width), lambda i: (i, 0))
    const = lambda shape: pl.BlockSpec(shape, lambda i: (0, 0), pipeline_mode=pl.Buffered(1))
    return pl.pallas_call(
        functools.partial(_ffn_kernel, ff_chunk=ff_chunk, final_norm=final_norm),
        grid=(t // tm,),
        in_specs=[row(d), row(out_a.shape[1]), row(out_b.shape[1]), const(wo.shape), const((1, d)),
                  const((d, d_ff)), const((d, d_ff)), const((d_ff, d)), const((1, d))],
        out_specs=row(d),
        out_shape=jax.ShapeDtypeStruct((t, d), F32),
        compiler_params=pltpu.CompilerParams(
            dimension_semantics=("arbitrary",), vmem_limit_bytes=VMEM_LIMIT),
        name="ffn",
    )(x, out_a, out_b, wo, g, wg, wu, wd, g_final)


def _rope_tables(seq_len, dim):
    pos = jnp.arange(seq_len, dtype=F32)
    inv = ROPE_THETA ** (-jnp.arange(0, dim, 2, dtype=F32) / dim)
    ang = pos[:, None] * inv[None, :]
    cos, sin = jnp.cos(ang), jnp.sin(ang)
    reps = LANES // dim
    return (jnp.tile(jnp.concatenate([cos, cos], axis=1), (1, reps)),
            jnp.tile(jnp.concatenate([-sin, sin], axis=1), (1, reps)))


def _pack_w_in(w):
    splits = (D_A, D_A, D_A, IDX_HEADS * IDX_DIM, IDX_DIM, IDX_HEADS, D_B, D_B, D_B)
    offs = [0]
    for sz in splits:
        offs.append(offs[-1] + sz)
    qa, ka, va, iq, ik, iw, qb, kb, vb = (w[:, offs[i]:offs[i + 1]] for i in range(len(splits)))
    ik4 = jnp.tile(ik, (1, LANES // IDX_DIM))
    iw_p = jnp.pad(iw, ((0, 0), (0, LANES - IDX_HEADS)))
    return jnp.concatenate([qa, ka, va, iq, ik4, iw_p, qb, kb, vb], axis=1).astype(BF16)


def kernel(x, w_in, w_out, g_mix, lam_q1, lam_k1, lam_q2, lam_k2, g_subln, g_ffn, w_gate, w_up,
           w_down, g_final):
    b, s, d = x.shape
    depth = w_in.shape[0]
    tm_proj, tq, tm_ffn = 512, 512, 512
    d_ff = w_gate.shape[-1]
    ff_chunk = d_ff // 2
    assert s % tm_proj == 0 and s % tq == 0 and (b * s) % tm_ffn == 0 and ff_chunk % LANES == 0

    tabs = _rope_tables(s, A_HEAD_DIM) + _rope_tables(s, IDX_DIM)
    for layer in range(depth):
        qa, ka, va, iq, ik, iw, qb, kb, vb = _proj_call(
            x, g_mix[layer][None, :], _pack_w_in(w_in[layer]), tabs, tm_proj)
        out_a = _dsa_call(qa, ka, va, iq, ik, iw, tq)
        lam_init = 0.8 - 0.6 * math.exp(-0.3 * layer)
        lam_p = jnp.stack([lam_q1[layer], lam_k1[layer], lam_q2[layer], lam_k2[layer]]).astype(F32)
        out_b = _diff_call(qb, kb, vb, lam_p, g_subln[layer][None, :], lam_init, tq)
        x = _ffn_call(
            x.reshape(b * s, d), out_a.reshape(b * s, D_A), out_b.reshape(b * s, D_B),
            w_out[layer].astype(BF16), g_ffn[layer][None, :], w_gate[layer].astype(BF16),
            w_up[layer].astype(BF16), w_down[layer].astype(BF16), g_final[None, :],
            layer == depth - 1, tm_ffn, ff_chunk).reshape(b, s, d)
    return x
```

```python
import functools
import math

import jax
import jax.numpy as jnp
import numpy as np
from jax import lax
from jax.experimental import pallas as pl
from jax.experimental.pallas import tpu as pltpu

F32 = jnp.float32
BF16 = jnp.bfloat16
I32 = jnp.int32

CHUNK = 64
ROPE_THETA = 10000.0
RMS_EPS = 1e-6
A_HEADS = 8
A_HEAD_DIM = 64
D_A = A_HEADS * A_HEAD_DIM
IDX_HEADS = 8
IDX_DIM = 32
TOPK_MAX = 256
B_HEADS = 4
B_HEAD_DIM = 64
D_B = B_HEADS * 2 * B_HEAD_DIM

LANES = 128
VMEM_LIMIT = 56 * 1024 * 1024

LOG2E = 1.4426950408889634
NEG = -1e30
NEG_BITS = int(np.float32(NEG).view(np.int32))
KEY_MIN = -(2 ** 31)

_OFF_QA, _OFF_KA, _OFF_VA = 0, 512, 1024
_OFF_IQ, _OFF_IK, _OFF_IW = 1536, 1792, 1920
_OFF_QB, _OFF_KB, _OFF_VB = 2048, 2560, 3072
_W_CAT = 3584

_NT = (((1,), (1,)), ((), ()))


def _rms(x, g):
    r = lax.rsqrt(jnp.mean(x * x, axis=-1, keepdims=True) + RMS_EPS)
    return x * r * g


def _proj_kernel(x_ref, g_ref, w_ref, cos_a_ref, sin_a_ref, cos_i_ref, sin_i_ref,
                 qa_ref, ka_ref, va_ref, iq_ref, ik_ref, iw_ref, qb_ref, kb_ref, vb_ref):
    h = _rms(x_ref[...], g_ref[...]).astype(BF16)
    tm = h.shape[0]
    lane = lax.broadcasted_iota(I32, (tm, LANES), 1)

    def proj(off, width):
        return jnp.dot(h, w_ref[:, off:off + width], preferred_element_type=F32)

    def rope_tile(y, cos, sin_signed, half):
        first = (lane & (2 * half - 1)) < half
        rot = jnp.where(first, pltpu.roll(y, LANES - half, 1), pltpu.roll(y, half, 1))
        return y * cos + rot * sin_signed

    def rope_store(out_ref, off, width, cos_ref, sin_ref, half, scale):
        y = proj(off, width)
        cos, sin = cos_ref[...], sin_ref[...]
        for t in range(width // LANES):
            r = rope_tile(y[:, t * LANES:(t + 1) * LANES], cos, sin, half)
            if scale != 1.0:
                r = r * scale
            out_ref[:, t * LANES:(t + 1) * LANES] = r.astype(out_ref.dtype)

    half_a, half_i = A_HEAD_DIM // 2, IDX_DIM // 2
    rope_store(qa_ref, _OFF_QA, D_A, cos_a_ref, sin_a_ref, half_a, A_HEAD_DIM ** -0.5 * LOG2E)
    rope_store(ka_ref, _OFF_KA, D_A, cos_a_ref, sin_a_ref, half_a, 1.0)
    va_ref[...] = proj(_OFF_VA, D_A).astype(va_ref.dtype)
    rope_store(iq_ref, _OFF_IQ, IDX_HEADS * IDX_DIM, cos_i_ref, sin_i_ref, half_i, 1.0)
    rope_store(ik_ref, _OFF_IK, LANES, cos_i_ref, sin_i_ref, half_i, 1.0)
    iw_ref[...] = proj(_OFF_IW, LANES) * (IDX_HEADS ** -0.5 * IDX_DIM ** -0.5)
    rope_store(qb_ref, _OFF_QB, D_B, cos_a_ref, sin_a_ref, half_a, B_HEAD_DIM ** -0.5 * LOG2E)
    rope_store(kb_ref, _OFF_KB, D_B, cos_a_ref, sin_a_ref, half_a, 1.0)
    vb_ref[...] = proj(_OFF_VB, D_B).astype(vb_ref.dtype)


def _proj_call(x, g, w_cat, tabs, tm):
    b, s, d = x.shape
    cos_a, sin_a, cos_i, sin_i = tabs
    row = lambda width: pl.BlockSpec((None, tm, width), lambda bi, i: (bi, i, 0))
    tab = pl.BlockSpec((tm, LANES), lambda bi, i: (i, 0))
    const = lambda shape: pl.BlockSpec(shape, lambda bi, i: (0, 0), pipeline_mode=pl.Buffered(1))
    widths = (D_A, D_A, D_A, IDX_HEADS * IDX_DIM, LANES, LANES, D_B, D_B, D_B)
    dtypes = (BF16, BF16, BF16, BF16, BF16, F32, BF16, BF16, BF16)
    return pl.pallas_call(
        _proj_kernel,
        grid=(b, s // tm),
        in_specs=[row(d), const((1, d)), const((d, _W_CAT)), tab, tab, tab, tab],
        out_specs=[row(w) for w in widths],
        out_shape=[jax.ShapeDtypeStruct((b, s, w), dt) for w, dt in zip(widths, dtypes)],
        compiler_params=pltpu.CompilerParams(
            dimension_semantics=("arbitrary", "arbitrary"), vmem_limit_bytes=VMEM_LIMIT),
        name="proj",
    )(x, g, w_cat, cos_a, sin_a, cos_i, sin_i)


def _online_softmax_step(s, v_t, m_ref, l_ref, acc_ref):
    chunks = [s[:, c * LANES:(c + 1) * LANES] for c in range(s.shape[1] // LANES)]
    m_prev = m_ref[...]
    m_cur = functools.reduce(jnp.maximum, chunks)
    m_new = jnp.maximum(m_prev, jnp.max(m_cur, axis=1, keepdims=True))
    alpha = jnp.exp2(m_prev - m_new)
    p = [jnp.exp2(c - m_new) for c in chunks]
    l_ref[...] = alpha * l_ref[...] + functools.reduce(jnp.add, p)
    pv = jnp.dot(jnp.concatenate(p, axis=1).astype(BF16), v_t, preferred_element_type=F32)
    acc_ref[...] = alpha * acc_ref[...] + pv
    m_ref[...] = m_new


def _softmax_finish(l_ref, acc_ref):
    return acc_ref[...] / jnp.sum(l_ref[...], axis=1, keepdims=True)


def _split_lane_halves(tile_f32, lane):
    lo = jnp.where(lane < LANES // 2, tile_f32, 0.0)
    hi = jnp.where(lane >= LANES // 2, tile_f32, 0.0)
    return jnp.concatenate([lo, hi], axis=0).astype(BF16)


def _dsa_kernel(q_ref, k_ref, v_ref, iq_ref, ik_ref, iw_ref, o_ref,
                key_ref, iqm_ref, wb_ref, mn_ref, mx_ref, thr_ref, jcut_ref, cnt_ref,
                m_ref, l_ref, acc_ref,
                *, tq, topk):
    tk = tq
    nch = tk // LANES
    tki = 256
    rb = 128
    j = pl.program_id(1)
    q0 = j * tq
    n_tiles = j + 1
    lane = lax.broadcasted_iota(I32, (tq, LANES), 1)
    lane_rb = lax.broadcasted_iota(I32, (rb, LANES), 1)
    row = lax.broadcasted_iota(I32, (tq, 1), 0)
    q_end = (((q0 + row) >> 6) + 1) << 6

    nb = tq // rb
    sub8 = lax.broadcasted_iota(I32, (8, LANES), 0)
    lane8 = lax.broadcasted_iota(I32, (8, LANES), 1)
    n_adm = ((((q0 + sub8 * rb + lane8) >> 6) + 1) << 6).astype(F32)
    k_eff = jnp.minimum(n_adm, float(topk))
    ones8 = jnp.ones((8, LANES), BF16)

    def rows_of(x8, r):
        x = jnp.broadcast_to(x8[r:r + 1, :], (rb, LANES))
        if x8.dtype == I32:
            return lax.bitcast_convert_type(jnp.transpose(lax.bitcast_convert_type(x, F32)), I32)
        return jnp.transpose(x)

    def lanes_of(col):
        out = jnp.zeros((8, LANES), col.dtype)
        for r in range(nb):
            x = jnp.broadcast_to(col[r * rb:(r + 1) * rb], (rb, LANES))
            if col.dtype == I32:
                x = lax.bitcast_convert_type(jnp.transpose(lax.bitcast_convert_type(x, F32)), I32)
            else:
                x = jnp.transpose(x)
            out = jnp.where(sub8 == r, x[0:8, :], out)
        return out

    iw = iw_ref[...]
    for h in range(IDX_HEADS):
        g = h % 4
        tile = iq_ref[:, (h // 4) * LANES:(h // 4 + 1) * LANES].astype(F32)
        keep = (lane >= g * IDX_DIM) & (lane < (g + 1) * IDX_DIM)
        iqm_ref[h * tq:(h + 1) * tq, :] = jnp.where(keep, tile, 0.0).astype(BF16)
        wb_ref[h * tq:(h + 1) * tq, :] = jnp.broadcast_to(iw[:, h:h + 1], (tq, LANES))

    mn_ref[...] = jnp.full((tq, LANES), 2 ** 31 - 1, I32)
    mx_ref[...] = jnp.full((tq, LANES), KEY_MIN, I32)

    def index_tile(it, masked):
        r0 = pl.multiple_of(it * tki, tki)
        d = lax.dot_general(iqm_ref[...], ik_ref[pl.ds(r0, tki), :], _NT,
                            preferred_element_type=F32)
        for c in range(tki // LANES):
            for r in range(tq // rb):
                rs = slice(r * rb, (r + 1) * rb)
                a = jnp.zeros((rb, LANES), F32)
                for h in range(IDX_HEADS):
                    hs = slice(h * tq + r * rb, h * tq + (r + 1) * rb)
                    a = a + wb_ref[hs, :] * jnp.maximum(d[hs, c * LANES:(c + 1) * LANES], 0.0)
                bits = lax.bitcast_convert_type(a, I32)
                key = bits ^ ((bits >> 31) & 0x7FFFFFFF)
                key_lo = key
                if masked:
                    adm = (it * tki + c * LANES + lane_rb) < q_end[rs]
                    key_lo = jnp.where(adm, key, 2 ** 31 - 1)
                    key = jnp.where(adm, key, KEY_MIN)
                mn_ref[rs, :] = jnp.minimum(mn_ref[rs, :], key_lo)
                mx_ref[rs, :] = jnp.maximum(mx_ref[rs, :], key)
                key_ref[it * (tki // LANES) + c, rs, :] = key

    def full_tile(it, carry):
        index_tile(it, False)
        return carry

    n_full = q0 // tki
    lax.fori_loop(0, n_full, full_tile, 0)
    for t in range(tq // tki):
        index_tile(n_full + t, True)

    def count_pass(per_block):
        for r in range(nb):
            rs = slice(r * rb, (r + 1) * rb)
            pred = per_block(rs)

            def body(kt, acc):
                for c in range(nch):
                    hit = pred(key_ref[kt * nch + c, rs, :], kt * tk + c * LANES)
                    acc = acc + jnp.where(hit, 1.0, 0.0)
                return acc

            cnt_ref[rs, :] = lax.fori_loop(0, n_tiles, body, jnp.zeros((rb, LANES), F32))
        c8 = lax.dot_general(ones8, cnt_ref[...].astype(BF16), _NT, preferred_element_type=F32)
        cnt = jnp.zeros((8, LANES), F32)
        for r in range(nb):
            cnt = jnp.where(sub8 == r, c8[:, r * rb:(r + 1) * rb], cnt)
        return cnt

    def set_rows(ref, x8):
        for r in range(nb):
            ref[r * rb:(r + 1) * rb, :] = rows_of(x8, r)

    def count_ge(t8):
        set_rows(thr_ref, t8)

        def per_block(rs):
            tb = thr_ref[rs, :]
            return lambda keys, kidx0: keys >= tb

        return count_pass(per_block)

    def flip(k):
        return k ^ ((k >> 31) & 0x7FFFFFFF)

    live = sub8 < nb
    lo0 = jnp.where(live, lanes_of(jnp.min(mn_ref[...], axis=1, keepdims=True)), 0)
    hi0 = jnp.where(live, lanes_of(jnp.max(mx_ref[...], axis=1, keepdims=True)), 0)
    c_hi0 = jnp.zeros((8, LANES), F32)
    value_steps = 26

    def bis_cond(st):
        it, lo, hi, c_lo, c_above = st
        open_rows = jnp.where((c_lo > k_eff) & (lo < hi), 1.0, 0.0)
        return jnp.max(open_rows) > 0.0

    def bis_body(st):
        it, lo, hi, c_lo, c_above = st
        mid_k = (lo >> 1) + (hi >> 1) + ((lo | hi) & 1)
        v_lo = lax.bitcast_convert_type(flip(lo), F32)
        v_hi = lax.bitcast_convert_type(flip(hi), F32)
        mid_v = flip(lax.bitcast_convert_type(0.5 * v_lo + 0.5 * v_hi, I32))
        mid_v = jnp.where(it == 0, 0, jnp.where(it == 1, 1, mid_v))
        mid_v = jnp.minimum(jnp.maximum(mid_v, lo + 1), hi)
        mid = jnp.where(it < value_steps, mid_v, mid_k)
        active = (c_lo > k_eff) & (lo < hi)
        c_mid = count_ge(mid)
        go_up = active & (c_mid >= k_eff)
        go_dn = active & (c_mid < k_eff)
        lo = jnp.where(go_up, mid, lo)
        c_lo = jnp.where(go_up, c_mid, c_lo)
        hi = jnp.where(go_dn, mid - 1, hi)
        c_above = jnp.where(go_dn, c_mid, c_above)
        return it + 1, lo, hi, c_lo, c_above

    _, thr, _, c_thr, c_above = lax.while_loop(
        bis_cond, bis_body, (jnp.int32(0), lo0, hi0, n_adm, c_hi0))
    set_rows(thr_ref, thr)
    need = k_eff - c_above
    has_tie = c_thr > k_eff
    jcut_ref[...] = jnp.full((tq, LANES), 2.0 ** 30, F32)

    @pl.when(jnp.max(jnp.where(has_tie, 1.0, 0.0)) > 0.0)
    def _():
        def jb_body(_, st):
            jl, jh = st
            jm = jnp.floor((jl + jh) * 0.5)
            set_rows(jcut_ref, jm)

            def per_block(rs):
                tb, jb = thr_ref[rs, :], jcut_ref[rs, :]
                return lambda keys, kidx0: (keys == tb) & ((kidx0 + lane_rb).astype(F32) < jb)

            ok = count_pass(per_block) >= need
            return jnp.where(ok, jl, jm), jnp.where(ok, jm, jh)

        n_keys = (n_tiles * tk).astype(F32)
        jl0 = jnp.zeros((8, LANES), F32)
        jh0 = jnp.zeros((8, LANES), F32) + n_keys
        steps = int(math.ceil(math.log2(key_ref.shape[0] * LANES))) + 1
        _, jh = lax.fori_loop(0, steps, jb_body, (jl0, jh0))
        set_rows(jcut_ref, jnp.where(has_tie, jh, 2.0 ** 30))

    def bias_tile(kt, carry):
        for c in range(nch):
            keys = key_ref[kt * nch + c]
            kidx = (kt * tk + c * LANES + lane).astype(F32)
            thr_b = thr_ref[...]
            sel = (keys > thr_b) | ((keys == thr_b) & (kidx < jcut_ref[...]))
            key_ref[kt * nch + c] = jnp.where(sel, 0, NEG_BITS)
        return carry

    lax.fori_loop(0, n_tiles, bias_tile, 0)

    for hp in range(A_HEADS // 2):
        ls = slice(hp * LANES, (hp + 1) * LANES)
        q_st = _split_lane_halves(q_ref[:, ls].astype(F32), lane)
        m_ref[...] = jnp.full(m_ref.shape, NEG, F32)
        l_ref[...] = jnp.zeros(l_ref.shape, F32)
        acc_ref[...] = jnp.zeros(acc_ref.shape, F32)

        def att_tile(kt, carry):
            r0 = pl.multiple_of(kt * tk, tk)
            s = lax.dot_general(q_st, k_ref[pl.ds(r0, tk), ls], _NT, preferred_element_type=F32)
            bias = jnp.concatenate(
                [lax.bitcast_convert_type(key_ref[kt * nch + c], F32) for c in range(nch)], axis=1)
            s = jnp.concatenate([s[:tq] + bias, s[tq:] + bias], axis=0)
            _online_softmax_step(s, v_ref[pl.ds(r0, tk), ls], m_ref, l_ref, acc_ref)
            return carry

        lax.fori_loop(0, n_tiles, att_tile, 0)
        o = _softmax_finish(l_ref, acc_ref)
        o_ref[:, ls] = jnp.where(lane < LANES // 2, o[:tq], o[tq:]).astype(o_ref.dtype)


def _dsa_call(qa, ka, va, iq, ik, iw, tq):
    b, s, _ = qa.shape
    topk = min(TOPK_MAX, s // 4)
    qrow = lambda width: pl.BlockSpec((None, tq, width), lambda bi, i: (bi, i, 0))
    seq = lambda width: pl.BlockSpec((None, s, width), lambda bi, i: (bi, 0, 0),
                                     pipeline_mode=pl.Buffered(1))
    return pl.pallas_call(
        functools.partial(_dsa_kernel, tq=tq, topk=topk),
        grid=(b, s // tq),
        in_specs=[qrow(D_A), seq(D_A), seq(D_A), qrow(IDX_HEADS * IDX_DIM), seq(LANES), qrow(LANES)],
        out_specs=qrow(D_A),
        out_shape=jax.ShapeDtypeStruct((b, s, D_A), BF16),
        scratch_shapes=[
            pltpu.VMEM((s // LANES, tq, LANES), I32),
            pltpu.VMEM((IDX_HEADS * tq, LANES), BF16),
            pltpu.VMEM((IDX_HEADS * tq, LANES), F32),
            pltpu.VMEM((tq, LANES), I32),
            pltpu.VMEM((tq, LANES), I32),
            pltpu.VMEM((tq, LANES), I32),
            pltpu.VMEM((tq, LANES), F32),
            pltpu.VMEM((tq, LANES), F32),
            pltpu.VMEM((2 * tq, LANES), F32),
            pltpu.VMEM((2 * tq, LANES), F32),
            pltpu.VMEM((2 * tq, LANES), F32),
        ],
        compiler_params=pltpu.CompilerParams(
            dimension_semantics=("arbitrary", "arbitrary"), vmem_limit_bytes=VMEM_LIMIT),
        name="dsa",
    )(qa, ka, va, iq, ik, iw)


def _diff_kernel(q_ref, k_ref, v_ref, lam_ref, g_ref, o_ref, m_ref, l_ref, acc_ref,
                 *, tq, lam_init):
    tk = tq
    j = pl.program_id(2)
    q0 = j * tq
    lane = lax.broadcasted_iota(I32, (tq, LANES), 1)
    col = lax.broadcasted_iota(I32, (tq, tk), 1)
    row = lax.broadcasted_iota(I32, (tq, 1), 0)
    q_end = (((q0 + row) >> 6) + 1) << 6

    q_st = _split_lane_halves(q_ref[...].astype(F32), lane)
    m_ref[...] = jnp.full(m_ref.shape, NEG, F32)
    l_ref[...] = jnp.zeros(l_ref.shape, F32)
    acc_ref[...] = jnp.zeros(acc_ref.shape, F32)

    def att_tile(kt, masked):
        r0 = pl.multiple_of(kt * tk, tk)
        s = lax.dot_general(q_st, k_ref[pl.ds(r0, tk), :], _NT, preferred_element_type=F32)
        if masked:
            adm = (kt * tk + col) < q_end
            s = jnp.concatenate([jnp.where(adm, s[:tq], NEG), jnp.where(adm, s[tq:], NEG)], axis=0)
        _online_softmax_step(s, v_ref[pl.ds(r0, tk), :], m_ref, l_ref, acc_ref)

    def full_tile(kt, carry):
        att_tile(kt, False)
        return carry

    lax.fori_loop(0, j, full_tile, 0)
    att_tile(j, True)

    lam_p = lam_ref[...]
    lam = (jnp.exp(jnp.sum(lam_p[0:1] * lam_p[1:2], axis=1, keepdims=True))
           - jnp.exp(jnp.sum(lam_p[2:3] * lam_p[3:4], axis=1, keepdims=True)) + lam_init)
    o = _softmax_finish(l_ref, acc_ref)
    o = o[:tq] - lam * o[tq:]
    o_ref[...] = (_rms(o, g_ref[...]) * (1.0 - lam_init)).astype(o_ref.dtype)


def _diff_call(qb, kb, vb, lam_p, g_subln, lam_init, tq):
    b, s, _ = qb.shape
    return pl.pallas_call(
        functools.partial(_diff_kernel, tq=tq, lam_init=lam_init),
        grid=(b, B_HEADS, s // tq),
        in_specs=[
            pl.BlockSpec((None, tq, LANES), lambda bi, h, i: (bi, i, h)),
            pl.BlockSpec((None, s, LANES), lambda bi, h, i: (bi, 0, h)),
            pl.BlockSpec((None, s, LANES), lambda bi, h, i: (bi, 0, h)),
            pl.BlockSpec((4, B_HEAD_DIM), lambda bi, h, i: (0, 0)),
            pl.BlockSpec((1, LANES), lambda bi, h, i: (0, 0)),
        ],
        out_specs=pl.BlockSpec((None, tq, LANES), lambda bi, h, i: (bi, i, h)),
        out_shape=jax.ShapeDtypeStruct((b, s, D_B), BF16),
        scratch_shapes=[
            pltpu.VMEM((2 * tq, LANES), F32),
            pltpu.VMEM((2 * tq, LANES), F32),
            pltpu.VMEM((2 * tq, LANES), F32),
        ],
        compiler_params=pltpu.CompilerParams(
            dimension_semantics=("arbitrary", "arbitrary", "arbitrary"),
            vmem_limit_bytes=VMEM_LIMIT),
        name="diff",
    )(qb, kb, vb, lam_p, g_subln)


def _ffn_kernel(x_ref, a_ref, b_ref, wo_ref, g_ref, wg_ref, wu_ref, wd_ref, gf_ref, o_ref,
                *, ff_chunk, final_norm):
    d_a = a_ref.shape[1]
    x1 = (x_ref[...]
          + jnp.dot(a_ref[...], wo_ref[:d_a, :], preferred_element_type=F32)
          + jnp.dot(b_ref[...], wo_ref[d_a:, :], preferred_element_type=F32))
    h2 = _rms(x1, g_ref[...]).astype(BF16)
    y = None
    for c in range(wg_ref.shape[1] // ff_chunk):
        cs = slice(c * ff_chunk, (c + 1) * ff_chunk)
        gate = jnp.dot(h2, wg_ref[:, cs], preferred_element_type=F32)
        up = jnp.dot(h2, wu_ref[:, cs], preferred_element_type=F32)
        act = (gate / (1.0 + jnp.exp(-gate)) * up).astype(BF16)
        down = jnp.dot(act, wd_ref[cs, :], preferred_element_type=F32)
        y = down if y is None else y + down
    x2 = x1 + y
    if final_norm:
        x2 = _rms(x2, gf_ref[...])
    o_ref[...] = x2


def _ffn_call(x, out_a, out_b, wo, g, wg, wu, wd, g_final, final_norm, tm, ff_chunk):
    t, d = x.shape
    d_ff = wg.shape[1]
    row = lambda width: pl.BlockSpec((tm, width), lambda i: (i, 0))
    const = lambda shape: pl.BlockSpec(shape, lambda i: (0, 0), pipeline_mode=pl.Buffered(1))
    return pl.pallas_call(
        functools.partial(_ffn_kernel, ff_chunk=ff_chunk, final_norm=final_norm),
        grid=(t // tm,),
        in_specs=[row(d), row(out_a.shape[1]), row(out_b.shape[1]), const(wo.shape), const((1, d)),
                  const((d, d_ff)), const((d, d_ff)), const((d_ff, d)), const((1, d))],
        out_specs=row(d),
        out_shape=jax.ShapeDtypeStruct((t, d), F32),
        compiler_params=pltpu.CompilerParams(
            dimension_semantics=("arbitrary",), vmem_limit_bytes=VMEM_LIMIT),
        name="ffn",
    )(x, out_a, out_b, wo, g, wg, wu, wd, g_final)


def _rope_tables(seq_len, dim):
    pos = jnp.arange(seq_len, dtype=F32)
    inv = ROPE_THETA ** (-jnp.arange(0, dim, 2, dtype=F32) / dim)
    ang = pos[:, None] * inv[None, :]
    cos, sin = jnp.cos(ang), jnp.sin(ang)
    reps = LANES // dim
    return (jnp.tile(jnp.concatenate([cos, cos], axis=1), (1, reps)),
            jnp.tile(jnp.concatenate([-sin, sin], axis=1), (1, reps)))


def _pack_w_in(w):
    splits = (D_A, D_A, D_A, IDX_HEADS * IDX_DIM, IDX_DIM, IDX_HEADS, D_B, D_B, D_B)
    offs = [0]
    for sz in splits:
        offs.append(offs[-1] + sz)
    qa, ka, va, iq, ik, iw, qb, kb, vb = (w[:, offs[i]:offs[i + 1]] for i in range(len(splits)))
    ik4 = jnp.tile(ik, (1, LANES // IDX_DIM))
    iw_p = jnp.pad(iw, ((0, 0), (0, LANES - IDX_HEADS)))
    return jnp.concatenate([qa, ka, va, iq, ik4, iw_p, qb, kb, vb], axis=1).astype(BF16)


def kernel(x, w_in, w_out, g_mix, lam_q1, lam_k1, lam_q2, lam_k2, g_subln, g_ffn, w_gate, w_up,
           w_down, g_final):
    b, s, d = x.shape
    depth = w_in.shape[0]
    tm_proj, tq, tm_ffn = 512, 512, 512
    d_ff = w_gate.shape[-1]
    ff_chunk = d_ff // 2
    assert s % tm_proj == 0 and s % tq == 0 and (b * s) % tm_ffn == 0 and ff_chunk % LANES == 0

    tabs = _rope_tables(s, A_HEAD_DIM) + _rope_tables(s, IDX_DIM)
    for layer in range(depth):
        qa, ka, va, iq, ik, iw, qb, kb, vb = _proj_call(
            x, g_mix[layer][None, :], _pack_w_in(w_in[layer]), tabs, tm_proj)
        out_a = _dsa_call(qa, ka, va, iq, ik, iw, tq)
        lam_init = 0.8 - 0.6 * math.exp(-0.3 * layer)
        lam_p = jnp.stack([lam_q1[layer], lam_k1[layer], lam_q2[layer], lam_k2[layer]]).astype(F32)
        out_b = _diff_call(qb, kb, vb, lam_p, g_subln[layer][None, :], lam_init, tq)
        x = _ffn_call(
            x.reshape(b * s, d), out_a.reshape(b * s, D_A), out_b.reshape(b * s, D_B),
            w_out[layer].astype(BF16), g_ffn[layer][None, :], w_gate[layer].astype(BF16),
            w_up[layer].astype(BF16), w_down[layer].astype(BF16), g_final[None, :],
            layer == depth - 1, tm_ffn, ff_chunk).reshape(b, s, d)
    return x
```

```python
import functools
import math

import jax
import jax.numpy as jnp
import numpy as np
from jax import lax
from jax.experimental import pallas as pl
from jax.experimental.pallas import tpu as pltpu

F32 = jnp.float32
BF16 = jnp.bfloat16
I32 = jnp.int32

CHUNK = 64
ROPE_THETA = 10000.0
RMS_EPS = 1e-6
A_HEADS = 8
A_HEAD_DIM = 64
D_A = A_HEADS * A_HEAD_DIM
IDX_HEADS = 8
IDX_DIM = 32
TOPK_MAX = 256
B_HEADS = 4
B_HEAD_DIM = 64
D_B = B_HEADS * 2 * B_HEAD_DIM

LANES = 128
VMEM_LIMIT = 56 * 1024 * 1024

LOG2E = 1.4426950408889634
NEG = -1e30
NEG_BITS = int(np.float32(NEG).view(np.int32))
KEY_MIN = -(2 ** 31)

_OFF_QA, _OFF_KA, _OFF_VA = 0, 512, 1024
_OFF_IQ, _OFF_IK, _OFF_IW = 1536, 1792, 1920
_OFF_QB, _OFF_KB, _OFF_VB = 2048, 2560, 3072
_W_CAT = 3584

_NT = (((1,), (1,)), ((), ()))


def _rms(x, g):
    r = lax.rsqrt(jnp.mean(x * x, axis=-1, keepdims=True) + RMS_EPS)
    return x * r * g


def _proj_kernel(x_ref, g_ref, w_ref, cos_a_ref, sin_a_ref, cos_i_ref, sin_i_ref,
                 qa_ref, ka_ref, va_ref, iq_ref, ik_ref, iw_ref, qb_ref, kb_ref, vb_ref):
    h = _rms(x_ref[...], g_ref[...]).astype(BF16)
    tm = h.shape[0]
    lane = lax.broadcasted_iota(I32, (tm, LANES), 1)

    def proj(off, width):
        return jnp.dot(h, w_ref[:, off:off + width], preferred_element_type=F32)

    def rope_tile(y, cos, sin_signed, half):
        first = (lane & (2 * half - 1)) < half
        rot = jnp.where(first, pltpu.roll(y, LANES - half, 1), pltpu.roll(y, half, 1))
        return y * cos + rot * sin_signed

    def rope_store(out_ref, off, width, cos_ref, sin_ref, half, scale):
        y = proj(off, width)
        cos, sin = cos_ref[...], sin_ref[...]
        for t in range(width // LANES):
            r = rope_tile(y[:, t * LANES:(t + 1) * LANES], cos, sin, half)
            if scale != 1.0:
                r = r * scale
            out_ref[:, t * LANES:(t + 1) * LANES] = r.astype(out_ref.dtype)

    half_a, half_i = A_HEAD_DIM // 2, IDX_DIM // 2
    rope_store(qa_ref, _OFF_QA, D_A, cos_a_ref, sin_a_ref, half_a, A_HEAD_DIM ** -0.5 * LOG2E)
    rope_store(ka_ref, _OFF_KA, D_A, cos_a_ref, sin_a_ref, half_a, 1.0)
    va_ref[...] = proj(_OFF_VA, D_A).astype(va_ref.dtype)
    rope_store(iq_ref, _OFF_IQ, IDX_HEADS * IDX_DIM, cos_i_ref, sin_i_ref, half_i, 1.0)
    rope_store(ik_ref, _OFF_IK, LANES, cos_i_ref, sin_i_ref, half_i, 1.0)
    iw_ref[...] = proj(_OFF_IW, LANES) * (IDX_HEADS ** -0.5 * IDX_DIM ** -0.5)
    rope_store(qb_ref, _OFF_QB, D_B, cos_a_ref, sin_a_ref, half_a, B_HEAD_DIM ** -0.5 * LOG2E)
    rope_store(kb_ref, _OFF_KB, D_B, cos_a_ref, sin_a_ref, half_a, 1.0)
    vb_ref[...] = proj(_OFF_VB, D_B).astype(vb_ref.dtype)


def _proj_call(x, g, w_cat, tabs, tm):
    b, s, d = x.shape
    cos_a, sin_a, cos_i, sin_i = tabs
    row = lambda width: pl.BlockSpec((None, tm, width), lambda bi, i: (bi, i, 0))
    tab = pl.BlockSpec((tm, LANES), lambda bi, i: (i, 0))
    const = lambda shape: pl.BlockSpec(shape, lambda bi, i: (0, 0), pipeline_mode=pl.Buffered(1))
    widths = (D_A, D_A, D_A, IDX_HEADS * IDX_DIM, LANES, LANES, D_B, D_B, D_B)
    dtypes = (BF16, BF16, BF16, BF16, BF16, F32, BF16, BF16, BF16)
    return pl.pallas_call(
        _proj_kernel,
        grid=(b, s // tm),
        in_specs=[row(d), const((1, d)), const((d, _W_CAT)), tab, tab, tab, tab],
        out_specs=[row(w) for w in widths],
        out_shape=[jax.ShapeDtypeStruct((b, s, w), dt) for w, dt in zip(widths, dtypes)],
        compiler_params=pltpu.CompilerParams(
            dimension_semantics=("arbitrary", "arbitrary"), vmem_limit_bytes=VMEM_LIMIT),
        name="proj",
    )(x, g, w_cat, cos_a, sin_a, cos_i, sin_i)


def _online_softmax_step(chunks, v_t, m_ref, l_ref, acc_ref):
    m_prev = m_ref[...]
    m_cur = functools.reduce(jnp.maximum, chunks)
    m_new = jnp.maximum(m_prev, jnp.max(m_cur, axis=1, keepdims=True))
    alpha = jnp.exp2(m_prev - m_new)
    p = [jnp.exp2(c - m_new) for c in chunks]
    l_ref[...] = alpha * l_ref[...] + functools.reduce(jnp.add, p)
    pv = jnp.dot(jnp.concatenate(p, axis=1).astype(BF16), v_t, preferred_element_type=F32)
    acc_ref[...] = alpha * acc_ref[...] + pv
    m_ref[...] = m_new


def _softmax_finish(l_ref, acc_ref):
    return acc_ref[...] / jnp.sum(l_ref[...], axis=1, keepdims=True)


def _split_lane_halves(tile_f32, lane):
    lo = jnp.where(lane < LANES // 2, tile_f32, 0.0)
    hi = jnp.where(lane >= LANES // 2, tile_f32, 0.0)
    return jnp.concatenate([lo, hi], axis=0).astype(BF16)


def _dsa_kernel(q_ref, k_ref, v_ref, iq_ref, ik_ref, iw_ref, o_ref,
                key_ref, iqm_ref, wb_ref, mn_ref, mx_ref, thr_ref, need_ref, cnt_ref,
                m_ref, l_ref, acc_ref,
                *, tq, topk):
    tk = tq
    nch = tk // LANES
    tki = 256
    rb = 128
    j = pl.program_id(1)
    q0 = j * tq
    n_tiles = j + 1
    lane = lax.broadcasted_iota(I32, (tq, LANES), 1)
    lane_rb = lax.broadcasted_iota(I32, (rb, LANES), 1)
    row = lax.broadcasted_iota(I32, (tq, 1), 0)
    q_end = (((q0 + row) >> 6) + 1) << 6

    nb = tq // rb
    sub8 = lax.broadcasted_iota(I32, (8, LANES), 0)
    lane8 = lax.broadcasted_iota(I32, (8, LANES), 1)
    n_adm = ((((q0 + sub8 * rb + lane8) >> 6) + 1) << 6).astype(F32)
    k_eff = jnp.minimum(n_adm, float(topk))
    ones8 = jnp.ones((8, LANES), BF16)
    tri_r = lax.broadcasted_iota(I32, (LANES, 2 * LANES), 0)
    tri_c = lax.broadcasted_iota(I32, (LANES, 2 * LANES), 1)
    tri_ones = jnp.where((tri_c >= LANES) | (tri_r <= tri_c), 1.0, 0.0).astype(BF16)

    def rows_of(x8, r):
        x = jnp.broadcast_to(x8[r:r + 1, :], (rb, LANES))
        if x8.dtype == I32:
            return lax.bitcast_convert_type(jnp.transpose(lax.bitcast_convert_type(x, F32)), I32)
        return jnp.transpose(x)

    def lanes_of(col):
        out = jnp.zeros((8, LANES), col.dtype)
        for r in range(nb):
            x = jnp.broadcast_to(col[r * rb:(r + 1) * rb], (rb, LANES))
            if col.dtype == I32:
                x = lax.bitcast_convert_type(jnp.transpose(lax.bitcast_convert_type(x, F32)), I32)
            else:
                x = jnp.transpose(x)
            out = jnp.where(sub8 == r, x[0:8, :], out)
        return out

    iw = iw_ref[...]
    for h in range(IDX_HEADS):
        g = h % 4
        tile = iq_ref[:, (h // 4) * LANES:(h // 4 + 1) * LANES].astype(F32)
        keep = (lane >= g * IDX_DIM) & (lane < (g + 1) * IDX_DIM)
        iqm_ref[h * tq:(h + 1) * tq, :] = jnp.where(keep, tile, 0.0).astype(BF16)
        wb_ref[h * tq:(h + 1) * tq, :] = jnp.broadcast_to(iw[:, h:h + 1], (tq, LANES))

    mn_ref[...] = jnp.full((tq, LANES), 2 ** 31 - 1, I32)
    mx_ref[...] = jnp.full((tq, LANES), KEY_MIN, I32)

    def index_tile(it, masked):
        r0 = pl.multiple_of(it * tki, tki)
        d = lax.dot_general(iqm_ref[...], ik_ref[pl.ds(r0, tki), :], _NT,
                            preferred_element_type=F32)
        for c in range(tki // LANES):
            for r in range(tq // rb):
                rs = slice(r * rb, (r + 1) * rb)
                a = jnp.zeros((rb, LANES), F32)
                for h in range(IDX_HEADS):
                    hs = slice(h * tq + r * rb, h * tq + (r + 1) * rb)
                    a = a + wb_ref[hs, :] * jnp.maximum(d[hs, c * LANES:(c + 1) * LANES], 0.0)
                bits = lax.bitcast_convert_type(a, I32)
                key = bits ^ ((bits >> 31) & 0x7FFFFFFF)
                key_lo = key
                if masked:
                    adm = (it * tki + c * LANES + lane_rb) < q_end[rs]
                    key_lo = jnp.where(adm, key, 2 ** 31 - 1)
                    key = jnp.where(adm, key, KEY_MIN)
                mn_ref[rs, :] = jnp.minimum(mn_ref[rs, :], key_lo)
                mx_ref[rs, :] = jnp.maximum(mx_ref[rs, :], key)
                key_ref[it * (tki // LANES) + c, rs, :] = key

    def full_tile(it, carry):
        index_tile(it, False)
        return carry

    n_full = q0 // tki
    lax.fori_loop(0, n_full, full_tile, 0)
    for t in range(tq // tki):
        index_tile(n_full + t, True)

    def count_pass(per_block):
        for r in range(nb):
            rs = slice(r * rb, (r + 1) * rb)
            pred = per_block(rs)

            def body(kt, acc):
                for c in range(nch):
                    hit = pred(key_ref[kt * nch + c, rs, :], kt * tk + c * LANES)
                    acc = acc + jnp.where(hit, 1.0, 0.0)
                return acc

            cnt_ref[rs, :] = lax.fori_loop(0, n_tiles, body, jnp.zeros((rb, LANES), F32))
        c8 = lax.dot_general(ones8, cnt_ref[...].astype(BF16), _NT, preferred_element_type=F32)
        cnt = jnp.zeros((8, LANES), F32)
        for r in range(nb):
            cnt = jnp.where(sub8 == r, c8[:, r * rb:(r + 1) * rb], cnt)
        return cnt

    def set_rows(ref, x8):
        for r in range(nb):
            ref[r * rb:(r + 1) * rb, :] = rows_of(x8, r)

    def count_ge(t8):
        set_rows(thr_ref, t8)

        def per_block(rs):
            tb = thr_ref[rs, :]
            return lambda keys, kidx0: keys >= tb

        return count_pass(per_block)

    def flip(k):
        return k ^ ((k >> 31) & 0x7FFFFFFF)

    live = sub8 < nb
    lo0 = jnp.where(live, lanes_of(jnp.min(mn_ref[...], axis=1, keepdims=True)), 0)
    hi0 = jnp.where(live, lanes_of(jnp.max(mx_ref[...], axis=1, keepdims=True)), 0)
    c_hi0 = jnp.zeros((8, LANES), F32)
    value_steps = 26

    def bis_cond(st):
        it, lo, hi, c_lo, c_above = st
        open_rows = jnp.where((c_lo > k_eff) & (lo < hi), 1.0, 0.0)
        return jnp.max(open_rows) > 0.0

    def bis_body(st):
        it, lo, hi, c_lo, c_above = st
        mid_k = (lo >> 1) + (hi >> 1) + ((lo | hi) & 1)
        v_lo = lax.bitcast_convert_type(flip(lo), F32)
        v_hi = lax.bitcast_convert_type(flip(hi), F32)
        mid_v = flip(lax.bitcast_convert_type(0.5 * v_lo + 0.5 * v_hi, I32))
        mid_v = jnp.where(it == 0, 0, jnp.where(it == 1, 1, mid_v))
        mid_v = jnp.minimum(jnp.maximum(mid_v, lo + 1), hi)
        mid = jnp.where(it < value_steps, mid_v, mid_k)
        active = (c_lo > k_eff) & (lo < hi)
        c_mid = count_ge(mid)
        go_up = active & (c_mid >= k_eff)
        go_dn = active & (c_mid < k_eff)
        lo = jnp.where(go_up, mid, lo)
        c_lo = jnp.where(go_up, c_mid, c_lo)
        hi = jnp.where(go_dn, mid - 1, hi)
        c_above = jnp.where(go_dn, c_mid, c_above)
        return it + 1, lo, hi, c_lo, c_above

    _, thr, _, c_thr, c_above = lax.while_loop(
        bis_cond, bis_body, (jnp.int32(0), lo0, hi0, n_adm, c_hi0))
    set_rows(thr_ref, thr)
    need = k_eff - c_above
    has_tie = c_thr > k_eff
    any_tie = jnp.max(jnp.where(has_tie, 1.0, 0.0)) > 0.0

    def bias_pass(with_ties):
        def bias_tile(kt, carry):
            for c in range(nch):
                keys = key_ref[kt * nch + c]
                thr_b = thr_ref[...]
                tie = keys == thr_b
                if with_ties:
                    hits = jnp.dot(jnp.where(tie, 1.0, 0.0).astype(BF16), tri_ones,
                                   preferred_element_type=F32)
                    seen = cnt_ref[...]
                    tie = tie & (hits[:, :LANES] + seen <= need_ref[...])
                    cnt_ref[...] = seen + hits[:, LANES:]
                key_ref[kt * nch + c] = jnp.where((keys > thr_b) | tie, 0, NEG_BITS)
            return carry

        lax.fori_loop(0, n_tiles, bias_tile, 0)

    @pl.when(any_tie)
    def _():
        set_rows(need_ref, jnp.where(has_tie, need, 2.0 ** 30))
        cnt_ref[...] = jnp.zeros((tq, LANES), F32)
        bias_pass(True)

    @pl.when(jnp.logical_not(any_tie))
    def _():
        bias_pass(False)

    for hp in range(A_HEADS // 2):
        ls = slice(hp * LANES, (hp + 1) * LANES)
        q_st = _split_lane_halves(q_ref[:, ls].astype(F32), lane)
        m_ref[...] = jnp.full(m_ref.shape, NEG, F32)
        l_ref[...] = jnp.zeros(l_ref.shape, F32)
        acc_ref[...] = jnp.zeros(acc_ref.shape, F32)

        def logits(kt):
            r0 = pl.multiple_of(kt * tk, tk)
            return lax.dot_general(q_st, k_ref[pl.ds(r0, tk), ls], _NT, preferred_element_type=F32)

        def att_tile(kt, carry):
            s = logits(kt)
            chunks = []
            for c in range(nch):
                bias = lax.bitcast_convert_type(key_ref[kt * nch + c], F32)
                sc = s[:, c * LANES:(c + 1) * LANES]
                chunks.append(jnp.concatenate([sc[:tq] + bias, sc[tq:] + bias], axis=0))
            r0 = pl.multiple_of(kt * tk, tk)
            _online_softmax_step(chunks, v_ref[pl.ds(r0, tk), ls], m_ref, l_ref, acc_ref)
            return carry

        lax.fori_loop(0, n_tiles, att_tile, 0)
        o = _softmax_finish(l_ref, acc_ref)
        o_ref[:, ls] = jnp.where(lane < LANES // 2, o[:tq], o[tq:]).astype(o_ref.dtype)


def _dsa_call(qa, ka, va, iq, ik, iw, tq):
    b, s, _ = qa.shape
    topk = min(TOPK_MAX, s // 4)
    qrow = lambda width: pl.BlockSpec((None, tq, width), lambda bi, i: (bi, i, 0))
    seq = lambda width: pl.BlockSpec((None, s, width), lambda bi, i: (bi, 0, 0),
                                     pipeline_mode=pl.Buffered(1))
    return pl.pallas_call(
        functools.partial(_dsa_kernel, tq=tq, topk=topk),
        grid=(b, s // tq),
        in_specs=[qrow(D_A), seq(D_A), seq(D_A), qrow(IDX_HEADS * IDX_DIM), seq(LANES), qrow(LANES)],
        out_specs=qrow(D_A),
        out_shape=jax.ShapeDtypeStruct((b, s, D_A), BF16),
        scratch_shapes=[
            pltpu.VMEM((s // LANES, tq, LANES), I32),
            pltpu.VMEM((IDX_HEADS * tq, LANES), BF16),
            pltpu.VMEM((IDX_HEADS * tq, LANES), F32),
            pltpu.VMEM((tq, LANES), I32),
            pltpu.VMEM((tq, LANES), I32),
            pltpu.VMEM((tq, LANES), I32),
            pltpu.VMEM((tq, LANES), F32),
            pltpu.VMEM((tq, LANES), F32),
            pltpu.VMEM((2 * tq, LANES), F32),
            pltpu.VMEM((2 * tq, LANES), F32),
            pltpu.VMEM((2 * tq, LANES), F32),
        ],
        compiler_params=pltpu.CompilerParams(
            dimension_semantics=("arbitrary", "arbitrary"), vmem_limit_bytes=VMEM_LIMIT),
        name="dsa",
    )(qa, ka, va, iq, ik, iw)


def _diff_kernel(q_ref, k_ref, v_ref, lam_ref, g_ref, o_ref, m_ref, l_ref, acc_ref,
                 *, tq, lam_init):
    tk = tq
    j = pl.program_id(2)
    q0 = j * tq
    lane = lax.broadcasted_iota(I32, (tq, LANES), 1)
    row = lax.broadcasted_iota(I32, (tq, 1), 0)
    q_end = (((q0 + row) >> 6) + 1) << 6

    q_st = _split_lane_halves(q_ref[...].astype(F32), lane)
    m_ref[...] = jnp.full(m_ref.shape, NEG, F32)
    l_ref[...] = jnp.zeros(l_ref.shape, F32)
    acc_ref[...] = jnp.zeros(acc_ref.shape, F32)

    def logits(kt):
        r0 = pl.multiple_of(kt * tk, tk)
        return lax.dot_general(q_st, k_ref[pl.ds(r0, tk), :], _NT, preferred_element_type=F32)

    def att_tile(kt, masked):
        s = logits(kt)
        chunks = []
        for c in range(tk // LANES):
            sc = s[:, c * LANES:(c + 1) * LANES]
            if masked:
                adm = (kt * tk + c * LANES + lane) < q_end
                sc = jnp.concatenate(
                    [jnp.where(adm, sc[:tq], NEG), jnp.where(adm, sc[tq:], NEG)], axis=0)
            chunks.append(sc)
        r0 = pl.multiple_of(kt * tk, tk)
        _online_softmax_step(chunks, v_ref[pl.ds(r0, tk), :], m_ref, l_ref, acc_ref)

    def full_tile(kt, carry):
        att_tile(kt, False)
        return carry

    lax.fori_loop(0, j, full_tile, 0)
    att_tile(j, True)

    lam_p = lam_ref[...]
    lam = (jnp.exp(jnp.sum(lam_p[0:1] * lam_p[1:2], axis=1, keepdims=True))
           - jnp.exp(jnp.sum(lam_p[2:3] * lam_p[3:4], axis=1, keepdims=True)) + lam_init)
    o = _softmax_finish(l_ref, acc_ref)
    o = o[:tq] - lam * o[tq:]
    o_ref[...] = (_rms(o, g_ref[...]) * (1.0 - lam_init)).astype(o_ref.dtype)


def _diff_call(qb, kb, vb, lam_p, g_subln, lam_init, tq):
    b, s, _ = qb.shape
    return pl.pallas_call(
        functools.partial(_diff_kernel, tq=tq, lam_init=lam_init),
        grid=(b, B_HEADS, s // tq),
        in_specs=[
            pl.BlockSpec((None, tq, LANES), lambda bi, h, i: (bi, i, h)),
            pl.BlockSpec((None, s, LANES), lambda bi, h, i: (bi, 0, h)),
            pl.BlockSpec((None, s, LANES), lambda bi, h, i: (bi, 0, h)),
            pl.BlockSpec((4, B_HEAD_DIM), lambda bi, h, i: (0, 0)),
            pl.BlockSpec((1, LANES), lambda bi, h, i: (0, 0)),
        ],
        out_specs=pl.BlockSpec((None, tq, LANES), lambda bi, h, i: (bi, i, h)),
        out_shape=jax.ShapeDtypeStruct((b, s, D_B), BF16),
        scratch_shapes=[
            pltpu.VMEM((2 * tq, LANES), F32),
            pltpu.VMEM((2 * tq, LANES), F32),
            pltpu.VMEM((2 * tq, LANES), F32),
        ],
        compiler_params=pltpu.CompilerParams(
            dimension_semantics=("arbitrary", "arbitrary", "arbitrary"),
            vmem_limit_bytes=VMEM_LIMIT),
        name="diff",
    )(qb, kb, vb, lam_p, g_subln)


def _ffn_kernel(x_ref, a_ref, b_ref, wo_ref, g_ref, wg_ref, wu_ref, wd_ref, gf_ref, o_ref,
                *, ff_chunk, final_norm):
    d_a = a_ref.shape[1]
    x1 = (x_ref[...]
          + jnp.dot(a_ref[...], wo_ref[:d_a, :], preferred_element_type=F32)
          + jnp.dot(b_ref[...], wo_ref[d_a:, :], preferred_element_type=F32))
    h2 = _rms(x1, g_ref[...]).astype(BF16)
    y = None
    for c in range(wg_ref.shape[1] // ff_chunk):
        cs = slice(c * ff_chunk, (c + 1) * ff_chunk)
        gate = jnp.dot(h2, wg_ref[:, cs], preferred_element_type=F32)
        up = jnp.dot(h2, wu_ref[:, cs], preferred_element_type=F32)
        act = (gate / (1.0 + jnp.exp(-gate)) * up).astype(BF16)
        down = jnp.dot(act, wd_ref[cs, :], preferred_element_type=F32)
        y = down if y is None else y + down
    x2 = x1 + y
    if final_norm:
        x2 = _rms(x2, gf_ref[...])
    o_ref[...] = x2


def _ffn_call(x, out_a, out_b, wo, g, wg, wu, wd, g_final, final_norm, tm, ff_chunk):
    t, d = x.shape
    d_ff = wg.shape[1]
    row = lambda width: pl.BlockSpec((tm, width), lambda i: (i, 0))
    const = lambda shape: pl.BlockSpec(shape, lambda i: (0, 0), pipeline_mode=pl.Buffered(1))
    return pl.pallas_call(
        functools.partial(_ffn_kernel, ff_chunk=ff_chunk, final_norm=final_norm),
        grid=(t // tm,),
        in_specs=[row(d), row(out_a.shape[1]), row(out_b.shape[1]), const(wo.shape), const((1, d)),
                  const((d, d_ff)), const((d, d_ff)), const((d_ff, d)), const((1, d))],
        out_specs=row(d),
        out_shape=jax.ShapeDtypeStruct((t, d), F32),
        compiler_params=pltpu.CompilerParams(
            dimension_semantics=("arbitrary",), vmem_limit_bytes=VMEM_LIMIT),
        name="ffn",
    )(x, out_a, out_b, wo, g, wg, wu, wd, g_final)


def _rope_tables(seq_len, dim):
    pos = jnp.arange(seq_len, dtype=F32)
    inv = ROPE_THETA ** (-jnp.arange(0, dim, 2, dtype=F32) / dim)
    ang = pos[:, None] * inv[None, :]
    cos, sin = jnp.cos(ang), jnp.sin(ang)
    reps = LANES // dim
    return (jnp.tile(jnp.concatenate([cos, cos], axis=1), (1, reps)),
            jnp.tile(jnp.concatenate([-sin, sin], axis=1), (1, reps)))


def _pack_w_in(w):
    splits = (D_A, D_A, D_A, IDX_HEADS * IDX_DIM, IDX_DIM, IDX_HEADS, D_B, D_B, D_B)
    offs = [0]
    for sz in splits:
        offs.append(offs[-1] + sz)
    qa, ka, va, iq, ik, iw, qb, kb, vb = (w[:, offs[i]:offs[i + 1]] for i in range(len(splits)))
    ik4 = jnp.tile(ik, (1, LANES // IDX_DIM))
    iw_p = jnp.pad(iw, ((0, 0), (0, LANES - IDX_HEADS)))
    return jnp.concatenate([qa, ka, va, iq, ik4, iw_p, qb, kb, vb], axis=1).astype(BF16)


def kernel(x, w_in, w_out, g_mix, lam_q1, lam_k1, lam_q2, lam_k2, g_subln, g_ffn, w_gate, w_up,
           w_down, g_final):
    b, s, d = x.shape
    depth = w_in.shape[0]
    tm_proj, tq, tm_ffn = 512, 512, 512
    d_ff = w_gate.shape[-1]
    ff_chunk = d_ff // 2
    assert s % tm_proj == 0 and s % tq == 0 and (b * s) % tm_ffn == 0 and ff_chunk % LANES == 0

    tabs = _rope_tables(s, A_HEAD_DIM) + _rope_tables(s, IDX_DIM)
    for layer in range(depth):
        qa, ka, va, iq, ik, iw, qb, kb, vb = _proj_call(
            x, g_mix[layer][None, :], _pack_w_in(w_in[layer]), tabs, tm_proj)
        out_a = _dsa_call(qa, ka, va, iq, ik, iw, tq)
        lam_init = 0.8 - 0.6 * math.exp(-0.3 * layer)
        lam_p = jnp.stack([lam_q1[layer], lam_k1[layer], lam_q2[layer], lam_k2[layer]]).astype(F32)
        out_b = _diff_call(qb, kb, vb, lam_p, g_subln[layer][None, :], lam_init, tq)
        x = _ffn_call(
            x.reshape(b * s, d), out_a.reshape(b * s, D_A), out_b.reshape(b * s, D_B),
            w_out[layer].astype(BF16), g_ffn[layer][None, :], w_gate[layer].astype(BF16),
            w_up[layer].astype(BF16), w_down[layer].astype(BF16), g_final[None, :],
            layer == depth - 1, tm_ffn, ff_chunk).reshape(b, s, d)
    return x
```

```python
import functools
import math

import jax
import jax.numpy as jnp
import numpy as np
from jax import lax
from jax.experimental import pallas as pl
from jax.experimental.pallas import tpu as pltpu

F32 = jnp.float32
BF16 = jnp.bfloat16
I32 = jnp.int32

CHUNK = 64
ROPE_THETA = 10000.0
RMS_EPS = 1e-6
A_HEADS = 8
A_HEAD_DIM = 64
D_A = A_HEADS * A_HEAD_DIM
IDX_HEADS = 8
IDX_DIM = 32
TOPK_MAX = 256
B_HEADS = 4
B_HEAD_DIM = 64
D_B = B_HEADS * 2 * B_HEAD_DIM

LANES = 128
VMEM_LIMIT = 56 * 1024 * 1024

LOG2E = 1.4426950408889634
NEG = -1e30
NEG_BITS = int(np.float32(NEG).view(np.int32))
KEY_MIN = -(2 ** 31)

_OFF_QA, _OFF_KA, _OFF_VA = 0, 512, 1024
_OFF_IQ, _OFF_IK, _OFF_IW = 1536, 1792, 1920
_OFF_QB, _OFF_KB, _OFF_VB = 2048, 2560, 3072
_W_CAT = 3584

_NT = (((1,), (1,)), ((), ()))


def _rms(x, g):
    r = lax.rsqrt(jnp.mean(x * x, axis=-1, keepdims=True) + RMS_EPS)
    return x * r * g


def _proj_kernel(x_ref, g_ref, w_ref, cos_a_ref, sin_a_ref, cos_i_ref, sin_i_ref,
                 qa_ref, ka_ref, va_ref, iq_ref, ik_ref, iw_ref, qb_ref, kb_ref, vb_ref):
    h = _rms(x_ref[...], g_ref[...]).astype(BF16)
    tm = h.shape[0]
    lane = lax.broadcasted_iota(I32, (tm, LANES), 1)

    def proj(off, width):
        return jnp.dot(h, w_ref[:, off:off + width], preferred_element_type=F32)

    def rope_tile(y, cos, sin_signed, half):
        first = (lane & (2 * half - 1)) < half
        rot = jnp.where(first, pltpu.roll(y, LANES - half, 1), pltpu.roll(y, half, 1))
        return y * cos + rot * sin_signed

    def rope_store(out_ref, off, width, cos_ref, sin_ref, half, scale):
        y = proj(off, width)
        cos, sin = cos_ref[...], sin_ref[...]
        for t in range(width // LANES):
            r = rope_tile(y[:, t * LANES:(t + 1) * LANES], cos, sin, half)
            if scale != 1.0:
                r = r * scale
            out_ref[:, t * LANES:(t + 1) * LANES] = r.astype(out_ref.dtype)

    half_a, half_i = A_HEAD_DIM // 2, IDX_DIM // 2
    rope_store(qa_ref, _OFF_QA, D_A, cos_a_ref, sin_a_ref, half_a, A_HEAD_DIM ** -0.5 * LOG2E)
    rope_store(ka_ref, _OFF_KA, D_A, cos_a_ref, sin_a_ref, half_a, 1.0)
    va_ref[...] = proj(_OFF_VA, D_A).astype(va_ref.dtype)
    rope_store(iq_ref, _OFF_IQ, IDX_HEADS * IDX_DIM, cos_i_ref, sin_i_ref, half_i, 1.0)
    rope_store(ik_ref, _OFF_IK, LANES, cos_i_ref, sin_i_ref, half_i, 1.0)
    iw_ref[...] = proj(_OFF_IW, LANES) * (IDX_HEADS ** -0.5 * IDX_DIM ** -0.5)
    rope_store(qb_ref, _OFF_QB, D_B, cos_a_ref, sin_a_ref, half_a, B_HEAD_DIM ** -0.5 * LOG2E)
    rope_store(kb_ref, _OFF_KB, D_B, cos_a_ref, sin_a_ref, half_a, 1.0)
    vb_ref[...] = proj(_OFF_VB, D_B).astype(vb_ref.dtype)


def _proj_call(x, g, w_cat, tabs, tm):
    b, s, d = x.shape
    cos_a, sin_a, cos_i, sin_i = tabs
    row = lambda width: pl.BlockSpec((None, tm, width), lambda bi, i: (bi, i, 0))
    tab = pl.BlockSpec((tm, LANES), lambda bi, i: (i, 0))
    const = lambda shape: pl.BlockSpec(shape, lambda bi, i: (0, 0), pipeline_mode=pl.Buffered(1))
    widths = (D_A, D_A, D_A, IDX_HEADS * IDX_DIM, LANES, LANES, D_B, D_B, D_B)
    dtypes = (BF16, BF16, BF16, BF16, BF16, F32, BF16, BF16, BF16)
    return pl.pallas_call(
        _proj_kernel,
        grid=(b, s // tm),
        in_specs=[row(d), const((1, d)), const((d, _W_CAT)), tab, tab, tab, tab],
        out_specs=[row(w) for w in widths],
        out_shape=[jax.ShapeDtypeStruct((b, s, w), dt) for w, dt in zip(widths, dtypes)],
        compiler_params=pltpu.CompilerParams(
            dimension_semantics=("arbitrary", "arbitrary"), vmem_limit_bytes=VMEM_LIMIT),
        name="proj",
    )(x, g, w_cat, cos_a, sin_a, cos_i, sin_i)


def _online_softmax_step(chunks, v_t, m_ref, l_ref, acc_ref):
    m_prev = m_ref[...]
    m_cur = functools.reduce(jnp.maximum, chunks)
    m_new = jnp.maximum(m_prev, jnp.max(m_cur, axis=1, keepdims=True))
    alpha = jnp.exp2(m_prev - m_new)
    p = [jnp.exp2(c - m_new) for c in chunks]
    l_ref[...] = alpha * l_ref[...] + functools.reduce(jnp.add, p)
    pv = jnp.dot(jnp.concatenate(p, axis=1).astype(BF16), v_t, preferred_element_type=F32)
    acc_ref[...] = alpha * acc_ref[...] + pv
    m_ref[...] = m_new


def _softmax_finish(l_ref, acc_ref):
    return acc_ref[...] / jnp.sum(l_ref[...], axis=1, keepdims=True)


def _split_lane_halves(tile_f32, lane):
    lo = jnp.where(lane < LANES // 2, tile_f32, 0.0)
    hi = jnp.where(lane >= LANES // 2, tile_f32, 0.0)
    return jnp.concatenate([lo, hi], axis=0).astype(BF16)


def _dsa_kernel(q_ref, k_ref, v_ref, iq_ref, ik_ref, iw_ref, o_ref,
                key_ref, iqm_ref, wb_ref, mn_ref, mx_ref, thr_ref, need_ref, cnt_ref,
                m_ref, l_ref, acc_ref,
                *, tq, topk):
    tk = tq
    nch = tk // LANES
    tki = 256
    rb = 128
    j = pl.program_id(1)
    q0 = j * tq
    n_tiles = j + 1
    lane = lax.broadcasted_iota(I32, (tq, LANES), 1)
    lane_rb = lax.broadcasted_iota(I32, (rb, LANES), 1)
    row = lax.broadcasted_iota(I32, (tq, 1), 0)
    q_end = (((q0 + row) >> 6) + 1) << 6

    nb = tq // rb
    sub8 = lax.broadcasted_iota(I32, (8, LANES), 0)
    lane8 = lax.broadcasted_iota(I32, (8, LANES), 1)
    n_adm = ((((q0 + sub8 * rb + lane8) >> 6) + 1) << 6).astype(F32)
    k_eff = jnp.minimum(n_adm, float(topk))
    ones8 = jnp.ones((8, LANES), BF16)
    tri_r = lax.broadcasted_iota(I32, (LANES, 2 * LANES), 0)
    tri_c = lax.broadcasted_iota(I32, (LANES, 2 * LANES), 1)
    tri_ones = jnp.where((tri_c >= LANES) | (tri_r <= tri_c), 1.0, 0.0).astype(BF16)

    def rows_of(x8, r):
        x = jnp.broadcast_to(x8[r:r + 1, :], (rb, LANES))
        if x8.dtype == I32:
            return lax.bitcast_convert_type(jnp.transpose(lax.bitcast_convert_type(x, F32)), I32)
        return jnp.transpose(x)

    def lanes_of(col):
        out = jnp.zeros((8, LANES), col.dtype)
        for r in range(nb):
            x = jnp.broadcast_to(col[r * rb:(r + 1) * rb], (rb, LANES))
            if col.dtype == I32:
                x = lax.bitcast_convert_type(jnp.transpose(lax.bitcast_convert_type(x, F32)), I32)
            else:
                x = jnp.transpose(x)
            out = jnp.where(sub8 == r, x[0:8, :], out)
        return out

    iw = iw_ref[...]
    for h in range(IDX_HEADS):
        g = h % 4
        tile = iq_ref[:, (h // 4) * LANES:(h // 4 + 1) * LANES].astype(F32)
        keep = (lane >= g * IDX_DIM) & (lane < (g + 1) * IDX_DIM)
        iq_h = jnp.where(keep, tile, 0.0).astype(BF16)
        w_h = jnp.broadcast_to(iw[:, h:h + 1], (tq, LANES))
        for r in range(nb):
            dst = slice((r * IDX_HEADS + h) * rb, (r * IDX_HEADS + h + 1) * rb)
            iqm_ref[dst, :] = iq_h[r * rb:(r + 1) * rb]
            wb_ref[dst, :] = w_h[r * rb:(r + 1) * rb]

    mn_ref[...] = jnp.full((tq, LANES), 2 ** 31 - 1, I32)
    mx_ref[...] = jnp.full((tq, LANES), KEY_MIN, I32)

    def index_tile(it, masked):
        r0 = pl.multiple_of(it * tki, tki)
        ik_t = ik_ref[pl.ds(r0, tki), :]
        for r in range(nb):
            rs = slice(r * rb, (r + 1) * rb)
            base = r * IDX_HEADS * rb
            d = lax.dot_general(iqm_ref[base:base + IDX_HEADS * rb, :], ik_t, _NT,
                                preferred_element_type=F32)
            for c in range(tki // LANES):
                a = jnp.zeros((rb, LANES), F32)
                for h in range(IDX_HEADS):
                    hs = slice(h * rb, (h + 1) * rb)
                    w_h = wb_ref[base + h * rb:base + (h + 1) * rb, :]
                    a = a + w_h * jnp.maximum(d[hs, c * LANES:(c + 1) * LANES], 0.0)
                bits = lax.bitcast_convert_type(a, I32)
                key = bits ^ ((bits >> 31) & 0x7FFFFFFF)
                key_lo = key
                if masked:
                    adm = (it * tki + c * LANES + lane_rb) < q_end[rs]
                    key_lo = jnp.where(adm, key, 2 ** 31 - 1)
                    key = jnp.where(adm, key, KEY_MIN)
                mn_ref[rs, :] = jnp.minimum(mn_ref[rs, :], key_lo)
                mx_ref[rs, :] = jnp.maximum(mx_ref[rs, :], key)
                key_ref[it * (tki // LANES) + c, rs, :] = key

    def full_tile(it, carry):
        index_tile(it, False)
        return carry

    n_full = q0 // tki
    lax.fori_loop(0, n_full, full_tile, 0)
    for t in range(tq // tki):
        index_tile(n_full + t, True)

    def count_pass(per_block):
        cnt_ref[...] = jnp.zeros((tq, LANES), F32)

        def body(kt, carry):
            for r in range(nb):
                rs = slice(r * rb, (r + 1) * rb)
                pred = per_block(rs)
                acc = cnt_ref[rs, :]
                for c in range(nch):
                    hit = pred(key_ref[kt * nch + c, rs, :], kt * tk + c * LANES)
                    acc = acc + jnp.where(hit, 1.0, 0.0)
                cnt_ref[rs, :] = acc
            return carry

        lax.fori_loop(0, n_tiles, body, 0)
        c8 = lax.dot_general(ones8, cnt_ref[...].astype(BF16), _NT, preferred_element_type=F32)
        cnt = jnp.zeros((8, LANES), F32)
        for r in range(nb):
            cnt = jnp.where(sub8 == r, c8[:, r * rb:(r + 1) * rb], cnt)
        return cnt

    def set_rows(ref, x8):
        for r in range(nb):
            ref[r * rb:(r + 1) * rb, :] = rows_of(x8, r)

    def count_ge(t8):
        set_rows(thr_ref, t8)

        def per_block(rs):
            tb = thr_ref[rs, :]
            return lambda keys, kidx0: keys >= tb

        return count_pass(per_block)

    def flip(k):
        return k ^ ((k >> 31) & 0x7FFFFFFF)

    live = sub8 < nb
    lo0 = jnp.where(live, lanes_of(jnp.min(mn_ref[...], axis=1, keepdims=True)), 0)
    hi0 = jnp.where(live, lanes_of(jnp.max(mx_ref[...], axis=1, keepdims=True)), 0)
    c_hi0 = jnp.zeros((8, LANES), F32)
    value_steps = 26

    def bis_cond(st):
        it, lo, hi, c_lo, c_above = st
        open_rows = jnp.where((c_lo > k_eff) & (lo < hi), 1.0, 0.0)
        return jnp.max(open_rows) > 0.0

    def bis_body(st):
        it, lo, hi, c_lo, c_above = st
        mid_k = (lo >> 1) + (hi >> 1) + ((lo | hi) & 1)
        v_lo = lax.bitcast_convert_type(flip(lo), F32)
        v_hi = lax.bitcast_convert_type(flip(hi), F32)
        mid_v = flip(lax.bitcast_convert_type(0.5 * v_lo + 0.5 * v_hi, I32))
        mid_v = jnp.where(it == 0, 0, jnp.where(it == 1, 1, mid_v))
        mid_v = jnp.minimum(jnp.maximum(mid_v, lo + 1), hi)
        mid = jnp.where(it < value_steps, mid_v, mid_k)
        active = (c_lo > k_eff) & (lo < hi)
        c_mid = count_ge(mid)
        go_up = active & (c_mid >= k_eff)
        go_dn = active & (c_mid < k_eff)
        lo = jnp.where(go_up, mid, lo)
        c_lo = jnp.where(go_up, c_mid, c_lo)
        hi = jnp.where(go_dn, mid - 1, hi)
        c_above = jnp.where(go_dn, c_mid, c_above)
        return it + 1, lo, hi, c_lo, c_above

    _, thr, _, c_thr, c_above = lax.while_loop(
        bis_cond, bis_body, (jnp.int32(0), lo0, hi0, n_adm, c_hi0))
    set_rows(thr_ref, thr)
    need = k_eff - c_above
    has_tie = c_thr > k_eff
    any_tie = jnp.max(jnp.where(has_tie, 1.0, 0.0)) > 0.0

    def bias_pass(with_ties):
        def bias_tile(kt, carry):
            for c in range(nch):
                keys = key_ref[kt * nch + c]
                thr_b = thr_ref[...]
                tie = keys == thr_b
                if with_ties:
                    hits = jnp.dot(jnp.where(tie, 1.0, 0.0).astype(BF16), tri_ones,
                                   preferred_element_type=F32)
                    seen = cnt_ref[...]
                    tie = tie & (hits[:, :LANES] + seen <= need_ref[...])
                    cnt_ref[...] = seen + hits[:, LANES:]
                key_ref[kt * nch + c] = jnp.where((keys > thr_b) | tie, 0, NEG_BITS)
            return carry

        lax.fori_loop(0, n_tiles, bias_tile, 0)

    @pl.when(any_tie)
    def _():
        set_rows(need_ref, jnp.where(has_tie, need, 2.0 ** 30))
        cnt_ref[...] = jnp.zeros((tq, LANES), F32)
        bias_pass(True)

    @pl.when(jnp.logical_not(any_tie))
    def _():
        bias_pass(False)

    for hp in range(A_HEADS // 2):
        ls = slice(hp * LANES, (hp + 1) * LANES)
        q_st = _split_lane_halves(q_ref[:, ls].astype(F32), lane)
        m_ref[...] = jnp.full(m_ref.shape, NEG, F32)
        l_ref[...] = jnp.zeros(l_ref.shape, F32)
        acc_ref[...] = jnp.zeros(acc_ref.shape, F32)

        def logits(kt):
            r0 = pl.multiple_of(kt * tk, tk)
            return lax.dot_general(q_st, k_ref[pl.ds(r0, tk), ls], _NT, preferred_element_type=F32)

        def att_tile(kt, carry):
            s = logits(kt)
            chunks = []
            for c in range(nch):
                bias = lax.bitcast_convert_type(key_ref[kt * nch + c], F32)
                sc = s[:, c * LANES:(c + 1) * LANES]
                chunks.append(jnp.concatenate([sc[:tq] + bias, sc[tq:] + bias], axis=0))
            r0 = pl.multiple_of(kt * tk, tk)
            _online_softmax_step(chunks, v_ref[pl.ds(r0, tk), ls], m_ref, l_ref, acc_ref)
            return carry

        lax.fori_loop(0, n_tiles, att_tile, 0)
        o = _softmax_finish(l_ref, acc_ref)
        o_ref[:, ls] = jnp.where(lane < LANES // 2, o[:tq], o[tq:]).astype(o_ref.dtype)


def _dsa_call(qa, ka, va, iq, ik, iw, tq):
    b, s, _ = qa.shape
    topk = min(TOPK_MAX, s // 4)
    qrow = lambda width: pl.BlockSpec((None, tq, width), lambda bi, i: (bi, i, 0))
    seq = lambda width: pl.BlockSpec((None, s, width), lambda bi, i: (bi, 0, 0),
                                     pipeline_mode=pl.Buffered(1))
    return pl.pallas_call(
        functools.partial(_dsa_kernel, tq=tq, topk=topk),
        grid=(b, s // tq),
        in_specs=[qrow(D_A), seq(D_A), seq(D_A), qrow(IDX_HEADS * IDX_DIM), seq(LANES), qrow(LANES)],
        out_specs=qrow(D_A),
        out_shape=jax.ShapeDtypeStruct((b, s, D_A), BF16),
        scratch_shapes=[
            pltpu.VMEM((s // LANES, tq, LANES), I32),
            pltpu.VMEM((IDX_HEADS * tq, LANES), BF16),
            pltpu.VMEM((IDX_HEADS * tq, LANES), F32),
            pltpu.VMEM((tq, LANES), I32),
            pltpu.VMEM((tq, LANES), I32),
            pltpu.VMEM((tq, LANES), I32),
            pltpu.VMEM((tq, LANES), F32),
            pltpu.VMEM((tq, LANES), F32),
            pltpu.VMEM((2 * tq, LANES), F32),
            pltpu.VMEM((2 * tq, LANES), F32),
            pltpu.VMEM((2 * tq, LANES), F32),
        ],
        compiler_params=pltpu.CompilerParams(
            dimension_semantics=("arbitrary", "arbitrary"), vmem_limit_bytes=VMEM_LIMIT),
        name="dsa",
    )(qa, ka, va, iq, ik, iw)


def _diff_kernel(q_ref, k_ref, v_ref, lam_ref, g_ref, o_ref, m_ref, l_ref, acc_ref,
                 *, tq, lam_init):
    tk = tq
    j = pl.program_id(2)
    q0 = j * tq
    lane = lax.broadcasted_iota(I32, (tq, LANES), 1)
    row = lax.broadcasted_iota(I32, (tq, 1), 0)
    q_end = (((q0 + row) >> 6) + 1) << 6

    q_st = _split_lane_halves(q_ref[...].astype(F32), lane)
    m_ref[...] = jnp.full(m_ref.shape, NEG, F32)
    l_ref[...] = jnp.zeros(l_ref.shape, F32)
    acc_ref[...] = jnp.zeros(acc_ref.shape, F32)

    def logits(kt):
        r0 = pl.multiple_of(kt * tk, tk)
        return lax.dot_general(q_st, k_ref[pl.ds(r0, tk), :], _NT, preferred_element_type=F32)

    def att_tile(kt, masked):
        s = logits(kt)
        chunks = []
        for c in range(tk // LANES):
            sc = s[:, c * LANES:(c + 1) * LANES]
            if masked:
                adm = (kt * tk + c * LANES + lane) < q_end
                sc = jnp.concatenate(
                    [jnp.where(adm, sc[:tq], NEG), jnp.where(adm, sc[tq:], NEG)], axis=0)
            chunks.append(sc)
        r0 = pl.multiple_of(kt * tk, tk)
        _online_softmax_step(chunks, v_ref[pl.ds(r0, tk), :], m_ref, l_ref, acc_ref)

    def full_tile(kt, carry):
        att_tile(kt, False)
        return carry

    lax.fori_loop(0, j, full_tile, 0)
    att_tile(j, True)

    lam_p = lam_ref[...]
    lam = (jnp.exp(jnp.sum(lam_p[0:1] * lam_p[1:2], axis=1, keepdims=True))
           - jnp.exp(jnp.sum(lam_p[2:3] * lam_p[3:4], axis=1, keepdims=True)) + lam_init)
    o = _softmax_finish(l_ref, acc_ref)
    o = o[:tq] - lam * o[tq:]
    o_ref[...] = (_rms(o, g_ref[...]) * (1.0 - lam_init)).astype(o_ref.dtype)


def _diff_call(qb, kb, vb, lam_p, g_subln, lam_init, tq):
    b, s, _ = qb.shape
    return pl.pallas_call(
        functools.partial(_diff_kernel, tq=tq, lam_init=lam_init),
        grid=(b, B_HEADS, s // tq),
        in_specs=[
            pl.BlockSpec((None, tq, LANES), lambda bi, h, i: (bi, i, h)),
            pl.BlockSpec((None, s, LANES), lambda bi, h, i: (bi, 0, h)),
            pl.BlockSpec((None, s, LANES), lambda bi, h, i: (bi, 0, h)),
            pl.BlockSpec((4, B_HEAD_DIM), lambda bi, h, i: (0, 0)),
            pl.BlockSpec((1, LANES), lambda bi, h, i: (0, 0)),
        ],
        out_specs=pl.BlockSpec((None, tq, LANES), lambda bi, h, i: (bi, i, h)),
        out_shape=jax.ShapeDtypeStruct((b, s, D_B), BF16),
        scratch_shapes=[
            pltpu.VMEM((2 * tq, LANES), F32),
            pltpu.VMEM((2 * tq, LANES), F32),
            pltpu.VMEM((2 * tq, LANES), F32),
        ],
        compiler_params=pltpu.CompilerParams(
            dimension_semantics=("arbitrary", "arbitrary", "arbitrary"),
            vmem_limit_bytes=VMEM_LIMIT),
        name="diff",
    )(qb, kb, vb, lam_p, g_subln)


def _ffn_kernel(x_ref, a_ref, b_ref, wo_ref, g_ref, wg_ref, wu_ref, wd_ref, gf_ref, o_ref,
                *, ff_chunk, final_norm):
    d_a = a_ref.shape[1]
    x1 = (x_ref[...]
          + jnp.dot(a_ref[...], wo_ref[:d_a, :], preferred_element_type=F32)
          + jnp.dot(b_ref[...], wo_ref[d_a:, :], preferred_element_type=F32))
    h2 = _rms(x1, g_ref[...]).astype(BF16)
    y = None
    for c in range(wg_ref.shape[1] // ff_chunk):
        cs = slice(c * ff_chunk, (c + 1) * ff_chunk)
        gate = jnp.dot(h2, wg_ref[:, cs], preferred_element_type=F32)
        up = jnp.dot(h2, wu_ref[:, cs], preferred_element_type=F32)
        act = (gate / (1.0 + jnp.exp(-gate)) * up).astype(BF16)
        down = jnp.dot(act, wd_ref[cs, :], preferred_element_type=F32)
        y = down if y is None else y + down
    x2 = x1 + y
    if final_norm:
        x2 = _rms(x2, gf_ref[...])
    o_ref[...] = x2


def _ffn_call(x, out_a, out_b, wo, g, wg, wu, wd, g_final, final_norm, tm, ff_chunk):
    t, d = x.shape
    d_ff = wg.shape[1]
    row = lambda width: pl.BlockSpec((tm, width), lambda i: (i, 0))
    const = lambda shape: pl.BlockSpec(shape, lambda i: (0, 0), pipeline_mode=pl.Buffered(1))
    return pl.pallas_call(
        functools.partial(_ffn_kernel, ff_chunk=ff_chunk, final_norm=final_norm),
        grid=(t // tm,),
        in_specs=[row(d), row(out_a.shape[1]), row(out_b.shape[1]), const(wo.shape), const((1, d)),
                  const((d, d_ff)), const((d, d_ff)), const((d_ff, d)), const((1, d))],
        out_specs=row(d),
        out_shape=jax.ShapeDtypeStruct((t, d), F32),
        compiler_params=pltpu.CompilerParams(
            dimension_semantics=("arbitrary",), vmem_limit_bytes=VMEM_LIMIT),
        name="ffn",
    )(x, out_a, out_b, wo, g, wg, wu, wd, g_final)


def _rope_tables(seq_len, dim):
    pos = jnp.arange(seq_len, dtype=F32)
    inv = ROPE_THETA ** (-jnp.arange(0, dim, 2, dtype=F32) / dim)
    ang = pos[:, None] * inv[None, :]
    cos, sin = jnp.cos(ang), jnp.sin(ang)
    reps = LANES // dim
    return (jnp.tile(jnp.concatenate([cos, cos], axis=1), (1, reps)),
            jnp.tile(jnp.concatenate([-sin, sin], axis=1), (1, reps)))


def _pack_w_in(w):
    splits = (D_A, D_A, D_A, IDX_HEADS * IDX_DIM, IDX_DIM, IDX_HEADS, D_B, D_B, D_B)
    offs = [0]
    for sz in splits:
        offs.append(offs[-1] + sz)
    qa, ka, va, iq, ik, iw, qb, kb, vb = (w[:, offs[i]:offs[i + 1]] for i in range(len(splits)))
    ik4 = jnp.tile(ik, (1, LANES // IDX_DIM))
    iw_p = jnp.pad(iw, ((0, 0), (0, LANES - IDX_HEADS)))
    return jnp.concatenate([qa, ka, va, iq, ik4, iw_p, qb, kb, vb], axis=1).astype(BF16)


def kernel(x, w_in, w_out, g_mix, lam_q1, lam_k1, lam_q2, lam_k2, g_subln, g_ffn, w_gate, w_up,
           w_down, g_final):
    b, s, d = x.shape
    depth = w_in.shape[0]
    tm_proj, tq, tm_ffn = 512, 512, 512
    d_ff = w_gate.shape[-1]
    ff_chunk = d_ff // 2
    assert s % tm_proj == 0 and s % tq == 0 and (b * s) % tm_ffn == 0 and ff_chunk % LANES == 0

    tabs = _rope_tables(s, A_HEAD_DIM) + _rope_tables(s, IDX_DIM)
    for layer in range(depth):
        qa, ka, va, iq, ik, iw, qb, kb, vb = _proj_call(
            x, g_mix[layer][None, :], _pack_w_in(w_in[layer]), tabs, tm_proj)
        out_a = _dsa_call(qa, ka, va, iq, ik, iw, tq)
        lam_init = 0.8 - 0.6 * math.exp(-0.3 * layer)
        lam_p = jnp.stack([lam_q1[layer], lam_k1[layer], lam_q2[layer], lam_k2[layer]]).astype(F32)
        out_b = _diff_call(qb, kb, vb, lam_p, g_subln[layer][None, :], lam_init, tq)
        x = _ffn_call(
            x.reshape(b * s, d), out_a.reshape(b * s, D_A), out_b.reshape(b * s, D_B),
            w_out[layer].astype(BF16), g_ffn[layer][None, :], w_gate[layer].astype(BF16),
            w_up[layer].astype(BF16), w_down[layer].astype(BF16), g_final[None, :],
            layer == depth - 1, tm_ffn, ff_chunk).reshape(b, s, d)
    return x
```

```python
import functools
import math

import jax
import jax.numpy as jnp
import numpy as np
from jax import lax
from jax.experimental import pallas as pl
from jax.experimental.pallas import tpu as pltpu

F32 = jnp.float32
BF16 = jnp.bfloat16
I32 = jnp.int32

CHUNK = 64
ROPE_THETA = 10000.0
RMS_EPS = 1e-6
A_HEADS = 8
A_HEAD_DIM = 64
D_A = A_HEADS * A_HEAD_DIM
IDX_HEADS = 8
IDX_DIM = 32
TOPK_MAX = 256
B_HEADS = 4
B_HEAD_DIM = 64
D_B = B_HEADS * 2 * B_HEAD_DIM

LANES = 128
VMEM_LIMIT = 56 * 1024 * 1024

LOG2E = 1.4426950408889634
NEG = -1e30
NEG_BITS = int(np.float32(NEG).view(np.int32))
KEY_MIN = -(2 ** 31)

_OFF_QA, _OFF_KA, _OFF_VA = 0, 512, 1024
_OFF_IQ, _OFF_IK, _OFF_IW = 1536, 1792, 1920
_OFF_QB, _OFF_KB, _OFF_VB = 2048, 2560, 3072
_W_CAT = 3584

_NT = (((1,), (1,)), ((), ()))


def _rms(x, g):
    r = lax.rsqrt(jnp.mean(x * x, axis=-1, keepdims=True) + RMS_EPS)
    return x * r * g


def _proj_kernel(x_ref, g_ref, w_ref, cos_a_ref, sin_a_ref, cos_i_ref, sin_i_ref,
                 qa_ref, ka_ref, va_ref, iq_ref, ik_ref, iw_ref, qb_ref, kb_ref, vb_ref):
    h = _rms(x_ref[...], g_ref[...]).astype(BF16)
    tm = h.shape[0]
    lane = lax.broadcasted_iota(I32, (tm, LANES), 1)

    def proj(off, width):
        return jnp.dot(h, w_ref[:, off:off + width], preferred_element_type=F32)

    def rope_tile(y, cos, sin_signed, half):
        first = (lane & (2 * half - 1)) < half
        rot = jnp.where(first, pltpu.roll(y, LANES - half, 1), pltpu.roll(y, half, 1))
        return y * cos + rot * sin_signed

    def rope_store(out_ref, off, width, cos_ref, sin_ref, half, scale):
        y = proj(off, width)
        cos, sin = cos_ref[...], sin_ref[...]
        for t in range(width // LANES):
            r = rope_tile(y[:, t * LANES:(t + 1) * LANES], cos, sin, half)
            if scale != 1.0:
                r = r * scale
            out_ref[:, t * LANES:(t + 1) * LANES] = r.astype(out_ref.dtype)

    half_a, half_i = A_HEAD_DIM // 2, IDX_DIM // 2
    rope_store(qa_ref, _OFF_QA, D_A, cos_a_ref, sin_a_ref, half_a, A_HEAD_DIM ** -0.5 * LOG2E)
    rope_store(ka_ref, _OFF_KA, D_A, cos_a_ref, sin_a_ref, half_a, 1.0)
    va_ref[...] = proj(_OFF_VA, D_A).astype(va_ref.dtype)
    rope_store(iq_ref, _OFF_IQ, IDX_HEADS * IDX_DIM, cos_i_ref, sin_i_ref, half_i, 1.0)
    rope_store(ik_ref, _OFF_IK, LANES, cos_i_ref, sin_i_ref, half_i, 1.0)
    iw_ref[...] = proj(_OFF_IW, LANES) * (IDX_HEADS ** -0.5 * IDX_DIM ** -0.5)
    rope_store(qb_ref, _OFF_QB, D_B, cos_a_ref, sin_a_ref, half_a, B_HEAD_DIM ** -0.5 * LOG2E)
    rope_store(kb_ref, _OFF_KB, D_B, cos_a_ref, sin_a_ref, half_a, 1.0)
    vb_ref[...] = proj(_OFF_VB, D_B).astype(vb_ref.dtype)


def _proj_call(x, g, w_cat, tabs, tm):
    b, s, d = x.shape
    cos_a, sin_a, cos_i, sin_i = tabs
    row = lambda width: pl.BlockSpec((None, tm, width), lambda bi, i: (bi, i, 0))
    tab = pl.BlockSpec((tm, LANES), lambda bi, i: (i, 0))
    const = lambda shape: pl.BlockSpec(shape, lambda bi, i: (0, 0), pipeline_mode=pl.Buffered(1))
    widths = (D_A, D_A, D_A, IDX_HEADS * IDX_DIM, LANES, LANES, D_B, D_B, D_B)
    dtypes = (BF16, BF16, BF16, BF16, BF16, F32, BF16, BF16, BF16)
    return pl.pallas_call(
        _proj_kernel,
        grid=(b, s // tm),
        in_specs=[row(d), const((1, d)), const((d, _W_CAT)), tab, tab, tab, tab],
        out_specs=[row(w) for w in widths],
        out_shape=[jax.ShapeDtypeStruct((b, s, w), dt) for w, dt in zip(widths, dtypes)],
        compiler_params=pltpu.CompilerParams(
            dimension_semantics=("arbitrary", "arbitrary"), vmem_limit_bytes=VMEM_LIMIT),
        name="proj",
    )(x, g, w_cat, cos_a, sin_a, cos_i, sin_i)


def _online_softmax_step(chunks, v_t, m_ref, l_ref, acc_ref):
    m_prev = m_ref[...]
    m_cur = functools.reduce(jnp.maximum, chunks)
    m_new = jnp.maximum(m_prev, jnp.max(m_cur, axis=1, keepdims=True))
    alpha = jnp.exp2(m_prev - m_new)
    p = [jnp.exp2(c - m_new) for c in chunks]
    l_ref[...] = alpha * l_ref[...] + functools.reduce(jnp.add, p)
    pv = jnp.dot(jnp.concatenate(p, axis=1).astype(BF16), v_t, preferred_element_type=F32)
    acc_ref[...] = alpha * acc_ref[...] + pv
    m_ref[...] = m_new


def _softmax_finish(l_ref, acc_ref):
    return acc_ref[...] / jnp.sum(l_ref[...], axis=1, keepdims=True)


def _split_lane_halves(tile_f32, lane):
    lo = jnp.where(lane < LANES // 2, tile_f32, 0.0)
    hi = jnp.where(lane >= LANES // 2, tile_f32, 0.0)
    return jnp.concatenate([lo, hi], axis=0).astype(BF16)


def _dsa_kernel(q_ref, k_ref, v_ref, iq_ref, ik_ref, iw_ref, o_ref,
                key_ref, iqm_ref, wb_ref, mx2_ref, mx_ref, thr_ref, need_ref, cnt_ref,
                m_ref, l_ref, acc_ref,
                *, tq, topk):
    tk = tq
    nch = tk // LANES
    tki = 256
    rb = 128
    j = pl.program_id(1)
    q0 = j * tq
    n_tiles = j + 1
    lane = lax.broadcasted_iota(I32, (tq, LANES), 1)
    lane_rb = lax.broadcasted_iota(I32, (rb, LANES), 1)
    row = lax.broadcasted_iota(I32, (tq, 1), 0)
    q_end = (((q0 + row) >> 6) + 1) << 6

    nb = tq // rb
    sub8 = lax.broadcasted_iota(I32, (8, LANES), 0)
    lane8 = lax.broadcasted_iota(I32, (8, LANES), 1)
    n_adm = ((((q0 + sub8 * rb + lane8) >> 6) + 1) << 6).astype(F32)
    k_eff = jnp.minimum(n_adm, float(topk))
    ones8 = jnp.ones((8, LANES), BF16)
    tri_r = lax.broadcasted_iota(I32, (LANES, 2 * LANES), 0)
    tri_c = lax.broadcasted_iota(I32, (LANES, 2 * LANES), 1)
    tri_ones = jnp.where((tri_c >= LANES) | (tri_r <= tri_c), 1.0, 0.0).astype(BF16)

    def rows_of(x8, r):
        x = jnp.broadcast_to(x8[r:r + 1, :], (rb, LANES))
        if x8.dtype == I32:
            return lax.bitcast_convert_type(jnp.transpose(lax.bitcast_convert_type(x, F32)), I32)
        return jnp.transpose(x)

    def lanes_of(col):
        out = jnp.zeros((8, LANES), col.dtype)
        for r in range(nb):
            x = jnp.broadcast_to(col[r * rb:(r + 1) * rb], (rb, LANES))
            if col.dtype == I32:
                x = lax.bitcast_convert_type(jnp.transpose(lax.bitcast_convert_type(x, F32)), I32)
            else:
                x = jnp.transpose(x)
            out = jnp.where(sub8 == r, x[0:8, :], out)
        return out

    iw = iw_ref[...]
    for h in range(IDX_HEADS):
        g = h % 4
        tile = iq_ref[:, (h // 4) * LANES:(h // 4 + 1) * LANES].astype(F32)
        keep = (lane >= g * IDX_DIM) & (lane < (g + 1) * IDX_DIM)
        iq_h = jnp.where(keep, tile, 0.0).astype(BF16)
        w_h = jnp.broadcast_to(iw[:, h:h + 1], (tq, LANES))
        for r in range(nb):
            dst = slice((r * IDX_HEADS + h) * rb, (r * IDX_HEADS + h + 1) * rb)
            iqm_ref[dst, :] = iq_h[r * rb:(r + 1) * rb]
            wb_ref[dst, :] = w_h[r * rb:(r + 1) * rb]

    mx2_ref[...] = jnp.full((tq, LANES), KEY_MIN, I32)
    mx_ref[...] = jnp.full((tq, LANES), KEY_MIN, I32)

    def index_tile(it, masked):
        r0 = pl.multiple_of(it * tki, tki)
        ik_t = ik_ref[pl.ds(r0, tki), :]
        for r in range(nb):
            rs = slice(r * rb, (r + 1) * rb)
            base = r * IDX_HEADS * rb
            d = lax.dot_general(iqm_ref[base:base + IDX_HEADS * rb, :], ik_t, _NT,
                                preferred_element_type=F32)
            for c in range(tki // LANES):
                a = jnp.zeros((rb, LANES), F32)
                for h in range(IDX_HEADS):
                    hs = slice(h * rb, (h + 1) * rb)
                    w_h = wb_ref[base + h * rb:base + (h + 1) * rb, :]
                    a = a + w_h * jnp.maximum(d[hs, c * LANES:(c + 1) * LANES], 0.0)
                bits = lax.bitcast_convert_type(a, I32)
                key = bits ^ ((bits >> 31) & 0x7FFFFFFF)
                if masked:
                    adm = (it * tki + c * LANES + lane_rb) < q_end[rs]
                    key = jnp.where(adm, key, KEY_MIN)
                top1 = mx_ref[rs, :]
                mx2_ref[rs, :] = jnp.maximum(mx2_ref[rs, :], jnp.minimum(top1, key))
                mx_ref[rs, :] = jnp.maximum(top1, key)
                key_ref[it * (tki // LANES) + c, rs, :] = key

    def full_tile(it, carry):
        index_tile(it, False)
        return carry

    n_full = q0 // tki
    lax.fori_loop(0, n_full, full_tile, 0)
    for t in range(tq // tki):
        index_tile(n_full + t, True)

    def count_pass(per_block):
        cnt_ref[...] = jnp.zeros((tq, LANES), F32)

        def body(kt, carry):
            for r in range(nb):
                rs = slice(r * rb, (r + 1) * rb)
                pred = per_block(rs)
                acc = cnt_ref[rs, :]
                for c in range(nch):
                    hit = pred(key_ref[kt * nch + c, rs, :], kt * tk + c * LANES)
                    acc = acc + jnp.where(hit, 1.0, 0.0)
                cnt_ref[rs, :] = acc
            return carry

        lax.fori_loop(0, n_tiles, body, 0)
        c8 = lax.dot_general(ones8, cnt_ref[...].astype(BF16), _NT, preferred_element_type=F32)
        cnt = jnp.zeros((8, LANES), F32)
        for r in range(nb):
            cnt = jnp.where(sub8 == r, c8[:, r * rb:(r + 1) * rb], cnt)
        return cnt

    def set_rows(ref, x8):
        for r in range(nb):
            ref[r * rb:(r + 1) * rb, :] = rows_of(x8, r)

    def count_ge(t8):
        set_rows(thr_ref, t8)

        def per_block(rs):
            tb = thr_ref[rs, :]
            return lambda keys, kidx0: keys >= tb

        return count_pass(per_block)

    def flip(k):
        return k ^ ((k >> 31) & 0x7FFFFFFF)

    live = sub8 < nb
    many = 2.0 ** 30
    small = n_adm <= float(topk)
    lo0 = lanes_of(jnp.min(mx2_ref[...], axis=1, keepdims=True))
    lo0 = jnp.where(live, jnp.where(small, KEY_MIN + 1, lo0), 0)
    hi0 = jnp.where(live, lanes_of(jnp.max(mx_ref[...], axis=1, keepdims=True)), 0)
    c_lo0 = jnp.where(small, n_adm, many)
    c_hi0 = jnp.zeros((8, LANES), F32)
    value_steps = 26

    def bis_cond(st):
        it, lo, hi, c_lo, c_above = st
        open_rows = jnp.where((c_lo > k_eff) & (lo < hi), 1.0, 0.0)
        return jnp.max(open_rows) > 0.0

    def bis_body(st):
        it, lo, hi, c_lo, c_above = st
        mid_k = (lo >> 1) + (hi >> 1) + ((lo | hi) & 1)
        v_lo = lax.bitcast_convert_type(flip(lo), F32)
        v_hi = lax.bitcast_convert_type(flip(hi), F32)
        mid_v = flip(lax.bitcast_convert_type(0.5 * v_lo + 0.5 * v_hi, I32))
        mid_v = jnp.where((lo < 0) & (hi >= 0), 0, jnp.where((lo == 0) & (hi >= 1), 1, mid_v))
        mid_v = jnp.minimum(jnp.maximum(mid_v, lo + 1), hi)
        mid = jnp.where(it < value_steps, mid_v, mid_k)
        active = (c_lo > k_eff) & (lo < hi)
        c_mid = count_ge(mid)
        go_up = active & (c_mid >= k_eff)
        go_dn = active & (c_mid < k_eff)
        lo = jnp.where(go_up, mid, lo)
        c_lo = jnp.where(go_up, c_mid, c_lo)
        hi = jnp.where(go_dn, mid - 1, hi)
        c_above = jnp.where(go_dn, c_mid, c_above)
        return it + 1, lo, hi, c_lo, c_above

    _, thr, _, c_thr, c_above = lax.while_loop(
        bis_cond, bis_body, (jnp.int32(0), lo0, hi0, c_lo0, c_hi0))
    set_rows(thr_ref, thr)
    need = k_eff - c_above
    has_tie = c_thr > k_eff
    any_tie = jnp.max(jnp.where(has_tie, 1.0, 0.0)) > 0.0

    def bias_pass(with_ties):
        def bias_tile(kt, carry):
            for c in range(nch):
                keys = key_ref[kt * nch + c]
                thr_b = thr_ref[...]
                tie = keys == thr_b
                if with_ties:
                    hits = jnp.dot(jnp.where(tie, 1.0, 0.0).astype(BF16), tri_ones,
                                   preferred_element_type=F32)
                    seen = cnt_ref[...]
                    tie = tie & (hits[:, :LANES] + seen <= need_ref[...])
                    cnt_ref[...] = seen + hits[:, LANES:]
                key_ref[kt * nch + c] = jnp.where((keys > thr_b) | tie, 0, NEG_BITS)
            return carry

        lax.fori_loop(0, n_tiles, bias_tile, 0)

    @pl.when(any_tie)
    def _():
        set_rows(need_ref, jnp.where(has_tie, need, 2.0 ** 30))
        cnt_ref[...] = jnp.zeros((tq, LANES), F32)
        bias_pass(True)

    @pl.when(jnp.logical_not(any_tie))
    def _():
        bias_pass(False)

    for hp in range(A_HEADS // 2):
        ls = slice(hp * LANES, (hp + 1) * LANES)
        q_st = _split_lane_halves(q_ref[:, ls].astype(F32), lane)
        m_ref[...] = jnp.full(m_ref.shape, NEG, F32)
        l_ref[...] = jnp.zeros(l_ref.shape, F32)
        acc_ref[...] = jnp.zeros(acc_ref.shape, F32)

        def logits(kt):
            r0 = pl.multiple_of(kt * tk, tk)
            return lax.dot_general(q_st, k_ref[pl.ds(r0, tk), ls], _NT, preferred_element_type=F32)

        def att_tile(kt, carry):
            s = logits(kt)
            chunks = []
            for c in range(nch):
                bias = lax.bitcast_convert_type(key_ref[kt * nch + c], F32)
                sc = s[:, c * LANES:(c + 1) * LANES]
                chunks.append(jnp.concatenate([sc[:tq] + bias, sc[tq:] + bias], axis=0))
            r0 = pl.multiple_of(kt * tk, tk)
            _online_softmax_step(chunks, v_ref[pl.ds(r0, tk), ls], m_ref, l_ref, acc_ref)
            return carry

        lax.fori_loop(0, n_tiles, att_tile, 0)
        o = _softmax_finish(l_ref, acc_ref)
        o_ref[:, ls] = jnp.where(lane < LANES // 2, o[:tq], o[tq:]).astype(o_ref.dtype)


def _dsa_call(qa, ka, va, iq, ik, iw, tq):
    b, s, _ = qa.shape
    topk = min(TOPK_MAX, s // 4)
    qrow = lambda width: pl.BlockSpec((None, tq, width), lambda bi, i: (bi, i, 0))
    seq = lambda width: pl.BlockSpec((None, s, width), lambda bi, i: (bi, 0, 0),
                                     pipeline_mode=pl.Buffered(1))
    return pl.pallas_call(
        functools.partial(_dsa_kernel, tq=tq, topk=topk),
        grid=(b, s // tq),
        in_specs=[qrow(D_A), seq(D_A), seq(D_A), qrow(IDX_HEADS * IDX_DIM), seq(LANES), qrow(LANES)],
        out_specs=qrow(D_A),
        out_shape=jax.ShapeDtypeStruct((b, s, D_A), BF16),
        scratch_shapes=[
            pltpu.VMEM((s // LANES, tq, LANES), I32),
            pltpu.VMEM((IDX_HEADS * tq, LANES), BF16),
            pltpu.VMEM((IDX_HEADS * tq, LANES), F32),
            pltpu.VMEM((tq, LANES), I32),
            pltpu.VMEM((tq, LANES), I32),
            pltpu.VMEM((tq, LANES), I32),
            pltpu.VMEM((tq, LANES), F32),
            pltpu.VMEM((tq, LANES), F32),
            pltpu.VMEM((2 * tq, LANES), F32),
            pltpu.VMEM((2 * tq, LANES), F32),
            pltpu.VMEM((2 * tq, LANES), F32),
        ],
        compiler_params=pltpu.CompilerParams(
            dimension_semantics=("arbitrary", "arbitrary"), vmem_limit_bytes=VMEM_LIMIT),
        name="dsa",
    )(qa, ka, va, iq, ik, iw)


def _diff_kernel(q_ref, k_ref, v_ref, lam_ref, g_ref, o_ref, m_ref, l_ref, acc_ref,
                 *, tq, tk, lam_init):
    j = pl.program_id(2)
    q0 = j * tq
    lane = lax.broadcasted_iota(I32, (tq, LANES), 1)
    row = lax.broadcasted_iota(I32, (tq, 1), 0)
    q_end = (((q0 + row) >> 6) + 1) << 6

    q_st = _split_lane_halves(q_ref[...].astype(F32), lane)
    m_ref[...] = jnp.full(m_ref.shape, NEG, F32)
    l_ref[...] = jnp.zeros(l_ref.shape, F32)
    acc_ref[...] = jnp.zeros(acc_ref.shape, F32)

    def logits(kt):
        r0 = pl.multiple_of(kt * tk, tk)
        return lax.dot_general(q_st, k_ref[pl.ds(r0, tk), :], _NT, preferred_element_type=F32)

    def att_tile(kt, masked):
        s = logits(kt)
        chunks = []
        for c in range(tk // LANES):
            sc = s[:, c * LANES:(c + 1) * LANES]
            if masked:
                adm = (kt * tk + c * LANES + lane) < q_end
                sc = jnp.concatenate(
                    [jnp.where(adm, sc[:tq], NEG), jnp.where(adm, sc[tq:], NEG)], axis=0)
            chunks.append(sc)
        r0 = pl.multiple_of(kt * tk, tk)
        _online_softmax_step(chunks, v_ref[pl.ds(r0, tk), :], m_ref, l_ref, acc_ref)

    def full_tile(kt, carry):
        att_tile(kt, False)
        return carry

    n_full = j * (tq // tk)
    lax.fori_loop(0, n_full, full_tile, 0)
    for t in range(tq // tk):
        att_tile(n_full + t, True)

    lam_p = lam_ref[...]
    lam = (jnp.exp(jnp.sum(lam_p[0:1] * lam_p[1:2], axis=1, keepdims=True))
           - jnp.exp(jnp.sum(lam_p[2:3] * lam_p[3:4], axis=1, keepdims=True)) + lam_init)
    o = _softmax_finish(l_ref, acc_ref)
    o = o[:tq] - lam * o[tq:]
    o_ref[...] = (_rms(o, g_ref[...]) * (1.0 - lam_init)).astype(o_ref.dtype)


def _diff_call(qb, kb, vb, lam_p, g_subln, lam_init, tq, tk):
    b, s, _ = qb.shape
    return pl.pallas_call(
        functools.partial(_diff_kernel, tq=tq, tk=tk, lam_init=lam_init),
        grid=(b, B_HEADS, s // tq),
        in_specs=[
            pl.BlockSpec((None, tq, LANES), lambda bi, h, i: (bi, i, h)),
            pl.BlockSpec((None, s, LANES), lambda bi, h, i: (bi, 0, h)),
            pl.BlockSpec((None, s, LANES), lambda bi, h, i: (bi, 0, h)),
            pl.BlockSpec((4, B_HEAD_DIM), lambda bi, h, i: (0, 0)),
            pl.BlockSpec((1, LANES), lambda bi, h, i: (0, 0)),
        ],
        out_specs=pl.BlockSpec((None, tq, LANES), lambda bi, h, i: (bi, i, h)),
        out_shape=jax.ShapeDtypeStruct((b, s, D_B), BF16),
        scratch_shapes=[
            pltpu.VMEM((2 * tq, LANES), F32),
            pltpu.VMEM((2 * tq, LANES), F32),
            pltpu.VMEM((2 * tq, LANES), F32),
        ],
        compiler_params=pltpu.CompilerParams(
            dimension_semantics=("arbitrary", "arbitrary", "arbitrary"),
            vmem_limit_bytes=VMEM_LIMIT),
        name="diff",
    )(qb, kb, vb, lam_p, g_subln)


def _ffn_kernel(x_ref, a_ref, b_ref, wo_ref, g_ref, wg_ref, wu_ref, wd_ref, gf_ref, o_ref,
                *, ff_chunk, final_norm):
    d_a = a_ref.shape[1]
    x1 = (x_ref[...]
          + jnp.dot(a_ref[...], wo_ref[:d_a, :], preferred_element_type=F32)
          + jnp.dot(b_ref[...], wo_ref[d_a:, :], preferred_element_type=F32))
    h2 = _rms(x1, g_ref[...]).astype(BF16)
    y = None
    for c in range(wg_ref.shape[1] // ff_chunk):
        cs = slice(c * ff_chunk, (c + 1) * ff_chunk)
        gate = jnp.dot(h2, wg_ref[:, cs], preferred_element_type=F32)
        up = jnp.dot(h2, wu_ref[:, cs], preferred_element_type=F32)
        act = (gate / (1.0 + jnp.exp(-gate)) * up).astype(BF16)
        down = jnp.dot(act, wd_ref[cs, :], preferred_element_type=F32)
        y = down if y is None else y + down
    x2 = x1 + y
    if final_norm:
        x2 = _rms(x2, gf_ref[...])
    o_ref[...] = x2


def _ffn_call(x, out_a, out_b, wo, g, wg, wu, wd, g_final, final_norm, tm, ff_chunk):
    t, d = x.shape
    d_ff = wg.shape[1]
    row = lambda width: pl.BlockSpec((tm, width), lambda i: (i, 0))
    const = lambda shape: pl.BlockSpec(shape, lambda i: (0, 0), pipeline_mode=pl.Buffered(1))
    return pl.pallas_call(
        functools.partial(_ffn_kernel, ff_chunk=ff_chunk, final_norm=final_norm),
        grid=(t // tm,),
        in_specs=[row(d), row(out_a.shape[1]), row(out_b.shape[1]), const(wo.shape), const((1, d)),
                  const((d, d_ff)), const((d, d_ff)), const((d_ff, d)), const((1, d))],
        out_specs=row(d),
        out_shape=jax.ShapeDtypeStruct((t, d), F32),
        compiler_params=pltpu.CompilerParams(
            dimension_semantics=("arbitrary",), vmem_limit_bytes=VMEM_LIMIT),
        name="ffn",
    )(x, out_a, out_b, wo, g, wg, wu, wd, g_final)


def _rope_tables(seq_len, dim):
    pos = jnp.arange(seq_len, dtype=F32)
    inv = ROPE_THETA ** (-jnp.arange(0, dim, 2, dtype=F32) / dim)
    ang = pos[:, None] * inv[None, :]
    cos, sin = jnp.cos(ang), jnp.sin(ang)
    reps = LANES // dim
    return (jnp.tile(jnp.concatenate([cos, cos], axis=1), (1, reps)),
            jnp.tile(jnp.concatenate([-sin, sin], axis=1), (1, reps)))


def _pack_w_in(w):
    splits = (D_A, D_A, D_A, IDX_HEADS * IDX_DIM, IDX_DIM, IDX_HEADS, D_B, D_B, D_B)
    offs = [0]
    for sz in splits:
        offs.append(offs[-1] + sz)
    qa, ka, va, iq, ik, iw, qb, kb, vb = (w[:, offs[i]:offs[i + 1]] for i in range(len(splits)))
    ik4 = jnp.tile(ik, (1, LANES // IDX_DIM))
    iw_p = jnp.pad(iw, ((0, 0), (0, LANES - IDX_HEADS)))
    return jnp.concatenate([qa, ka, va, iq, ik4, iw_p, qb, kb, vb], axis=1).astype(BF16)


def kernel(x, w_in, w_out, g_mix, lam_q1, lam_k1, lam_q2, lam_k2, g_subln, g_ffn, w_gate, w_up,
           w_down, g_final):
    b, s, d = x.shape
    depth = w_in.shape[0]
    tm_proj, tq, tm_ffn = 512, 512, 512
    tq_diff = 2 * tq if s % (2 * tq) == 0 else tq
    d_ff = w_gate.shape[-1]
    ff_chunk = d_ff // 2
    assert s % tm_proj == 0 and s % tq == 0 and (b * s) % tm_ffn == 0 and ff_chunk % LANES == 0

    tabs = _rope_tables(s, A_HEAD_DIM) + _rope_tables(s, IDX_DIM)
    for layer in range(depth):
        qa, ka, va, iq, ik, iw, qb, kb, vb = _proj_call(
            x, g_mix[layer][None, :], _pack_w_in(w_in[layer]), tabs, tm_proj)
        out_a = _dsa_call(qa, ka, va, iq, ik, iw, tq)
        lam_init = 0.8 - 0.6 * math.exp(-0.3 * layer)
        lam_p = jnp.stack([lam_q1[layer], lam_k1[layer], lam_q2[layer], lam_k2[layer]]).astype(F32)
        out_b = _diff_call(qb, kb, vb, lam_p, g_subln[layer][None, :], lam_init, tq_diff, tq_diff)
        x = _ffn_call(
            x.reshape(b * s, d), out_a.reshape(b * s, D_A), out_b.reshape(b * s, D_B),
            w_out[layer].astype(BF16), g_ffn[layer][None, :], w_gate[layer].astype(BF16),
            w_up[layer].astype(BF16), w_down[layer].astype(BF16), g_final[None, :],
            layer == depth - 1, tm_ffn, ff_chunk).reshape(b, s, d)
    return x
```

```python
import functools
import math

import jax
import jax.numpy as jnp
import numpy as np
from jax import lax
from jax.experimental import pallas as pl
from jax.experimental.pallas import tpu as pltpu

F32 = jnp.float32
BF16 = jnp.bfloat16
I32 = jnp.int32

CHUNK = 64
ROPE_THETA = 10000.0
RMS_EPS = 1e-6
A_HEADS = 8
A_HEAD_DIM = 64
D_A = A_HEADS * A_HEAD_DIM
IDX_HEADS = 8
IDX_DIM = 32
TOPK_MAX = 256
B_HEADS = 4
B_HEAD_DIM = 64
D_B = B_HEADS * 2 * B_HEAD_DIM

LANES = 128
VMEM_LIMIT = 56 * 1024 * 1024

LOG2E = 1.4426950408889634
NEG = -1e30
NEG_BITS = int(np.float32(NEG).view(np.int32))
KEY_MIN = -(2 ** 31)

_OFF_QA, _OFF_KA, _OFF_VA = 0, 512, 1024
_OFF_IQ, _OFF_IK, _OFF_IW = 1536, 1792, 1920
_OFF_QB, _OFF_KB, _OFF_VB = 2048, 2560, 3072
_W_CAT = 3584

_NT = (((1,), (1,)), ((), ()))


def _rms(x, g):
    r = lax.rsqrt(jnp.mean(x * x, axis=-1, keepdims=True) + RMS_EPS)
    return x * r * g


def _proj_kernel(x_ref, g_ref, w_ref, cos_a_ref, sin_a_ref, cos_i_ref, sin_i_ref,
                 qa_ref, ka_ref, va_ref, iq_ref, ik_ref, iw_ref, qb_ref, kb_ref, vb_ref):
    h = _rms(x_ref[...], g_ref[...]).astype(BF16)
    tm = h.shape[0]
    lane = lax.broadcasted_iota(I32, (tm, LANES), 1)

    def proj(off, width):
        return jnp.dot(h, w_ref[:, off:off + width], preferred_element_type=F32)

    def rope_tile(y, cos, sin_signed, half):
        first = (lane & (2 * half - 1)) < half
        rot = jnp.where(first, pltpu.roll(y, LANES - half, 1), pltpu.roll(y, half, 1))
        return y * cos + rot * sin_signed

    def rope_store(out_ref, off, width, cos_ref, sin_ref, half, scale):
        y = proj(off, width)
        cos, sin = cos_ref[...], sin_ref[...]
        for t in range(width // LANES):
            r = rope_tile(y[:, t * LANES:(t + 1) * LANES], cos, sin, half)
            if scale != 1.0:
                r = r * scale
            out_ref[:, t * LANES:(t + 1) * LANES] = r.astype(out_ref.dtype)

    half_a, half_i = A_HEAD_DIM // 2, IDX_DIM // 2
    rope_store(qa_ref, _OFF_QA, D_A, cos_a_ref, sin_a_ref, half_a, A_HEAD_DIM ** -0.5 * LOG2E)
    rope_store(ka_ref, _OFF_KA, D_A, cos_a_ref, sin_a_ref, half_a, 1.0)
    va_ref[...] = proj(_OFF_VA, D_A).astype(va_ref.dtype)
    rope_store(iq_ref, _OFF_IQ, IDX_HEADS * IDX_DIM, cos_i_ref, sin_i_ref, half_i, 1.0)
    rope_store(ik_ref, _OFF_IK, LANES, cos_i_ref, sin_i_ref, half_i, 1.0)
    iw_ref[...] = proj(_OFF_IW, LANES) * (IDX_HEADS ** -0.5 * IDX_DIM ** -0.5)
    rope_store(qb_ref, _OFF_QB, D_B, cos_a_ref, sin_a_ref, half_a, B_HEAD_DIM ** -0.5 * LOG2E)
    rope_store(kb_ref, _OFF_KB, D_B, cos_a_ref, sin_a_ref, half_a, 1.0)
    vb_ref[...] = proj(_OFF_VB, D_B).astype(vb_ref.dtype)


def _proj_call(x, g, w_cat, tabs, tm):
    b, s, d = x.shape
    cos_a, sin_a, cos_i, sin_i = tabs
    row = lambda width: pl.BlockSpec((None, tm, width), lambda bi, i: (bi, i, 0))
    tab = pl.BlockSpec((tm, LANES), lambda bi, i: (i, 0))
    const = lambda shape: pl.BlockSpec(shape, lambda bi, i: (0, 0), pipeline_mode=pl.Buffered(1))
    widths = (D_A, D_A, D_A, IDX_HEADS * IDX_DIM, LANES, LANES, D_B, D_B, D_B)
    dtypes = (BF16, BF16, BF16, BF16, BF16, F32, BF16, BF16, BF16)
    return pl.pallas_call(
        _proj_kernel,
        grid=(b, s // tm),
        in_specs=[row(d), const((1, d)), const((d, _W_CAT)), tab, tab, tab, tab],
        out_specs=[row(w) for w in widths],
        out_shape=[jax.ShapeDtypeStruct((b, s, w), dt) for w, dt in zip(widths, dtypes)],
        compiler_params=pltpu.CompilerParams(
            dimension_semantics=("arbitrary", "arbitrary"), vmem_limit_bytes=VMEM_LIMIT),
        name="proj",
    )(x, g, w_cat, cos_a, sin_a, cos_i, sin_i)


def _online_softmax_step(chunks, v_t, m_ref, l_ref, acc_ref):
    m_prev = m_ref[...]
    m_cur = functools.reduce(jnp.maximum, chunks)
    m_new = jnp.maximum(m_prev, jnp.max(m_cur, axis=1, keepdims=True))
    alpha = jnp.exp2(m_prev - m_new)
    p = [jnp.exp2(c - m_new) for c in chunks]
    l_ref[...] = alpha * l_ref[...] + functools.reduce(jnp.add, p)
    pv = jnp.dot(jnp.concatenate(p, axis=1).astype(BF16), v_t, preferred_element_type=F32)
    acc_ref[...] = alpha * acc_ref[...] + pv
    m_ref[...] = m_new


def _softmax_finish(l_ref, acc_ref):
    return acc_ref[...] / jnp.sum(l_ref[...], axis=1, keepdims=True)


def _split_lane_halves(tile_f32, lane):
    lo = jnp.where(lane < LANES // 2, tile_f32, 0.0)
    hi = jnp.where(lane >= LANES // 2, tile_f32, 0.0)
    return jnp.concatenate([lo, hi], axis=0).astype(BF16)


def _dsa_kernel(q_ref, k_ref, v_ref, iq_ref, ik_ref, iw_ref, o_ref,
                key_ref, iqm_ref, wb_ref, mx2_ref, mx_ref, thr_ref, need_ref, cnt_ref,
                m_ref, l_ref, acc_ref,
                *, tq, topk, att_ratio):
    tk = tq
    nch = tk // LANES
    tki = 256
    rb = 128
    j = pl.program_id(1)
    q0 = j * tq
    n_tiles = j + 1
    lane = lax.broadcasted_iota(I32, (tq, LANES), 1)
    lane_rb = lax.broadcasted_iota(I32, (rb, LANES), 1)
    row = lax.broadcasted_iota(I32, (tq, 1), 0)
    q_end = (((q0 + row) >> 6) + 1) << 6

    nb = tq // rb
    sub8 = lax.broadcasted_iota(I32, (8, LANES), 0)
    lane8 = lax.broadcasted_iota(I32, (8, LANES), 1)
    n_adm = ((((q0 + sub8 * rb + lane8) >> 6) + 1) << 6).astype(F32)
    k_eff = jnp.minimum(n_adm, float(topk))
    ones8 = jnp.ones((8, LANES), BF16)
    tri_r = lax.broadcasted_iota(I32, (LANES, 2 * LANES), 0)
    tri_c = lax.broadcasted_iota(I32, (LANES, 2 * LANES), 1)
    tri_ones = jnp.where((tri_c >= LANES) | (tri_r <= tri_c), 1.0, 0.0).astype(BF16)

    def rows_of(x8, r):
        x = jnp.broadcast_to(x8[r:r + 1, :], (rb, LANES))
        if x8.dtype == I32:
            return lax.bitcast_convert_type(jnp.transpose(lax.bitcast_convert_type(x, F32)), I32)
        return jnp.transpose(x)

    def lanes_of(col):
        out = jnp.zeros((8, LANES), col.dtype)
        for r in range(nb):
            x = jnp.broadcast_to(col[r * rb:(r + 1) * rb], (rb, LANES))
            if col.dtype == I32:
                x = lax.bitcast_convert_type(jnp.transpose(lax.bitcast_convert_type(x, F32)), I32)
            else:
                x = jnp.transpose(x)
            out = jnp.where(sub8 == r, x[0:8, :], out)
        return out

    iw = iw_ref[...]
    for h in range(IDX_HEADS):
        g = h % 4
        tile = iq_ref[:, (h // 4) * LANES:(h // 4 + 1) * LANES].astype(F32)
        keep = (lane >= g * IDX_DIM) & (lane < (g + 1) * IDX_DIM)
        iq_h = jnp.where(keep, tile, 0.0).astype(BF16)
        w_h = jnp.broadcast_to(iw[:, h:h + 1], (tq, LANES))
        for r in range(nb):
            dst = slice((r * IDX_HEADS + h) * rb, (r * IDX_HEADS + h + 1) * rb)
            iqm_ref[dst, :] = iq_h[r * rb:(r + 1) * rb]
            wb_ref[dst, :] = w_h[r * rb:(r + 1) * rb]

    mx2_ref[...] = jnp.full((tq, LANES), KEY_MIN, I32)
    mx_ref[...] = jnp.full((tq, LANES), KEY_MIN, I32)

    def index_tile(it, masked):
        r0 = pl.multiple_of(it * tki, tki)
        ik_t = ik_ref[pl.ds(r0, tki), :]
        for r in range(nb):
            rs = slice(r * rb, (r + 1) * rb)
            base = r * IDX_HEADS * rb
            d = lax.dot_general(iqm_ref[base:base + IDX_HEADS * rb, :], ik_t, _NT,
                                preferred_element_type=F32)
            for c in range(tki // LANES):
                a = jnp.zeros((rb, LANES), F32)
                for h in range(IDX_HEADS):
                    hs = slice(h * rb, (h + 1) * rb)
                    w_h = wb_ref[base + h * rb:base + (h + 1) * rb, :]
                    a = a + w_h * jnp.maximum(d[hs, c * LANES:(c + 1) * LANES], 0.0)
                bits = lax.bitcast_convert_type(a, I32)
                key = bits ^ ((bits >> 31) & 0x7FFFFFFF)
                if masked:
                    adm = (it * tki + c * LANES + lane_rb) < q_end[rs]
                    key = jnp.where(adm, key, KEY_MIN)
                top1 = mx_ref[rs, :]
                mx2_ref[rs, :] = jnp.maximum(mx2_ref[rs, :], jnp.minimum(top1, key))
                mx_ref[rs, :] = jnp.maximum(top1, key)
                key_ref[it * (tki // LANES) + c, rs, :] = key

    def full_tile(it, carry):
        index_tile(it, False)
        return carry

    n_full = q0 // tki
    lax.fori_loop(0, n_full, full_tile, 0)
    for t in range(tq // tki):
        index_tile(n_full + t, True)

    def count_pass(per_block):
        cnt_ref[...] = jnp.zeros((tq, LANES), F32)

        def body(kt, carry):
            for r in range(nb):
                rs = slice(r * rb, (r + 1) * rb)
                pred = per_block(rs)
                acc = cnt_ref[rs, :]
                for c in range(nch):
                    hit = pred(key_ref[kt * nch + c, rs, :], kt * tk + c * LANES)
                    acc = acc + jnp.where(hit, 1.0, 0.0)
                cnt_ref[rs, :] = acc
            return carry

        lax.fori_loop(0, n_tiles, body, 0)
        c8 = lax.dot_general(ones8, cnt_ref[...].astype(BF16), _NT, preferred_element_type=F32)
        cnt = jnp.zeros((8, LANES), F32)
        for r in range(nb):
            cnt = jnp.where(sub8 == r, c8[:, r * rb:(r + 1) * rb], cnt)
        return cnt

    def set_rows(ref, x8):
        for r in range(nb):
            ref[r * rb:(r + 1) * rb, :] = rows_of(x8, r)

    def count_ge(t8):
        set_rows(thr_ref, t8)

        def per_block(rs):
            tb = thr_ref[rs, :]
            return lambda keys, kidx0: keys >= tb

        return count_pass(per_block)

    def flip(k):
        return k ^ ((k >> 31) & 0x7FFFFFFF)

    live = sub8 < nb
    many = 2.0 ** 30
    small = n_adm <= float(topk)
    lo0 = lanes_of(jnp.min(mx2_ref[...], axis=1, keepdims=True))
    lo0 = jnp.where(live, jnp.where(small, KEY_MIN + 1, lo0), 0)
    hi0 = jnp.where(live, lanes_of(jnp.max(mx_ref[...], axis=1, keepdims=True)), 0)
    c_lo0 = jnp.where(small, n_adm, many)
    c_hi0 = jnp.zeros((8, LANES), F32)
    value_steps = 26

    def bis_cond(st):
        it, lo, hi, c_lo, c_above = st
        open_rows = jnp.where((c_lo > k_eff) & (lo < hi), 1.0, 0.0)
        return jnp.max(open_rows) > 0.0

    def bis_body(st):
        it, lo, hi, c_lo, c_above = st
        mid_k = (lo >> 1) + (hi >> 1) + ((lo | hi) & 1)
        v_lo = lax.bitcast_convert_type(flip(lo), F32)
        v_hi = lax.bitcast_convert_type(flip(hi), F32)
        mid_v = flip(lax.bitcast_convert_type(0.5 * v_lo + 0.5 * v_hi, I32))
        mid_v = jnp.where((lo < 0) & (hi >= 0), 0, jnp.where((lo == 0) & (hi >= 1), 1, mid_v))
        mid_v = jnp.minimum(jnp.maximum(mid_v, lo + 1), hi)
        mid = jnp.where(it < value_steps, mid_v, mid_k)
        active = (c_lo > k_eff) & (lo < hi)
        c_mid = count_ge(mid)
        go_up = active & (c_mid >= k_eff)
        go_dn = active & (c_mid < k_eff)
        lo = jnp.where(go_up, mid, lo)
        c_lo = jnp.where(go_up, c_mid, c_lo)
        hi = jnp.where(go_dn, mid - 1, hi)
        c_above = jnp.where(go_dn, c_mid, c_above)
        return it + 1, lo, hi, c_lo, c_above

    _, thr, _, c_thr, c_above = lax.while_loop(
        bis_cond, bis_body, (jnp.int32(0), lo0, hi0, c_lo0, c_hi0))
    set_rows(thr_ref, thr)
    need = k_eff - c_above
    has_tie = c_thr > k_eff
    any_tie = jnp.max(jnp.where(has_tie, 1.0, 0.0)) > 0.0

    def bias_pass(with_ties):
        def bias_tile(kt, carry):
            for c in range(nch):
                keys = key_ref[kt * nch + c]
                thr_b = thr_ref[...]
                tie = keys == thr_b
                if with_ties:
                    hits = jnp.dot(jnp.where(tie, 1.0, 0.0).astype(BF16), tri_ones,
                                   preferred_element_type=F32)
                    seen = cnt_ref[...]
                    tie = tie & (hits[:, :LANES] + seen <= need_ref[...])
                    cnt_ref[...] = seen + hits[:, LANES:]
                key_ref[kt * nch + c] = jnp.where((keys > thr_b) | tie, 0, NEG_BITS)
            return carry

        lax.fori_loop(0, n_tiles, bias_tile, 0)

    @pl.when(any_tie)
    def _():
        set_rows(need_ref, jnp.where(has_tie, need, 2.0 ** 30))
        cnt_ref[...] = jnp.zeros((tq, LANES), F32)
        bias_pass(True)

    @pl.when(jnp.logical_not(any_tie))
    def _():
        bias_pass(False)

    n_wide = n_tiles >> (att_ratio.bit_length() - 1)

    for hp in range(A_HEADS // 2):
        ls = slice(hp * LANES, (hp + 1) * LANES)
        q_st = _split_lane_halves(q_ref[:, ls].astype(F32), lane)
        m_ref[...] = jnp.full(m_ref.shape, NEG, F32)
        l_ref[...] = jnp.zeros(l_ref.shape, F32)
        acc_ref[...] = jnp.zeros(acc_ref.shape, F32)

        def att_step(kt, width):
            r0 = pl.multiple_of(kt * (width * tk), width * tk)
            s = lax.dot_general(q_st, k_ref[pl.ds(r0, width * tk), ls], _NT,
                                preferred_element_type=F32)
            chunks = []
            for c in range(width * nch):
                bias = lax.bitcast_convert_type(key_ref[kt * (width * nch) + c], F32)
                sc = s[:, c * LANES:(c + 1) * LANES]
                chunks.append(jnp.concatenate([sc[:tq] + bias, sc[tq:] + bias], axis=0))
            _online_softmax_step(chunks, v_ref[pl.ds(r0, width * tk), ls], m_ref, l_ref, acc_ref)

        def wide_step(kt, carry):
            att_step(kt, att_ratio)
            return carry

        def single_step(kt, carry):
            att_step(kt, 1)
            return carry

        lax.fori_loop(0, n_wide, wide_step, 0)
        lax.fori_loop(n_wide * att_ratio, n_tiles, single_step, 0)
        o = _softmax_finish(l_ref, acc_ref)
        o_ref[:, ls] = jnp.where(lane < LANES // 2, o[:tq], o[tq:]).astype(o_ref.dtype)


def _dsa_call(qa, ka, va, iq, ik, iw, tq):
    b, s, _ = qa.shape
    topk = min(TOPK_MAX, s // 4)
    qrow = lambda width: pl.BlockSpec((None, tq, width), lambda bi, i: (bi, i, 0))
    seq = lambda width: pl.BlockSpec((None, s, width), lambda bi, i: (bi, 0, 0),
                                     pipeline_mode=pl.Buffered(1))
    att_ratio = 2 if s % (2 * tq) == 0 else 1
    return pl.pallas_call(
        functools.partial(_dsa_kernel, tq=tq, topk=topk, att_ratio=att_ratio),
        grid=(b, s // tq),
        in_specs=[qrow(D_A), seq(D_A), seq(D_A), qrow(IDX_HEADS * IDX_DIM), seq(LANES), qrow(LANES)],
        out_specs=qrow(D_A),
        out_shape=jax.ShapeDtypeStruct((b, s, D_A), BF16),
        scratch_shapes=[
            pltpu.VMEM((s // LANES, tq, LANES), I32),
            pltpu.VMEM((IDX_HEADS * tq, LANES), BF16),
            pltpu.VMEM((IDX_HEADS * tq, LANES), F32),
            pltpu.VMEM((tq, LANES), I32),
            pltpu.VMEM((tq, LANES), I32),
            pltpu.VMEM((tq, LANES), I32),
            pltpu.VMEM((tq, LANES), F32),
            pltpu.VMEM((tq, LANES), F32),
            pltpu.VMEM((2 * tq, LANES), F32),
            pltpu.VMEM((2 * tq, LANES), F32),
            pltpu.VMEM((2 * tq, LANES), F32),
        ],
        compiler_params=pltpu.CompilerParams(
            dimension_semantics=("arbitrary", "arbitrary"), vmem_limit_bytes=VMEM_LIMIT),
        name="dsa",
    )(qa, ka, va, iq, ik, iw)


def _diff_kernel(q_ref, k_ref, v_ref, lam_ref, g_ref, o_ref, m_ref, l_ref, acc_ref,
                 *, tq, tk, lam_init):
    j = pl.program_id(2)
    q0 = j * tq
    lane = lax.broadcasted_iota(I32, (tq, LANES), 1)
    row = lax.broadcasted_iota(I32, (tq, 1), 0)
    q_end = (((q0 + row) >> 6) + 1) << 6

    q_st = _split_lane_halves(q_ref[...].astype(F32), lane)
    m_ref[...] = jnp.full(m_ref.shape, NEG, F32)
    l_ref[...] = jnp.zeros(l_ref.shape, F32)
    acc_ref[...] = jnp.zeros(acc_ref.shape, F32)

    def logits(kt):
        r0 = pl.multiple_of(kt * tk, tk)
        return lax.dot_general(q_st, k_ref[pl.ds(r0, tk), :], _NT, preferred_element_type=F32)

    def att_tile(kt, masked):
        s = logits(kt)
        chunks = []
        for c in range(tk // LANES):
            sc = s[:, c * LANES:(c + 1) * LANES]
            if masked:
                adm = (kt * tk + c * LANES + lane) < q_end
                sc = jnp.concatenate(
                    [jnp.where(adm, sc[:tq], NEG), jnp.where(adm, sc[tq:], NEG)], axis=0)
            chunks.append(sc)
        r0 = pl.multiple_of(kt * tk, tk)
        _online_softmax_step(chunks, v_ref[pl.ds(r0, tk), :], m_ref, l_ref, acc_ref)

    def full_tile(kt, carry):
        att_tile(kt, False)
        return carry

    n_full = j * (tq // tk)
    lax.fori_loop(0, n_full, full_tile, 0)
    for t in range(tq // tk):
        att_tile(n_full + t, True)

    lam_p = lam_ref[...]
    lam = (jnp.exp(jnp.sum(lam_p[0:1] * lam_p[1:2], axis=1, keepdims=True))
           - jnp.exp(jnp.sum(lam_p[2:3] * lam_p[3:4], axis=1, keepdims=True)) + lam_init)
    o = _softmax_finish(l_ref, acc_ref)
    o = o[:tq] - lam * o[tq:]
    o_ref[...] = (_rms(o, g_ref[...]) * (1.0 - lam_init)).astype(o_ref.dtype)


def _diff_call(qb, kb, vb, lam_p, g_subln, lam_init, tq, tk):
    b, s, _ = qb.shape
    return pl.pallas_call(
        functools.partial(_diff_kernel, tq=tq, tk=tk, lam_init=lam_init),
        grid=(b, B_HEADS, s // tq),
        in_specs=[
            pl.BlockSpec((None, tq, LANES), lambda bi, h, i: (bi, i, h)),
            pl.BlockSpec((None, s, LANES), lambda bi, h, i: (bi, 0, h)),
            pl.BlockSpec((None, s, LANES), lambda bi, h, i: (bi, 0, h)),
            pl.BlockSpec((4, B_HEAD_DIM), lambda bi, h, i: (0, 0)),
            pl.BlockSpec((1, LANES), lambda bi, h, i: (0, 0)),
        ],
        out_specs=pl.BlockSpec((None, tq, LANES), lambda bi, h, i: (bi, i, h)),
        out_shape=jax.ShapeDtypeStruct((b, s, D_B), BF16),
        scratch_shapes=[
            pltpu.VMEM((2 * tq, LANES), F32),
            pltpu.VMEM((2 * tq, LANES), F32),
            pltpu.VMEM((2 * tq, LANES), F32),
        ],
        compiler_params=pltpu.CompilerParams(
            dimension_semantics=("arbitrary", "arbitrary", "arbitrary"),
            vmem_limit_bytes=VMEM_LIMIT),
        name="diff",
    )(qb, kb, vb, lam_p, g_subln)


def _ffn_kernel(x_ref, a_ref, b_ref, wo_ref, g_ref, wg_ref, wu_ref, wd_ref, gf_ref, o_ref,
                *, ff_chunk, final_norm):
    d_a = a_ref.shape[1]
    x1 = (x_ref[...]
          + jnp.dot(a_ref[...], wo_ref[:d_a, :], preferred_element_type=F32)
          + jnp.dot(b_ref[...], wo_ref[d_a:, :], preferred_element_type=F32))
    h2 = _rms(x1, g_ref[...]).astype(BF16)
    y = None
    for c in range(wg_ref.shape[1] // ff_chunk):
        cs = slice(c * ff_chunk, (c + 1) * ff_chunk)
        gate = jnp.dot(h2, wg_ref[:, cs], preferred_element_type=F32)
        up = jnp.dot(h2, wu_ref[:, cs], preferred_element_type=F32)
        act = (gate / (1.0 + jnp.exp(-gate)) * up).astype(BF16)
        down = jnp.dot(act, wd_ref[cs, :], preferred_element_type=F32)
        y = down if y is None else y + down
    x2 = x1 + y
    if final_norm:
        x2 = _rms(x2, gf_ref[...])
    o_ref[...] = x2


def _ffn_call(x, out_a, out_b, wo, g, wg, wu, wd, g_final, final_norm, tm, ff_chunk):
    t, d = x.shape
    d_ff = wg.shape[1]
    row = lambda width: pl.BlockSpec((tm, width), lambda i: (i, 0))
    const = lambda shape: pl.BlockSpec(shape, lambda i: (0, 0), pipeline_mode=pl.Buffered(1))
    return pl.pallas_call(
        functools.partial(_ffn_kernel, ff_chunk=ff_chunk, final_norm=final_norm),
        grid=(t // tm,),
        in_specs=[row(d), row(out_a.shape[1]), row(out_b.shape[1]), const(wo.shape), const((1, d)),
                  const((d, d_ff)), const((d, d_ff)), const((d_ff, d)), const((1, d))],
        out_specs=row(d),
        out_shape=jax.ShapeDtypeStruct((t, d), F32),
        compiler_params=pltpu.CompilerParams(
            dimension_semantics=("arbitrary",), vmem_limit_bytes=VMEM_LIMIT),
        name="ffn",
    )(x, out_a, out_b, wo, g, wg, wu, wd, g_final)


def _rope_tables(seq_len, dim):
    pos = jnp.arange(seq_len, dtype=F32)
    inv = ROPE_THETA ** (-jnp.arange(0, dim, 2, dtype=F32) / dim)
    ang = pos[:, None] * inv[None, :]
    cos, sin = jnp.cos(ang), jnp.sin(ang)
    reps = LANES // dim
    return (jnp.tile(jnp.concatenate([cos, cos], axis=1), (1, reps)),
            jnp.tile(jnp.concatenate([-sin, sin], axis=1), (1, reps)))


def _pack_w_in(w):
    splits = (D_A, D_A, D_A, IDX_HEADS * IDX_DIM, IDX_DIM, IDX_HEADS, D_B, D_B, D_B)
    offs = [0]
    for sz in splits:
        offs.append(offs[-1] + sz)
    qa, ka, va, iq, ik, iw, qb, kb, vb = (w[:, offs[i]:offs[i + 1]] for i in range(len(splits)))
    ik4 = jnp.tile(ik, (1, LANES // IDX_DIM))
    iw_p = jnp.pad(iw, ((0, 0), (0, LANES - IDX_HEADS)))
    return jnp.concatenate([qa, ka, va, iq, ik4, iw_p, qb, kb, vb], axis=1).astype(BF16)


def kernel(x, w_in, w_out, g_mix, lam_q1, lam_k1, lam_q2, lam_k2, g_subln, g_ffn, w_gate, w_up,
           w_down, g_final):
    b, s, d = x.shape
    depth = w_in.shape[0]
    tm_proj, tq, tm_ffn = 512, 512, 512
    tq_diff = 2 * tq if s % (2 * tq) == 0 else tq
    d_ff = w_gate.shape[-1]
    ff_chunk = d_ff // 2
    assert s % tm_proj == 0 and s % tq == 0 and (b * s) % tm_ffn == 0 and ff_chunk % LANES == 0

    tabs = _rope_tables(s, A_HEAD_DIM) + _rope_tables(s, IDX_DIM)
    for layer in range(depth):
        qa, ka, va, iq, ik, iw, qb, kb, vb = _proj_call(
            x, g_mix[layer][None, :], _pack_w_in(w_in[layer]), tabs, tm_proj)
        out_a = _dsa_call(qa, ka, va, iq, ik, iw, tq)
        lam_init = 0.8 - 0.6 * math.exp(-0.3 * layer)
        lam_p = jnp.stack([lam_q1[layer], lam_k1[layer], lam_q2[layer], lam_k2[layer]]).astype(F32)
        out_b = _diff_call(qb, kb, vb, lam_p, g_subln[layer][None, :], lam_init, tq_diff, tq_diff)
        x = _ffn_call(
            x.reshape(b * s, d), out_a.reshape(b * s, D_A), out_b.reshape(b * s, D_B),
            w_out[layer].astype(BF16), g_ffn[layer][None, :], w_gate[layer].astype(BF16),
            w_up[layer].astype(BF16), w_down[layer].astype(BF16), g_final[None, :],
            layer == depth - 1, tm_ffn, ff_chunk).reshape(b, s, d)
    return x
```

```python
import functools
import math

import jax
import jax.numpy as jnp
import numpy as np
from jax import lax
from jax.experimental import pallas as pl
from jax.experimental.pallas import tpu as pltpu

F32 = jnp.float32
BF16 = jnp.bfloat16
I32 = jnp.int32

CHUNK = 64
ROPE_THETA = 10000.0
RMS_EPS = 1e-6
A_HEADS = 8
A_HEAD_DIM = 64
D_A = A_HEADS * A_HEAD_DIM
IDX_HEADS = 8
IDX_DIM = 32
TOPK_MAX = 256
B_HEADS = 4
B_HEAD_DIM = 64
D_B = B_HEADS * 2 * B_HEAD_DIM

LANES = 128
VMEM_LIMIT = 56 * 1024 * 1024

LOG2E = 1.4426950408889634
NEG = -1e30
NEG_BITS = int(np.float32(NEG).view(np.int32))
KEY_MIN = -(2 ** 31)

_OFF_QA, _OFF_KA, _OFF_VA = 0, 512, 1024
_OFF_IQ, _OFF_IK, _OFF_IW = 1536, 1792, 1920
_OFF_QB, _OFF_KB, _OFF_VB = 2048, 2560, 3072
_W_CAT = 3584

_NT = (((1,), (1,)), ((), ()))


def _rms(x, g):
    r = lax.rsqrt(jnp.mean(x * x, axis=-1, keepdims=True) + RMS_EPS)
    return x * r * g


def _proj_kernel(x_ref, g_ref, w_ref, cos_a_ref, sin_a_ref, cos_i_ref, sin_i_ref,
                 qa_ref, ka_ref, va_ref, iq_ref, ik_ref, iw_ref, qb_ref, kb_ref, vb_ref):
    h = _rms(x_ref[...], g_ref[...]).astype(BF16)
    tm = h.shape[0]
    lane = lax.broadcasted_iota(I32, (tm, LANES), 1)

    def proj(off, width):
        return jnp.dot(h, w_ref[:, off:off + width], preferred_element_type=F32)

    def rope_tile(y, cos, sin_signed, half):
        first = (lane & (2 * half - 1)) < half
        rot = jnp.where(first, pltpu.roll(y, LANES - half, 1), pltpu.roll(y, half, 1))
        return y * cos + rot * sin_signed

    def rope_store(out_ref, off, width, cos_ref, sin_ref, half, scale):
        y = proj(off, width)
        cos, sin = cos_ref[...], sin_ref[...]
        for t in range(width // LANES):
            r = rope_tile(y[:, t * LANES:(t + 1) * LANES], cos, sin, half)
            if scale != 1.0:
                r = r * scale
            out_ref[:, t * LANES:(t + 1) * LANES] = r.astype(out_ref.dtype)

    half_a, half_i = A_HEAD_DIM // 2, IDX_DIM // 2
    rope_store(qa_ref, _OFF_QA, D_A, cos_a_ref, sin_a_ref, half_a, A_HEAD_DIM ** -0.5 * LOG2E)
    rope_store(ka_ref, _OFF_KA, D_A, cos_a_ref, sin_a_ref, half_a, 1.0)
    va_ref[...] = proj(_OFF_VA, D_A).astype(va_ref.dtype)
    rope_store(iq_ref, _OFF_IQ, IDX_HEADS * IDX_DIM, cos_i_ref, sin_i_ref, half_i, 1.0)
    rope_store(ik_ref, _OFF_IK, LANES, cos_i_ref, sin_i_ref, half_i, 1.0)
    iw_ref[...] = proj(_OFF_IW, LANES) * (IDX_HEADS ** -0.5 * IDX_DIM ** -0.5)
    rope_store(qb_ref, _OFF_QB, D_B, cos_a_ref, sin_a_ref, half_a, B_HEAD_DIM ** -0.5 * LOG2E)
    rope_store(kb_ref, _OFF_KB, D_B, cos_a_ref, sin_a_ref, half_a, 1.0)
    vb_ref[...] = proj(_OFF_VB, D_B).astype(vb_ref.dtype)


def _proj_call(x, g, w_cat, tabs, tm):
    b, s, d = x.shape
    cos_a, sin_a, cos_i, sin_i = tabs
    row = lambda width: pl.BlockSpec((None, tm, width), lambda bi, i: (bi, i, 0))
    tab = pl.BlockSpec((tm, LANES), lambda bi, i: (i, 0))
    const = lambda shape: pl.BlockSpec(shape, lambda bi, i: (0, 0), pipeline_mode=pl.Buffered(1))
    widths = (D_A, D_A, D_A, IDX_HEADS * IDX_DIM, LANES, LANES, D_B, D_B, D_B)
    dtypes = (BF16, BF16, BF16, BF16, BF16, F32, BF16, BF16, BF16)
    return pl.pallas_call(
        _proj_kernel,
        grid=(b, s // tm),
        in_specs=[row(d), const((1, d)), const((d, _W_CAT)), tab, tab, tab, tab],
        out_specs=[row(w) for w in widths],
        out_shape=[jax.ShapeDtypeStruct((b, s, w), dt) for w, dt in zip(widths, dtypes)],
        compiler_params=pltpu.CompilerParams(
            dimension_semantics=("arbitrary", "arbitrary"), vmem_limit_bytes=VMEM_LIMIT),
        name="proj",
    )(x, g, w_cat, cos_a, sin_a, cos_i, sin_i)


def _online_softmax_step(chunks, v_t, m_ref, l_ref, acc_ref):
    m_prev = m_ref[...]
    m_cur = functools.reduce(jnp.maximum, chunks)
    m_new = jnp.maximum(m_prev, jnp.max(m_cur, axis=1, keepdims=True))
    alpha = jnp.exp2(m_prev - m_new)
    p = [jnp.exp2(c - m_new) for c in chunks]
    l_ref[...] = alpha * l_ref[...] + functools.reduce(jnp.add, p)
    pv = jnp.dot(jnp.concatenate(p, axis=1).astype(BF16), v_t, preferred_element_type=F32)
    acc_ref[...] = alpha * acc_ref[...] + pv
    m_ref[...] = m_new


def _softmax_finish(l_ref, acc_ref):
    return acc_ref[...] / jnp.sum(l_ref[...], axis=1, keepdims=True)


def _split_lane_halves(tile_f32, lane):
    lo = jnp.where(lane < LANES // 2, tile_f32, 0.0)
    hi = jnp.where(lane >= LANES // 2, tile_f32, 0.0)
    return jnp.concatenate([lo, hi], axis=0).astype(BF16)


def _dsa_kernel(q_ref, k_ref, v_ref, iq_ref, ik_ref, iw_ref, o_ref,
                key_ref, iqm_ref, wb_ref, mx2_ref, mx_ref, thr_ref, need_ref, cnt_ref,
                m_ref, l_ref, acc_ref,
                *, tq, topk, att_ratio):
    tk = tq
    nch = tk // LANES
    tki = 256
    rb = 128
    j = pl.program_id(1)
    q0 = j * tq
    n_tiles = j + 1
    lane = lax.broadcasted_iota(I32, (tq, LANES), 1)
    lane_rb = lax.broadcasted_iota(I32, (rb, LANES), 1)
    row = lax.broadcasted_iota(I32, (tq, 1), 0)
    q_end = (((q0 + row) >> 6) + 1) << 6

    nb = tq // rb
    sub8 = lax.broadcasted_iota(I32, (8, LANES), 0)
    lane8 = lax.broadcasted_iota(I32, (8, LANES), 1)
    n_adm = ((((q0 + sub8 * rb + lane8) >> 6) + 1) << 6).astype(F32)
    k_eff = jnp.minimum(n_adm, float(topk))
    ones8 = jnp.ones((8, LANES), BF16)
    tri_r = lax.broadcasted_iota(I32, (LANES, 2 * LANES), 0)
    tri_c = lax.broadcasted_iota(I32, (LANES, 2 * LANES), 1)
    tri_ones = jnp.where((tri_c >= LANES) | (tri_r <= tri_c), 1.0, 0.0).astype(BF16)

    def rows_of(x8, r):
        x = jnp.broadcast_to(x8[r:r + 1, :], (rb, LANES))
        if x8.dtype == I32:
            return lax.bitcast_convert_type(jnp.transpose(lax.bitcast_convert_type(x, F32)), I32)
        return jnp.transpose(x)

    def lanes_of(col):
        out = jnp.zeros((8, LANES), col.dtype)
        for r in range(nb):
            x = jnp.broadcast_to(col[r * rb:(r + 1) * rb], (rb, LANES))
            if col.dtype == I32:
                x = lax.bitcast_convert_type(jnp.transpose(lax.bitcast_convert_type(x, F32)), I32)
            else:
                x = jnp.transpose(x)
            out = jnp.where(sub8 == r, x[0:8, :], out)
        return out

    iw = iw_ref[...]
    for h in range(IDX_HEADS):
        g = h % 4
        tile = iq_ref[:, (h // 4) * LANES:(h // 4 + 1) * LANES].astype(F32)
        keep = (lane >= g * IDX_DIM) & (lane < (g + 1) * IDX_DIM)
        iq_h = jnp.where(keep, tile, 0.0).astype(BF16)
        w_h = jnp.broadcast_to(iw[:, h:h + 1], (tq, LANES))
        for r in range(nb):
            dst = slice((r * IDX_HEADS + h) * rb, (r * IDX_HEADS + h + 1) * rb)
            iqm_ref[dst, :] = iq_h[r * rb:(r + 1) * rb]
            wb_ref[dst, :] = w_h[r * rb:(r + 1) * rb]

    mx2_ref[...] = jnp.full((tq, LANES), KEY_MIN, I32)
    mx_ref[...] = jnp.full((tq, LANES), KEY_MIN, I32)

    def index_tile(it, masked):
        r0 = pl.multiple_of(it * tki, tki)
        ik_t = ik_ref[pl.ds(r0, tki), :]
        for r in range(nb):
            rs = slice(r * rb, (r + 1) * rb)
            base = r * IDX_HEADS * rb
            d = lax.dot_general(iqm_ref[base:base + IDX_HEADS * rb, :], ik_t, _NT,
                                preferred_element_type=F32)
            for c in range(tki // LANES):
                a = jnp.zeros((rb, LANES), F32)
                for h in range(IDX_HEADS):
                    hs = slice(h * rb, (h + 1) * rb)
                    w_h = wb_ref[base + h * rb:base + (h + 1) * rb, :]
                    a = a + w_h * jnp.maximum(d[hs, c * LANES:(c + 1) * LANES], 0.0)
                bits = lax.bitcast_convert_type(a, I32)
                key = bits ^ ((bits >> 31) & 0x7FFFFFFF)
                kidx = it * tki + c * LANES + lane_rb
                key = jnp.where(key == 0, -1 - kidx, key)
                if masked:
                    key = jnp.where(kidx < q_end[rs], key, KEY_MIN)
                top1 = mx_ref[rs, :]
                mx2_ref[rs, :] = jnp.maximum(mx2_ref[rs, :], jnp.minimum(top1, key))
                mx_ref[rs, :] = jnp.maximum(top1, key)
                key_ref[it * (tki // LANES) + c, rs, :] = key

    def full_tile(it, carry):
        index_tile(it, False)
        return carry

    n_full = q0 // tki
    lax.fori_loop(0, n_full, full_tile, 0)
    for t in range(tq // tki):
        index_tile(n_full + t, True)

    def count_pass(per_block):
        cnt_ref[...] = jnp.zeros((tq, LANES), F32)

        def body(kt, carry):
            for r in range(nb):
                rs = slice(r * rb, (r + 1) * rb)
                pred = per_block(rs)
                acc = cnt_ref[rs, :]
                for c in range(nch):
                    hit = pred(key_ref[kt * nch + c, rs, :], kt * tk + c * LANES)
                    acc = acc + jnp.where(hit, 1.0, 0.0)
                cnt_ref[rs, :] = acc
            return carry

        lax.fori_loop(0, n_tiles, body, 0)
        c8 = lax.dot_general(ones8, cnt_ref[...].astype(BF16), _NT, preferred_element_type=F32)
        cnt = jnp.zeros((8, LANES), F32)
        for r in range(nb):
            cnt = jnp.where(sub8 == r, c8[:, r * rb:(r + 1) * rb], cnt)
        return cnt

    def set_rows(ref, x8):
        for r in range(nb):
            ref[r * rb:(r + 1) * rb, :] = rows_of(x8, r)

    def count_ge(t8):
        set_rows(thr_ref, t8)

        def per_block(rs):
            tb = thr_ref[rs, :]
            return lambda keys, kidx0: keys >= tb

        return count_pass(per_block)

    def flip(k):
        return k ^ ((k >> 31) & 0x7FFFFFFF)

    live = sub8 < nb
    many = 2.0 ** 30
    small = n_adm <= float(topk)
    lo0 = lanes_of(jnp.min(mx2_ref[...], axis=1, keepdims=True))
    lo0 = jnp.where(live, jnp.where(small, KEY_MIN + 1, lo0), 0)
    hi0 = jnp.where(live, lanes_of(jnp.max(mx_ref[...], axis=1, keepdims=True)), 0)
    c_lo0 = jnp.where(small, n_adm, many)
    c_hi0 = jnp.zeros((8, LANES), F32)
    value_steps = 26
    zero_lo = -(key_ref.shape[0] * LANES)

    def bis_cond(st):
        it, lo, hi, c_lo, c_above = st
        open_rows = jnp.where((c_lo > k_eff) & (lo < hi), 1.0, 0.0)
        return jnp.max(open_rows) > 0.0

    def bis_body(st):
        it, lo, hi, c_lo, c_above = st
        mid_k = (lo >> 1) + (hi >> 1) + ((lo | hi) & 1)
        v_lo = lax.bitcast_convert_type(flip(lo), F32)
        v_hi = lax.bitcast_convert_type(flip(hi), F32)
        mid_v = flip(lax.bitcast_convert_type(0.5 * v_lo + 0.5 * v_hi, I32))
        probe_lo = (lo < zero_lo) & (hi >= zero_lo)
        probe_hi = (lo >= zero_lo) & (lo < 1) & (hi >= 1)
        in_zero = (lo >= zero_lo) & (hi < 1)
        mid_v = jnp.where(probe_lo, zero_lo, jnp.where(probe_hi, 1, jnp.where(in_zero, mid_k, mid_v)))
        mid_v = jnp.minimum(jnp.maximum(mid_v, lo + 1), hi)
        mid = jnp.where(it < value_steps, mid_v, mid_k)
        active = (c_lo > k_eff) & (lo < hi)
        c_mid = count_ge(mid)
        go_up = active & (c_mid >= k_eff)
        go_dn = active & (c_mid < k_eff)
        lo = jnp.where(go_up, mid, lo)
        c_lo = jnp.where(go_up, c_mid, c_lo)
        hi = jnp.where(go_dn, mid - 1, hi)
        c_above = jnp.where(go_dn, c_mid, c_above)
        return it + 1, lo, hi, c_lo, c_above

    _, thr, _, c_thr, c_above = lax.while_loop(
        bis_cond, bis_body, (jnp.int32(0), lo0, hi0, c_lo0, c_hi0))
    set_rows(thr_ref, thr)
    need = k_eff - c_above
    has_tie = live & (c_thr > k_eff)
    any_tie = jnp.max(jnp.where(has_tie, 1.0, 0.0)) > 0.0

    def bias_pass(with_ties):
        def bias_tile(kt, carry):
            for c in range(nch):
                keys = key_ref[kt * nch + c]
                thr_b = thr_ref[...]
                tie = keys == thr_b
                if with_ties:
                    hits = jnp.dot(jnp.where(tie, 1.0, 0.0).astype(BF16), tri_ones,
                                   preferred_element_type=F32)
                    seen = cnt_ref[...]
                    tie = tie & (hits[:, :LANES] + seen <= need_ref[...])
                    cnt_ref[...] = seen + hits[:, LANES:]
                key_ref[kt * nch + c] = jnp.where((keys > thr_b) | tie, 0, NEG_BITS)
            return carry

        lax.fori_loop(0, n_tiles, bias_tile, 0)

    @pl.when(any_tie)
    def _():
        set_rows(need_ref, jnp.where(has_tie, need, 2.0 ** 30))
        cnt_ref[...] = jnp.zeros((tq, LANES), F32)
        bias_pass(True)

    @pl.when(jnp.logical_not(any_tie))
    def _():
        bias_pass(False)

    n_wide = n_tiles >> (att_ratio.bit_length() - 1)

    for hp in range(A_HEADS // 2):
        ls = slice(hp * LANES, (hp + 1) * LANES)
        q_st = _split_lane_halves(q_ref[:, ls].astype(F32), lane)
        m_ref[...] = jnp.full(m_ref.shape, NEG, F32)
        l_ref[...] = jnp.zeros(l_ref.shape, F32)
        acc_ref[...] = jnp.zeros(acc_ref.shape, F32)

        def att_step(kt, width):
            r0 = pl.multiple_of(kt * (width * tk), width * tk)
            s = lax.dot_general(q_st, k_ref[pl.ds(r0, width * tk), ls], _NT,
                                preferred_element_type=F32)
            chunks = []
            for c in range(width * nch):
                bias = lax.bitcast_convert_type(key_ref[kt * (width * nch) + c], F32)
                sc = s[:, c * LANES:(c + 1) * LANES]
                chunks.append(jnp.concatenate([sc[:tq] + bias, sc[tq:] + bias], axis=0))
            _online_softmax_step(chunks, v_ref[pl.ds(r0, width * tk), ls], m_ref, l_ref, acc_ref)

        def wide_step(kt, carry):
            att_step(kt, att_ratio)
            return carry

        def single_step(kt, carry):
            att_step(kt, 1)
            return carry

        lax.fori_loop(0, n_wide, wide_step, 0)
        lax.fori_loop(n_wide * att_ratio, n_tiles, single_step, 0)
        o = _softmax_finish(l_ref, acc_ref)
        o_ref[:, ls] = jnp.where(lane < LANES // 2, o[:tq], o[tq:]).astype(o_ref.dtype)


def _dsa_call(qa, ka, va, iq, ik, iw, tq):
    b, s, _ = qa.shape
    topk = min(TOPK_MAX, s // 4)
    qrow = lambda width: pl.BlockSpec((None, tq, width), lambda bi, i: (bi, i, 0))
    seq = lambda width: pl.BlockSpec((None, s, width), lambda bi, i: (bi, 0, 0),
                                     pipeline_mode=pl.Buffered(1))
    att_ratio = 2 if s % (2 * tq) == 0 else 1
    return pl.pallas_call(
        functools.partial(_dsa_kernel, tq=tq, topk=topk, att_ratio=att_ratio),
        grid=(b, s // tq),
        in_specs=[qrow(D_A), seq(D_A), seq(D_A), qrow(IDX_HEADS * IDX_DIM), seq(LANES), qrow(LANES)],
        out_specs=qrow(D_A),
        out_shape=jax.ShapeDtypeStruct((b, s, D_A), BF16),
        scratch_shapes=[
            pltpu.VMEM((s // LANES, tq, LANES), I32),
            pltpu.VMEM((IDX_HEADS * tq, LANES), BF16),
            pltpu.VMEM((IDX_HEADS * tq, LANES), F32),
            pltpu.VMEM((tq, LANES), I32),
            pltpu.VMEM((tq, LANES), I32),
            pltpu.VMEM((tq, LANES), I32),
            pltpu.VMEM((tq, LANES), F32),
            pltpu.VMEM((tq, LANES), F32),
            pltpu.VMEM((2 * tq, LANES), F32),
            pltpu.VMEM((2 * tq, LANES), F32),
            pltpu.VMEM((2 * tq, LANES), F32),
        ],
        compiler_params=pltpu.CompilerParams(
            dimension_semantics=("arbitrary", "arbitrary"), vmem_limit_bytes=VMEM_LIMIT),
        name="dsa",
    )(qa, ka, va, iq, ik, iw)


def _diff_kernel(q_ref, k_ref, v_ref, lam_ref, g_ref, o_ref, m_ref, l_ref, acc_ref,
                 *, tq, tk, lam_init):
    j = pl.program_id(2)
    q0 = j * tq
    lane = lax.broadcasted_iota(I32, (tq, LANES), 1)
    row = lax.broadcasted_iota(I32, (tq, 1), 0)
    q_end = (((q0 + row) >> 6) + 1) << 6

    q_st = _split_lane_halves(q_ref[...].astype(F32), lane)
    m_ref[...] = jnp.full(m_ref.shape, NEG, F32)
    l_ref[...] = jnp.zeros(l_ref.shape, F32)
    acc_ref[...] = jnp.zeros(acc_ref.shape, F32)

    def logits(kt):
        r0 = pl.multiple_of(kt * tk, tk)
        return lax.dot_general(q_st, k_ref[pl.ds(r0, tk), :], _NT, preferred_element_type=F32)

    def att_tile(kt, masked):
        s = logits(kt)
        chunks = []
        for c in range(tk // LANES):
            sc = s[:, c * LANES:(c + 1) * LANES]
            if masked:
                adm = (kt * tk + c * LANES + lane) < q_end
                sc = jnp.concatenate(
                    [jnp.where(adm, sc[:tq], NEG), jnp.where(adm, sc[tq:], NEG)], axis=0)
            chunks.append(sc)
        r0 = pl.multiple_of(kt * tk, tk)
        _online_softmax_step(chunks, v_ref[pl.ds(r0, tk), :], m_ref, l_ref, acc_ref)

    def full_tile(kt, carry):
        att_tile(kt, False)
        return carry

    n_full = j * (tq // tk)
    lax.fori_loop(0, n_full, full_tile, 0)
    for t in range(tq // tk):
        att_tile(n_full + t, True)

    lam_p = lam_ref[...]
    lam = (jnp.exp(jnp.sum(lam_p[0:1] * lam_p[1:2], axis=1, keepdims=True))
           - jnp.exp(jnp.sum(lam_p[2:3] * lam_p[3:4], axis=1, keepdims=True)) + lam_init)
    o = _softmax_finish(l_ref, acc_ref)
    o = o[:tq] - lam * o[tq:]
    o_ref[...] = (_rms(o, g_ref[...]) * (1.0 - lam_init)).astype(o_ref.dtype)


def _diff_call(qb, kb, vb, lam_p, g_subln, lam_init, tq, tk):
    b, s, _ = qb.shape
    return pl.pallas_call(
        functools.partial(_diff_kernel, tq=tq, tk=tk, lam_init=lam_init),
        grid=(b, B_HEADS, s // tq),
        in_specs=[
            pl.BlockSpec((None, tq, LANES), lambda bi, h, i: (bi, i, h)),
            pl.BlockSpec((None, s, LANES), lambda bi, h, i: (bi, 0, h)),
            pl.BlockSpec((None, s, LANES), lambda bi, h, i: (bi, 0, h)),
            pl.BlockSpec((4, B_HEAD_DIM), lambda bi, h, i: (0, 0)),
            pl.BlockSpec((1, LANES), lambda bi, h, i: (0, 0)),
        ],
        out_specs=pl.BlockSpec((None, tq, LANES), lambda bi, h, i: (bi, i, h)),
        out_shape=jax.ShapeDtypeStruct((b, s, D_B), BF16),
        scratch_shapes=[
            pltpu.VMEM((2 * tq, LANES), F32),
            pltpu.VMEM((2 * tq, LANES), F32),
            pltpu.VMEM((2 * tq, LANES), F32),
        ],
        compiler_params=pltpu.CompilerParams(
            dimension_semantics=("arbitrary", "arbitrary", "arbitrary"),
            vmem_limit_bytes=VMEM_LIMIT),
        name="diff",
    )(qb, kb, vb, lam_p, g_subln)


def _ffn_kernel(x_ref, a_ref, b_ref, wo_ref, g_ref, wg_ref, wu_ref, wd_ref, gf_ref, o_ref,
                *, ff_chunk, final_norm):
    d_a = a_ref.shape[1]
    x1 = (x_ref[...]
          + jnp.dot(a_ref[...], wo_ref[:d_a, :], preferred_element_type=F32)
          + jnp.dot(b_ref[...], wo_ref[d_a:, :], preferred_element_type=F32))
    h2 = _rms(x1, g_ref[...]).astype(BF16)
    y = None
    for c in range(wg_ref.shape[1] // ff_chunk):
        cs = slice(c * ff_chunk, (c + 1) * ff_chunk)
        gate = jnp.dot(h2, wg_ref[:, cs], preferred_element_type=F32)
        up = jnp.dot(h2, wu_ref[:, cs], preferred_element_type=F32)
        act = (gate / (1.0 + jnp.exp(-gate)) * up).astype(BF16)
        down = jnp.dot(act, wd_ref[cs, :], preferred_element_type=F32)
        y = down if y is None else y + down
    x2 = x1 + y
    if final_norm:
        x2 = _rms(x2, gf_ref[...])
    o_ref[...] = x2


def _ffn_call(x, out_a, out_b, wo, g, wg, wu, wd, g_final, final_norm, tm, ff_chunk):
    t, d = x.shape
    d_ff = wg.shape[1]
    row = lambda width: pl.BlockSpec((tm, width), lambda i: (i, 0))
    const = lambda shape: pl.BlockSpec(shape, lambda i: (0, 0), pipeline_mode=pl.Buffered(1))
    return pl.pallas_call(
        functools.partial(_ffn_kernel, ff_chunk=ff_chunk, final_norm=final_norm),
        grid=(t // tm,),
        in_specs=[row(d), row(out_a.shape[1]), row(out_b.shape[1]), const(wo.shape), const((1, d)),
                  const((d, d_ff)), const((d, d_ff)), const((d_ff, d)), const((1, d))],
        out_specs=row(d),
        out_shape=jax.ShapeDtypeStruct((t, d), F32),
        compiler_params=pltpu.CompilerParams(
            dimension_semantics=("arbitrary",), vmem_limit_bytes=VMEM_LIMIT),
        name="ffn",
    )(x, out_a, out_b, wo, g, wg, wu, wd, g_final)


def _rope_tables(seq_len, dim):
    pos = jnp.arange(seq_len, dtype=F32)
    inv = ROPE_THETA ** (-jnp.arange(0, dim, 2, dtype=F32) / dim)
    ang = pos[:, None] * inv[None, :]
    cos, sin = jnp.cos(ang), jnp.sin(ang)
    reps = LANES // dim
    return (jnp.tile(jnp.concatenate([cos, cos], axis=1), (1, reps)),
            jnp.tile(jnp.concatenate([-sin, sin], axis=1), (1, reps)))


def _pack_w_in(w):
    splits = (D_A, D_A, D_A, IDX_HEADS * IDX_DIM, IDX_DIM, IDX_HEADS, D_B, D_B, D_B)
    offs = [0]
    for sz in splits:
        offs.append(offs[-1] + sz)
    qa, ka, va, iq, ik, iw, qb, kb, vb = (w[:, offs[i]:offs[i + 1]] for i in range(len(splits)))
    ik4 = jnp.tile(ik, (1, LANES // IDX_DIM))
    iw_p = jnp.pad(iw, ((0, 0), (0, LANES - IDX_HEADS)))
    return jnp.concatenate([qa, ka, va, iq, ik4, iw_p, qb, kb, vb], axis=1).astype(BF16)


def kernel(x, w_in, w_out, g_mix, lam_q1, lam_k1, lam_q2, lam_k2, g_subln, g_ffn, w_gate, w_up,
           w_down, g_final):
    b, s, d = x.shape
    depth = w_in.shape[0]
    tm_proj, tq, tm_ffn = 512, 512, 512
    tq_diff = 2 * tq if s % (2 * tq) == 0 else tq
    d_ff = w_gate.shape[-1]
    ff_chunk = d_ff // 2
    assert s % tm_proj == 0 and s % tq == 0 and (b * s) % tm_ffn == 0 and ff_chunk % LANES == 0

    tabs = _rope_tables(s, A_HEAD_DIM) + _rope_tables(s, IDX_DIM)
    for layer in range(depth):
        qa, ka, va, iq, ik, iw, qb, kb, vb = _proj_call(
            x, g_mix[layer][None, :], _pack_w_in(w_in[layer]), tabs, tm_proj)
        out_a = _dsa_call(qa, ka, va, iq, ik, iw, tq)
        lam_init = 0.8 - 0.6 * math.exp(-0.3 * layer)
        lam_p = jnp.stack([lam_q1[layer], lam_k1[layer], lam_q2[layer], lam_k2[layer]]).astype(F32)
        out_b = _diff_call(qb, kb, vb, lam_p, g_subln[layer][None, :], lam_init, tq_diff, tq_diff)
        x = _ffn_call(
            x.reshape(b * s, d), out_a.reshape(b * s, D_A), out_b.reshape(b * s, D_B),
            w_out[layer].astype(BF16), g_ffn[layer][None, :], w_gate[layer].astype(BF16),
            w_up[layer].astype(BF16), w_down[layer].astype(BF16), g_final[None, :],
            layer == depth - 1, tm_ffn, ff_chunk).reshape(b, s, d)
    return x
```

```python
import functools
import math

import jax
import jax.numpy as jnp
import numpy as np
from jax import lax
from jax.experimental import pallas as pl
from jax.experimental.pallas import tpu as pltpu

F32 = jnp.float32
BF16 = jnp.bfloat16
I32 = jnp.int32

CHUNK = 64
ROPE_THETA = 10000.0
RMS_EPS = 1e-6
A_HEADS = 8
A_HEAD_DIM = 64
D_A = A_HEADS * A_HEAD_DIM
IDX_HEADS = 8
IDX_DIM = 32
TOPK_MAX = 256
B_HEADS = 4
B_HEAD_DIM = 64
D_B = B_HEADS * 2 * B_HEAD_DIM

LANES = 128
VMEM_LIMIT = 56 * 1024 * 1024

LOG2E = 1.4426950408889634
NEG = -1e30
NEG_BITS = int(np.float32(NEG).view(np.int32))
KEY_MIN = -(2 ** 31)

_OFF_QA, _OFF_KA, _OFF_VA = 0, 512, 1024
_OFF_IQ, _OFF_IK, _OFF_IW = 1536, 1792, 1920
_OFF_QB, _OFF_KB, _OFF_VB = 2048, 2560, 3072
_W_CAT = 3584

_NT = (((1,), (1,)), ((), ()))


def _rms(x, g):
    r = lax.rsqrt(jnp.mean(x * x, axis=-1, keepdims=True) + RMS_EPS)
    return x * r * g


def _proj_kernel(x_ref, g_ref, w_ref, cos_a_ref, sin_a_ref, cos_i_ref, sin_i_ref,
                 qa_ref, ka_ref, va_ref, iq_ref, ik_ref, iw_ref, qb_ref, kb_ref, vb_ref):
    h = _rms(x_ref[...], g_ref[...]).astype(BF16)
    tm = h.shape[0]
    lane = lax.broadcasted_iota(I32, (tm, LANES), 1)

    def proj(off, width):
        return jnp.dot(h, w_ref[:, off:off + width], preferred_element_type=F32)

    def rope_tile(y, cos, sin_signed, half):
        first = (lane & (2 * half - 1)) < half
        rot = jnp.where(first, pltpu.roll(y, LANES - half, 1), pltpu.roll(y, half, 1))
        return y * cos + rot * sin_signed

    def rope_store(out_ref, off, width, cos_ref, sin_ref, half, scale):
        y = proj(off, width)
        cos, sin = cos_ref[...], sin_ref[...]
        for t in range(width // LANES):
            r = rope_tile(y[:, t * LANES:(t + 1) * LANES], cos, sin, half)
            if scale != 1.0:
                r = r * scale
            out_ref[:, t * LANES:(t + 1) * LANES] = r.astype(out_ref.dtype)

    half_a, half_i = A_HEAD_DIM // 2, IDX_DIM // 2
    rope_store(qa_ref, _OFF_QA, D_A, cos_a_ref, sin_a_ref, half_a, A_HEAD_DIM ** -0.5 * LOG2E)
    rope_store(ka_ref, _OFF_KA, D_A, cos_a_ref, sin_a_ref, half_a, 1.0)
    va_ref[...] = proj(_OFF_VA, D_A).astype(va_ref.dtype)
    rope_store(iq_ref, _OFF_IQ, IDX_HEADS * IDX_DIM, cos_i_ref, sin_i_ref, half_i, 1.0)
    rope_store(ik_ref, _OFF_IK, LANES, cos_i_ref, sin_i_ref, half_i, 1.0)
    iw_ref[...] = proj(_OFF_IW, LANES) * (IDX_HEADS ** -0.5 * IDX_DIM ** -0.5)
    rope_store(qb_ref, _OFF_QB, D_B, cos_a_ref, sin_a_ref, half_a, B_HEAD_DIM ** -0.5 * LOG2E)
    rope_store(kb_ref, _OFF_KB, D_B, cos_a_ref, sin_a_ref, half_a, 1.0)
    vb_ref[...] = proj(_OFF_VB, D_B).astype(vb_ref.dtype)


def _proj_call(x, g, w_cat, tabs, tm):
    b, s, d = x.shape
    cos_a, sin_a, cos_i, sin_i = tabs
    row = lambda width: pl.BlockSpec((None, tm, width), lambda bi, i: (bi, i, 0))
    tab = pl.BlockSpec((tm, LANES), lambda bi, i: (i, 0))
    const = lambda shape: pl.BlockSpec(shape, lambda bi, i: (0, 0), pipeline_mode=pl.Buffered(1))
    widths = (D_A, D_A, D_A, IDX_HEADS * IDX_DIM, LANES, LANES, D_B, D_B, D_B)
    dtypes = (BF16, BF16, BF16, BF16, BF16, F32, BF16, BF16, BF16)
    return pl.pallas_call(
        _proj_kernel,
        grid=(b, s // tm),
        in_specs=[row(d), const((1, d)), const((d, _W_CAT)), tab, tab, tab, tab],
        out_specs=[row(w) for w in widths],
        out_shape=[jax.ShapeDtypeStruct((b, s, w), dt) for w, dt in zip(widths, dtypes)],
        compiler_params=pltpu.CompilerParams(
            dimension_semantics=("arbitrary", "arbitrary"), vmem_limit_bytes=VMEM_LIMIT),
        name="proj",
    )(x, g, w_cat, cos_a, sin_a, cos_i, sin_i)


def _online_softmax_step(chunks, v_t, m_ref, l_ref, acc_ref):
    m_prev = m_ref[...]
    m_cur = functools.reduce(jnp.maximum, chunks)
    m_new = jnp.maximum(m_prev, jnp.max(m_cur, axis=1, keepdims=True))
    alpha = jnp.exp2(m_prev - m_new)
    p = [jnp.exp2(c - m_new) for c in chunks]
    l_ref[...] = alpha * l_ref[...] + functools.reduce(jnp.add, p)
    pv = jnp.dot(jnp.concatenate(p, axis=1).astype(BF16), v_t, preferred_element_type=F32)
    acc_ref[...] = alpha * acc_ref[...] + pv
    m_ref[...] = m_new


def _softmax_finish(l_ref, acc_ref):
    return acc_ref[...] / jnp.sum(l_ref[...], axis=1, keepdims=True)


def _split_lane_halves(tile_f32, lane):
    lo = jnp.where(lane < LANES // 2, tile_f32, 0.0)
    hi = jnp.where(lane >= LANES // 2, tile_f32, 0.0)
    return jnp.concatenate([lo, hi], axis=0).astype(BF16)


def _dsa_kernel(q_ref, k_ref, v_ref, iq_ref, ik_ref, iw_ref, o_ref,
                key_ref, iqm_ref, wb_ref, mx2_ref, mx_ref, thr_ref, need_ref, cnt_ref,
                m_ref, l_ref, acc_ref,
                *, tq, topk, att_ratio):
    tk = tq
    nch = tk // LANES
    tki = 512
    rb = 128
    j = pl.program_id(1)
    q0 = j * tq
    n_tiles = j + 1
    lane = lax.broadcasted_iota(I32, (tq, LANES), 1)
    lane_rb = lax.broadcasted_iota(I32, (rb, LANES), 1)
    row = lax.broadcasted_iota(I32, (tq, 1), 0)
    q_end = (((q0 + row) >> 6) + 1) << 6

    nb = tq // rb
    sub8 = lax.broadcasted_iota(I32, (8, LANES), 0)
    lane8 = lax.broadcasted_iota(I32, (8, LANES), 1)
    n_adm = ((((q0 + sub8 * rb + lane8) >> 6) + 1) << 6).astype(F32)
    k_eff = jnp.minimum(n_adm, float(topk))
    ones8 = jnp.ones((8, LANES), BF16)
    tri_r = lax.broadcasted_iota(I32, (LANES, 2 * LANES), 0)
    tri_c = lax.broadcasted_iota(I32, (LANES, 2 * LANES), 1)
    tri_ones = jnp.where((tri_c >= LANES) | (tri_r <= tri_c), 1.0, 0.0).astype(BF16)

    def rows_of(x8, r):
        x = jnp.broadcast_to(x8[r:r + 1, :], (rb, LANES))
        if x8.dtype == I32:
            return lax.bitcast_convert_type(jnp.transpose(lax.bitcast_convert_type(x, F32)), I32)
        return jnp.transpose(x)

    def lanes_of(col):
        out = jnp.zeros((8, LANES), col.dtype)
        for r in range(nb):
            x = jnp.broadcast_to(col[r * rb:(r + 1) * rb], (rb, LANES))
            if col.dtype == I32:
                x = lax.bitcast_convert_type(jnp.transpose(lax.bitcast_convert_type(x, F32)), I32)
            else:
                x = jnp.transpose(x)
            out = jnp.where(sub8 == r, x[0:8, :], out)
        return out

    iw = iw_ref[...]
    for h in range(IDX_HEADS):
        g = h % 4
        tile = iq_ref[:, (h // 4) * LANES:(h // 4 + 1) * LANES].astype(F32)
        keep = (lane >= g * IDX_DIM) & (lane < (g + 1) * IDX_DIM)
        iq_h = jnp.where(keep, tile, 0.0).astype(BF16)
        w_h = jnp.broadcast_to(iw[:, h:h + 1], (tq, LANES))
        for r in range(nb):
            dst = slice((r * IDX_HEADS + h) * rb, (r * IDX_HEADS + h + 1) * rb)
            iqm_ref[dst, :] = iq_h[r * rb:(r + 1) * rb]
            wb_ref[dst, :] = w_h[r * rb:(r + 1) * rb]

    mx2_ref[...] = jnp.full((tq, LANES), KEY_MIN, I32)
    mx_ref[...] = jnp.full((tq, LANES), KEY_MIN, I32)

    def index_tile(it, masked):
        r0 = pl.multiple_of(it * tki, tki)
        ik_t = ik_ref[pl.ds(r0, tki), :]
        for r in range(nb):
            rs = slice(r * rb, (r + 1) * rb)
            base = r * IDX_HEADS * rb
            d = lax.dot_general(iqm_ref[base:base + IDX_HEADS * rb, :], ik_t, _NT,
                                preferred_element_type=F32)
            for c in range(tki // LANES):
                a = jnp.zeros((rb, LANES), F32)
                for h in range(IDX_HEADS):
                    hs = slice(h * rb, (h + 1) * rb)
                    w_h = wb_ref[base + h * rb:base + (h + 1) * rb, :]
                    a = a + w_h * jnp.maximum(d[hs, c * LANES:(c + 1) * LANES], 0.0)
                bits = lax.bitcast_convert_type(a, I32)
                key = bits ^ ((bits >> 31) & 0x7FFFFFFF)
                kidx = it * tki + c * LANES + lane_rb
                key = jnp.where(key == 0, -1 - kidx, key)
                if masked:
                    key = jnp.where(kidx < q_end[rs], key, KEY_MIN)
                top1 = mx_ref[rs, :]
                mx2_ref[rs, :] = jnp.maximum(mx2_ref[rs, :], jnp.minimum(top1, key))
                mx_ref[rs, :] = jnp.maximum(top1, key)
                key_ref[it * (tki // LANES) + c, rs, :] = key

    def full_tile(it, carry):
        index_tile(it, False)
        return carry

    n_full = q0 // tki
    lax.fori_loop(0, n_full, full_tile, 0)
    for t in range(tq // tki):
        index_tile(n_full + t, True)

    def count_pass(per_block):
        cnt_ref[...] = jnp.zeros((tq, LANES), F32)

        def body(kt, carry):
            for r in range(nb):
                rs = slice(r * rb, (r + 1) * rb)
                pred = per_block(rs)
                acc = cnt_ref[rs, :]
                for c in range(nch):
                    hit = pred(key_ref[kt * nch + c, rs, :], kt * tk + c * LANES)
                    acc = acc + jnp.where(hit, 1.0, 0.0)
                cnt_ref[rs, :] = acc
            return carry

        lax.fori_loop(0, n_tiles, body, 0)
        c8 = lax.dot_general(ones8, cnt_ref[...].astype(BF16), _NT, preferred_element_type=F32)
        cnt = jnp.zeros((8, LANES), F32)
        for r in range(nb):
            cnt = jnp.where(sub8 == r, c8[:, r * rb:(r + 1) * rb], cnt)
        return cnt

    def set_rows(ref, x8):
        for r in range(nb):
            ref[r * rb:(r + 1) * rb, :] = rows_of(x8, r)

    def count_ge(t8):
        set_rows(thr_ref, t8)

        def per_block(rs):
            tb = thr_ref[rs, :]
            return lambda keys, kidx0: keys >= tb

        return count_pass(per_block)

    def flip(k):
        return k ^ ((k >> 31) & 0x7FFFFFFF)

    live = sub8 < nb
    many = 2.0 ** 30
    small = n_adm <= float(topk)
    lo0 = lanes_of(jnp.min(mx2_ref[...], axis=1, keepdims=True))
    lo0 = jnp.where(live, jnp.where(small, KEY_MIN + 1, lo0), 0)
    hi0 = jnp.where(live, lanes_of(jnp.max(mx_ref[...], axis=1, keepdims=True)), 0)
    c_lo0 = jnp.where(small, n_adm, many)
    c_hi0 = jnp.zeros((8, LANES), F32)
    value_steps = 26
    zero_lo = -(key_ref.shape[0] * LANES)

    def bis_cond(st):
        it, lo, hi, c_lo, c_above = st
        open_rows = jnp.where((c_lo > k_eff) & (lo < hi), 1.0, 0.0)
        return jnp.max(open_rows) > 0.0

    def bis_body(st):
        it, lo, hi, c_lo, c_above = st
        mid_k = (lo >> 1) + (hi >> 1) + ((lo | hi) & 1)
        v_lo = lax.bitcast_convert_type(flip(lo), F32)
        v_hi = lax.bitcast_convert_type(flip(hi), F32)
        mid_v = flip(lax.bitcast_convert_type(0.5 * v_lo + 0.5 * v_hi, I32))
        probe_lo = (lo < zero_lo) & (hi >= zero_lo)
        probe_hi = (lo >= zero_lo) & (lo < 1) & (hi >= 1)
        in_zero = (lo >= zero_lo) & (hi < 1)
        mid_v = jnp.where(probe_lo, zero_lo, jnp.where(probe_hi, 1, jnp.where(in_zero, mid_k, mid_v)))
        mid_v = jnp.minimum(jnp.maximum(mid_v, lo + 1), hi)
        mid = jnp.where(it < value_steps, mid_v, mid_k)
        active = (c_lo > k_eff) & (lo < hi)
        c_mid = count_ge(mid)
        go_up = active & (c_mid >= k_eff)
        go_dn = active & (c_mid < k_eff)
        lo = jnp.where(go_up, mid, lo)
        c_lo = jnp.where(go_up, c_mid, c_lo)
        hi = jnp.where(go_dn, mid - 1, hi)
        c_above = jnp.where(go_dn, c_mid, c_above)
        return it + 1, lo, hi, c_lo, c_above

    _, thr, _, c_thr, c_above = lax.while_loop(
        bis_cond, bis_body, (jnp.int32(0), lo0, hi0, c_lo0, c_hi0))
    set_rows(thr_ref, thr)
    need = k_eff - c_above
    has_tie = live & (c_thr > k_eff)
    any_tie = jnp.max(jnp.where(has_tie, 1.0, 0.0)) > 0.0

    def bias_pass(with_ties):
        def bias_tile(kt, carry):
            for c in range(nch):
                keys = key_ref[kt * nch + c]
                thr_b = thr_ref[...]
                tie = keys == thr_b
                if with_ties:
                    hits = jnp.dot(jnp.where(tie, 1.0, 0.0).astype(BF16), tri_ones,
                                   preferred_element_type=F32)
                    seen = cnt_ref[...]
                    tie = tie & (hits[:, :LANES] + seen <= need_ref[...])
                    cnt_ref[...] = seen + hits[:, LANES:]
                key_ref[kt * nch + c] = jnp.where((keys > thr_b) | tie, 0, NEG_BITS)
            return carry

        lax.fori_loop(0, n_tiles, bias_tile, 0)

    @pl.when(any_tie)
    def _():
        set_rows(need_ref, jnp.where(has_tie, need, 2.0 ** 30))
        cnt_ref[...] = jnp.zeros((tq, LANES), F32)
        bias_pass(True)

    @pl.when(jnp.logical_not(any_tie))
    def _():
        bias_pass(False)

    n_wide = n_tiles >> (att_ratio.bit_length() - 1)

    for hp in range(A_HEADS // 2):
        ls = slice(hp * LANES, (hp + 1) * LANES)
        q_st = _split_lane_halves(q_ref[:, ls].astype(F32), lane)
        m_ref[...] = jnp.full(m_ref.shape, NEG, F32)
        l_ref[...] = jnp.zeros(l_ref.shape, F32)
        acc_ref[...] = jnp.zeros(acc_ref.shape, F32)

        def att_step(kt, width):
            r0 = pl.multiple_of(kt * (width * tk), width * tk)
            s = lax.dot_general(q_st, k_ref[pl.ds(r0, width * tk), ls], _NT,
                                preferred_element_type=F32)
            chunks = []
            for c in range(width * nch):
                bias = lax.bitcast_convert_type(key_ref[kt * (width * nch) + c], F32)
                sc = s[:, c * LANES:(c + 1) * LANES]
                chunks.append(jnp.concatenate([sc[:tq] + bias, sc[tq:] + bias], axis=0))
            _online_softmax_step(chunks, v_ref[pl.ds(r0, width * tk), ls], m_ref, l_ref, acc_ref)

        def wide_step(kt, carry):
            att_step(kt, att_ratio)
            return carry

        def single_step(kt, carry):
            att_step(kt, 1)
            return carry

        lax.fori_loop(0, n_wide, wide_step, 0)
        lax.fori_loop(n_wide * att_ratio, n_tiles, single_step, 0)
        o = _softmax_finish(l_ref, acc_ref)
        o_ref[:, ls] = jnp.where(lane < LANES // 2, o[:tq], o[tq:]).astype(o_ref.dtype)


def _dsa_call(qa, ka, va, iq, ik, iw, tq):
    b, s, _ = qa.shape
    topk = min(TOPK_MAX, s // 4)
    qrow = lambda width: pl.BlockSpec((None, tq, width), lambda bi, i: (bi, i, 0))
    seq = lambda width: pl.BlockSpec((None, s, width), lambda bi, i: (bi, 0, 0),
                                     pipeline_mode=pl.Buffered(1))
    att_ratio = 2 if s % (2 * tq) == 0 else 1
    return pl.pallas_call(
        functools.partial(_dsa_kernel, tq=tq, topk=topk, att_ratio=att_ratio),
        grid=(b, s // tq),
        in_specs=[qrow(D_A), seq(D_A), seq(D_A), qrow(IDX_HEADS * IDX_DIM), seq(LANES), qrow(LANES)],
        out_specs=qrow(D_A),
        out_shape=jax.ShapeDtypeStruct((b, s, D_A), BF16),
        scratch_shapes=[
            pltpu.VMEM((s // LANES, tq, LANES), I32),
            pltpu.VMEM((IDX_HEADS * tq, LANES), BF16),
            pltpu.VMEM((IDX_HEADS * tq, LANES), F32),
            pltpu.VMEM((tq, LANES), I32),
            pltpu.VMEM((tq, LANES), I32),
            pltpu.VMEM((tq, LANES), I32),
            pltpu.VMEM((tq, LANES), F32),
            pltpu.VMEM((tq, LANES), F32),
            pltpu.VMEM((2 * tq, LANES), F32),
            pltpu.VMEM((2 * tq, LANES), F32),
            pltpu.VMEM((2 * tq, LANES), F32),
        ],
        compiler_params=pltpu.CompilerParams(
            dimension_semantics=("arbitrary", "arbitrary"), vmem_limit_bytes=VMEM_LIMIT),
        name="dsa",
    )(qa, ka, va, iq, ik, iw)


def _diff_kernel(q_ref, k_ref, v_ref, lam_ref, g_ref, o_ref, m_ref, l_ref, acc_ref,
                 *, tq, tk, lam_init):
    j = pl.program_id(2)
    q0 = j * tq
    lane = lax.broadcasted_iota(I32, (tq, LANES), 1)
    row = lax.broadcasted_iota(I32, (tq, 1), 0)
    q_end = (((q0 + row) >> 6) + 1) << 6

    q_st = _split_lane_halves(q_ref[...].astype(F32), lane)
    m_ref[...] = jnp.full(m_ref.shape, NEG, F32)
    l_ref[...] = jnp.zeros(l_ref.shape, F32)
    acc_ref[...] = jnp.zeros(acc_ref.shape, F32)

    def logits(kt):
        r0 = pl.multiple_of(kt * tk, tk)
        return lax.dot_general(q_st, k_ref[pl.ds(r0, tk), :], _NT, preferred_element_type=F32)

    def att_tile(kt, masked):
        s = logits(kt)
        chunks = []
        for c in range(tk // LANES):
            sc = s[:, c * LANES:(c + 1) * LANES]
            if masked:
                adm = (kt * tk + c * LANES + lane) < q_end
                sc = jnp.concatenate(
                    [jnp.where(adm, sc[:tq], NEG), jnp.where(adm, sc[tq:], NEG)], axis=0)
            chunks.append(sc)
        r0 = pl.multiple_of(kt * tk, tk)
        _online_softmax_step(chunks, v_ref[pl.ds(r0, tk), :], m_ref, l_ref, acc_ref)

    def full_tile(kt, carry):
        att_tile(kt, False)
        return carry

    n_full = j * (tq // tk)
    lax.fori_loop(0, n_full, full_tile, 0)
    for t in range(tq // tk):
        att_tile(n_full + t, True)

    lam_p = lam_ref[...]
    lam = (jnp.exp(jnp.sum(lam_p[0:1] * lam_p[1:2], axis=1, keepdims=True))
           - jnp.exp(jnp.sum(lam_p[2:3] * lam_p[3:4], axis=1, keepdims=True)) + lam_init)
    o = _softmax_finish(l_ref, acc_ref)
    o = o[:tq] - lam * o[tq:]
    o_ref[...] = (_rms(o, g_ref[...]) * (1.0 - lam_init)).astype(o_ref.dtype)


def _diff_call(qb, kb, vb, lam_p, g_subln, lam_init, tq, tk):
    b, s, _ = qb.shape
    return pl.pallas_call(
        functools.partial(_diff_kernel, tq=tq, tk=tk, lam_init=lam_init),
        grid=(b, B_HEADS, s // tq),
        in_specs=[
            pl.BlockSpec((None, tq, LANES), lambda bi, h, i: (bi, i, h)),
            pl.BlockSpec((None, s, LANES), lambda bi, h, i: (bi, 0, h)),
            pl.BlockSpec((None, s, LANES), lambda bi, h, i: (bi, 0, h)),
            pl.BlockSpec((4, B_HEAD_DIM), lambda bi, h, i: (0, 0)),
            pl.BlockSpec((1, LANES), lambda bi, h, i: (0, 0)),
        ],
        out_specs=pl.BlockSpec((None, tq, LANES), lambda bi, h, i: (bi, i, h)),
        out_shape=jax.ShapeDtypeStruct((b, s, D_B), BF16),
        scratch_shapes=[
            pltpu.VMEM((2 * tq, LANES), F32),
            pltpu.VMEM((2 * tq, LANES), F32),
            pltpu.VMEM((2 * tq, LANES), F32),
        ],
        compiler_params=pltpu.CompilerParams(
            dimension_semantics=("arbitrary", "arbitrary", "arbitrary"),
            vmem_limit_bytes=VMEM_LIMIT),
        name="diff",
    )(qb, kb, vb, lam_p, g_subln)


def _ffn_kernel(x_ref, a_ref, b_ref, wo_ref, g_ref, wg_ref, wu_ref, wd_ref, gf_ref, o_ref,
                *, ff_chunk, final_norm):
    d_a = a_ref.shape[1]
    x1 = (x_ref[...]
          + jnp.dot(a_ref[...], wo_ref[:d_a, :], preferred_element_type=F32)
          + jnp.dot(b_ref[...], wo_ref[d_a:, :], preferred_element_type=F32))
    h2 = _rms(x1, g_ref[...]).astype(BF16)
    y = None
    for c in range(wg_ref.shape[1] // ff_chunk):
        cs = slice(c * ff_chunk, (c + 1) * ff_chunk)
        gate = jnp.dot(h2, wg_ref[:, cs], preferred_element_type=F32)
        up = jnp.dot(h2, wu_ref[:, cs], preferred_element_type=F32)
        act = (gate / (1.0 + jnp.exp(-gate)) * up).astype(BF16)
        down = jnp.dot(act, wd_ref[cs, :], preferred_element_type=F32)
        y = down if y is None else y + down
    x2 = x1 + y
    if final_norm:
        x2 = _rms(x2, gf_ref[...])
    o_ref[...] = x2


def _ffn_call(x, out_a, out_b, wo, g, wg, wu, wd, g_final, final_norm, tm, ff_chunk):
    t, d = x.shape
    d_ff = wg.shape[1]
    row = lambda width: pl.BlockSpec((tm, width), lambda i: (i, 0))
    const = lambda shape: pl.BlockSpec(shape, lambda i: (0, 0), pipeline_mode=pl.Buffered(1))
    return pl.pallas_call(
        functools.partial(_ffn_kernel, ff_chunk=ff_chunk, final_norm=final_norm),
        grid=(t // tm,),
        in_specs=[row(d), row(out_a.shape[1]), row(out_b.shape[1]), const(wo.shape), const((1, d)),
                  const((d, d_ff)), const((d, d_ff)), const((d_ff, d)), const((1, d))],
        out_specs=row(d),
        out_shape=jax.ShapeDtypeStruct((t, d), F32),
        compiler_params=pltpu.CompilerParams(
            dimension_semantics=("arbitrary",), vmem_limit_bytes=VMEM_LIMIT),
        name="ffn",
    )(x, out_a, out_b, wo, g, wg, wu, wd, g_final)


def _rope_tables(seq_len, dim):
    pos = jnp.arange(seq_len, dtype=F32)
    inv = ROPE_THETA ** (-jnp.arange(0, dim, 2, dtype=F32) / dim)
    ang = pos[:, None] * inv[None, :]
    cos, sin = jnp.cos(ang), jnp.sin(ang)
    reps = LANES // dim
    return (jnp.tile(jnp.concatenate([cos, cos], axis=1), (1, reps)),
            jnp.tile(jnp.concatenate([-sin, sin], axis=1), (1, reps)))


def _pack_w_in(w):
    splits = (D_A, D_A, D_A, IDX_HEADS * IDX_DIM, IDX_DIM, IDX_HEADS, D_B, D_B, D_B)
    offs = [0]
    for sz in splits:
        offs.append(offs[-1] + sz)
    qa, ka, va, iq, ik, iw, qb, kb, vb = (w[:, offs[i]:offs[i + 1]] for i in range(len(splits)))
    ik4 = jnp.tile(ik, (1, LANES // IDX_DIM))
    iw_p = jnp.pad(iw, ((0, 0), (0, LANES - IDX_HEADS)))
    return jnp.concatenate([qa, ka, va, iq, ik4, iw_p, qb, kb, vb], axis=1).astype(BF16)


def kernel(x, w_in, w_out, g_mix, lam_q1, lam_k1, lam_q2, lam_k2, g_subln, g_ffn, w_gate, w_up,
           w_down, g_final):
    b, s, d = x.shape
    depth = w_in.shape[0]
    tm_proj, tq, tm_ffn = 512, 512, 512
    tq_diff = 2 * tq if s % (2 * tq) == 0 else tq
    d_ff = w_gate.shape[-1]
    ff_chunk = d_ff // 2
    assert s % tm_proj == 0 and s % tq == 0 and (b * s) % tm_ffn == 0 and ff_chunk % LANES == 0

    tabs = _rope_tables(s, A_HEAD_DIM) + _rope_tables(s, IDX_DIM)
    for layer in range(depth):
        qa, ka, va, iq, ik, iw, qb, kb, vb = _proj_call(
            x, g_mix[layer][None, :], _pack_w_in(w_in[layer]), tabs, tm_proj)
        out_a = _dsa_call(qa, ka, va, iq, ik, iw, tq)
        lam_init = 0.8 - 0.6 * math.exp(-0.3 * layer)
        lam_p = jnp.stack([lam_q1[layer], lam_k1[layer], lam_q2[layer], lam_k2[layer]]).astype(F32)
        out_b = _diff_call(qb, kb, vb, lam_p, g_subln[layer][None, :], lam_init, tq_diff, tq_diff)
        x = _ffn_call(
            x.reshape(b * s, d), out_a.reshape(b * s, D_A), out_b.reshape(b * s, D_B),
            w_out[layer].astype(BF16), g_ffn[layer][None, :], w_gate[layer].astype(BF16),
            w_up[layer].astype(BF16), w_down[layer].astype(BF16), g_final[None, :],
            layer == depth - 1, tm_ffn, ff_chunk).reshape(b, s, d)
    return x
```

```python
import functools
import math

import jax
import jax.numpy as jnp
import numpy as np
from jax import lax
from jax.experimental import pallas as pl
from jax.experimental.pallas import tpu as pltpu

F32 = jnp.float32
BF16 = jnp.bfloat16
I32 = jnp.int32

CHUNK = 64
CHUNK_SHIFT = CHUNK.bit_length() - 1
ROPE_THETA = 10000.0
RMS_EPS = 1e-6
A_HEADS = 8
A_HEAD_DIM = 64
D_A = A_HEADS * A_HEAD_DIM
IDX_HEADS = 8
IDX_DIM = 32
TOPK_MAX = 256
B_HEADS = 4
B_HEAD_DIM = 64
D_B = B_HEADS * 2 * B_HEAD_DIM

LANES = 128
VMEM_LIMIT = 56 * 1024 * 1024

LOG2E = 1.4426950408889634
NEG = -1e30
NEG_BITS = int(np.float32(NEG).view(np.int32))
KEY_MIN = -(2 ** 31)

_OFF_QA, _OFF_KA, _OFF_VA = 0, 512, 1024
_OFF_IQ, _OFF_IK, _OFF_IW = 1536, 1792, 1920
_OFF_QB, _OFF_KB, _OFF_VB = 2048, 2560, 3072
_W_CAT = 3584

_NT = (((1,), (1,)), ((), ()))


def _rms(x, g):
    r = lax.rsqrt(jnp.mean(x * x, axis=-1, keepdims=True) + RMS_EPS)
    return x * r * g


def _proj_kernel(x_ref, g_ref, w_ref, cos_a_ref, sin_a_ref, cos_i_ref, sin_i_ref,
                 qa_ref, ka_ref, va_ref, iq_ref, ik_ref, iw_ref, qb_ref, kb_ref, vb_ref):
    h = _rms(x_ref[...], g_ref[...]).astype(BF16)
    tm = h.shape[0]
    lane = lax.broadcasted_iota(I32, (tm, LANES), 1)

    def proj(off, width):
        return jnp.dot(h, w_ref[:, off:off + width], preferred_element_type=F32)

    def rope_tile(y, cos, sin_signed, half):
        first = (lane & (2 * half - 1)) < half
        rot = jnp.where(first, pltpu.roll(y, LANES - half, 1), pltpu.roll(y, half, 1))
        return y * cos + rot * sin_signed

    def rope_store(out_ref, off, width, cos_ref, sin_ref, half, scale):
        y = proj(off, width)
        cos, sin = cos_ref[...], sin_ref[...]
        for t in range(width // LANES):
            r = rope_tile(y[:, t * LANES:(t + 1) * LANES], cos, sin, half)
            if scale != 1.0:
                r = r * scale
            out_ref[:, t * LANES:(t + 1) * LANES] = r.astype(out_ref.dtype)

    half_a, half_i = A_HEAD_DIM // 2, IDX_DIM // 2
    rope_store(qa_ref, _OFF_QA, D_A, cos_a_ref, sin_a_ref, half_a, A_HEAD_DIM ** -0.5 * LOG2E)
    rope_store(ka_ref, _OFF_KA, D_A, cos_a_ref, sin_a_ref, half_a, 1.0)
    va_ref[...] = proj(_OFF_VA, D_A).astype(va_ref.dtype)
    rope_store(iq_ref, _OFF_IQ, IDX_HEADS * IDX_DIM, cos_i_ref, sin_i_ref, half_i, 1.0)
    rope_store(ik_ref, _OFF_IK, LANES, cos_i_ref, sin_i_ref, half_i, 1.0)
    iw_ref[...] = proj(_OFF_IW, LANES) * (IDX_HEADS ** -0.5 * IDX_DIM ** -0.5)
    rope_store(qb_ref, _OFF_QB, D_B, cos_a_ref, sin_a_ref, half_a, B_HEAD_DIM ** -0.5 * LOG2E)
    rope_store(kb_ref, _OFF_KB, D_B, cos_a_ref, sin_a_ref, half_a, 1.0)
    vb_ref[...] = proj(_OFF_VB, D_B).astype(vb_ref.dtype)


def _proj_call(x, g, w_cat, tabs, tm):
    b, s, d = x.shape
    cos_a, sin_a, cos_i, sin_i = tabs
    row = lambda width: pl.BlockSpec((None, tm, width), lambda bi, i: (bi, i, 0))
    tab = pl.BlockSpec((tm, LANES), lambda bi, i: (i, 0))
    const = lambda shape: pl.BlockSpec(shape, lambda bi, i: (0, 0), pipeline_mode=pl.Buffered(1))
    widths = (D_A, D_A, D_A, IDX_HEADS * IDX_DIM, LANES, LANES, D_B, D_B, D_B)
    dtypes = (BF16, BF16, BF16, BF16, BF16, F32, BF16, BF16, BF16)
    return pl.pallas_call(
        _proj_kernel,
        grid=(b, s // tm),
        in_specs=[row(d), const((1, d)), const((d, _W_CAT)), tab, tab, tab, tab],
        out_specs=[row(w) for w in widths],
        out_shape=[jax.ShapeDtypeStruct((b, s, w), dt) for w, dt in zip(widths, dtypes)],
        compiler_params=pltpu.CompilerParams(
            dimension_semantics=("arbitrary", "arbitrary"), vmem_limit_bytes=VMEM_LIMIT),
        name="proj",
    )(x, g, w_cat, cos_a, sin_a, cos_i, sin_i)


def _online_softmax_step(chunks, v_t, m_ref, l_ref, acc_ref):
    m_prev = m_ref[...]
    m_cur = functools.reduce(jnp.maximum, chunks)
    m_new = jnp.maximum(m_prev, jnp.max(m_cur, axis=1, keepdims=True))
    alpha = jnp.exp2(m_prev - m_new)
    p = [jnp.exp2(c - m_new) for c in chunks]
    l_ref[...] = alpha * l_ref[...] + functools.reduce(jnp.add, p)
    pv = jnp.dot(jnp.concatenate(p, axis=1).astype(BF16), v_t, preferred_element_type=F32)
    acc_ref[...] = alpha * acc_ref[...] + pv
    m_ref[...] = m_new


def _softmax_finish(l_ref, acc_ref):
    return acc_ref[...] / jnp.sum(l_ref[...], axis=1, keepdims=True)


def _chunk_end(pos):
    return ((pos >> CHUNK_SHIFT) + 1) << CHUNK_SHIFT


def _split_lane_halves(tile_f32, lane):
    lo = jnp.where(lane < LANES // 2, tile_f32, 0.0)
    hi = jnp.where(lane >= LANES // 2, tile_f32, 0.0)
    return jnp.concatenate([lo, hi], axis=0).astype(BF16)


def _dsa_kernel(q_ref, k_ref, v_ref, iq_ref, ik_ref, iw_ref, o_ref,
                key_ref, iqm_ref, wb_ref, mx2_ref, mx_ref, thr_ref, need_ref, cnt_ref,
                m_ref, l_ref, acc_ref,
                *, tq, topk, att_ratio):
    tk = tq
    nch = tk // LANES
    tki = 512
    rb = 128
    j = pl.program_id(1)
    q0 = j * tq
    n_tiles = j + 1
    lane = lax.broadcasted_iota(I32, (tq, LANES), 1)
    lane_rb = lax.broadcasted_iota(I32, (rb, LANES), 1)
    row = lax.broadcasted_iota(I32, (tq, 1), 0)
    q_end = _chunk_end(q0 + row)

    nb = tq // rb
    sub8 = lax.broadcasted_iota(I32, (8, LANES), 0)
    lane8 = lax.broadcasted_iota(I32, (8, LANES), 1)
    n_adm = _chunk_end(q0 + sub8 * rb + lane8).astype(F32)
    k_eff = jnp.minimum(n_adm, float(topk))
    ones8 = jnp.ones((8, LANES), BF16)
    tri_r = lax.broadcasted_iota(I32, (LANES, 2 * LANES), 0)
    tri_c = lax.broadcasted_iota(I32, (LANES, 2 * LANES), 1)
    tri_ones = jnp.where((tri_c >= LANES) | (tri_r <= tri_c), 1.0, 0.0).astype(BF16)

    def rows_of(x8, r):
        x = jnp.broadcast_to(x8[r:r + 1, :], (rb, LANES))
        if x8.dtype == I32:
            return lax.bitcast_convert_type(jnp.transpose(lax.bitcast_convert_type(x, F32)), I32)
        return jnp.transpose(x)

    def lanes_of(col):
        out = jnp.zeros((8, LANES), col.dtype)
        for r in range(nb):
            x = jnp.broadcast_to(col[r * rb:(r + 1) * rb], (rb, LANES))
            if col.dtype == I32:
                x = lax.bitcast_convert_type(jnp.transpose(lax.bitcast_convert_type(x, F32)), I32)
            else:
                x = jnp.transpose(x)
            out = jnp.where(sub8 == r, x[0:8, :], out)
        return out

    iw = iw_ref[...]
    for h in range(IDX_HEADS):
        g = h % 4
        tile = iq_ref[:, (h // 4) * LANES:(h // 4 + 1) * LANES].astype(F32)
        keep = (lane >= g * IDX_DIM) & (lane < (g + 1) * IDX_DIM)
        iq_h = jnp.where(keep, tile, 0.0).astype(BF16)
        w_h = jnp.broadcast_to(iw[:, h:h + 1], (tq, LANES))
        for r in range(nb):
            dst = slice((r * IDX_HEADS + h) * rb, (r * IDX_HEADS + h + 1) * rb)
            iqm_ref[dst, :] = iq_h[r * rb:(r + 1) * rb]
            wb_ref[dst, :] = w_h[r * rb:(r + 1) * rb]

    mx2_ref[...] = jnp.full((tq, LANES), KEY_MIN, I32)
    mx_ref[...] = jnp.full((tq, LANES), KEY_MIN, I32)

    def index_tile(it, masked):
        r0 = pl.multiple_of(it * tki, tki)
        ik_t = ik_ref[pl.ds(r0, tki), :]
        for r in range(nb):
            rs = slice(r * rb, (r + 1) * rb)
            base = r * IDX_HEADS * rb
            d = lax.dot_general(iqm_ref[base:base + IDX_HEADS * rb, :], ik_t, _NT,
                                preferred_element_type=F32)
            for c in range(tki // LANES):
                a = jnp.zeros((rb, LANES), F32)
                for h in range(IDX_HEADS):
                    hs = slice(h * rb, (h + 1) * rb)
                    w_h = wb_ref[base + h * rb:base + (h + 1) * rb, :]
                    a = a + w_h * jnp.maximum(d[hs, c * LANES:(c + 1) * LANES], 0.0)
                bits = lax.bitcast_convert_type(a, I32)
                key = bits ^ ((bits >> 31) & 0x7FFFFFFF)
                kidx = it * tki + c * LANES + lane_rb
                key = jnp.where(key == 0, -1 - kidx, key)
                if masked:
                    key = jnp.where(kidx < q_end[rs], key, KEY_MIN)
                top1 = mx_ref[rs, :]
                mx2_ref[rs, :] = jnp.maximum(mx2_ref[rs, :], jnp.minimum(top1, key))
                mx_ref[rs, :] = jnp.maximum(top1, key)
                key_ref[it * (tki // LANES) + c, rs, :] = key

    def full_tile(it, carry):
        index_tile(it, False)
        return carry

    n_full = q0 // tki
    lax.fori_loop(0, n_full, full_tile, 0)
    for t in range(tq // tki):
        index_tile(n_full + t, True)

    def count_pass(per_block):
        cnt_ref[...] = jnp.zeros((tq, LANES), F32)

        def body(kt, carry):
            for r in range(nb):
                rs = slice(r * rb, (r + 1) * rb)
                pred = per_block(rs)
                acc = cnt_ref[rs, :]
                for c in range(nch):
                    hit = pred(key_ref[kt * nch + c, rs, :], kt * tk + c * LANES)
                    acc = acc + jnp.where(hit, 1.0, 0.0)
                cnt_ref[rs, :] = acc
            return carry

        lax.fori_loop(0, n_tiles, body, 0)
        c8 = lax.dot_general(ones8, cnt_ref[...].astype(BF16), _NT, preferred_element_type=F32)
        cnt = jnp.zeros((8, LANES), F32)
        for r in range(nb):
            cnt = jnp.where(sub8 == r, c8[:, r * rb:(r + 1) * rb], cnt)
        return cnt

    def set_rows(ref, x8):
        for r in range(nb):
            ref[r * rb:(r + 1) * rb, :] = rows_of(x8, r)

    def count_ge(t8):
        set_rows(thr_ref, t8)

        def per_block(rs):
            tb = thr_ref[rs, :]
            return lambda keys, kidx0: keys >= tb

        return count_pass(per_block)

    def flip(k):
        return k ^ ((k >> 31) & 0x7FFFFFFF)

    live = sub8 < nb
    many = 2.0 ** 30
    small = n_adm <= float(topk)
    lo0 = lanes_of(jnp.min(mx2_ref[...], axis=1, keepdims=True))
    lo0 = jnp.where(live, jnp.where(small, KEY_MIN + 1, lo0), 0)
    hi0 = jnp.where(live, lanes_of(jnp.max(mx_ref[...], axis=1, keepdims=True)), 0)
    c_lo0 = jnp.where(small, n_adm, many)
    c_hi0 = jnp.zeros((8, LANES), F32)
    value_steps = 26
    zero_lo = -(key_ref.shape[0] * LANES)

    def bis_cond(st):
        it, lo, hi, c_lo, c_above = st
        open_rows = jnp.where((c_lo > k_eff) & (lo < hi), 1.0, 0.0)
        return jnp.max(open_rows) > 0.0

    def bis_body(st):
        it, lo, hi, c_lo, c_above = st
        mid_k = (lo >> 1) + (hi >> 1) + ((lo | hi) & 1)
        v_lo = lax.bitcast_convert_type(flip(lo), F32)
        v_hi = lax.bitcast_convert_type(flip(hi), F32)
        mid_v = flip(lax.bitcast_convert_type(0.5 * v_lo + 0.5 * v_hi, I32))
        probe_lo = (lo < zero_lo) & (hi >= zero_lo)
        probe_hi = (lo >= zero_lo) & (lo < 1) & (hi >= 1)
        in_zero = (lo >= zero_lo) & (hi < 1)
        mid_v = jnp.where(probe_lo, zero_lo, jnp.where(probe_hi, 1, jnp.where(in_zero, mid_k, mid_v)))
        mid_v = jnp.minimum(jnp.maximum(mid_v, lo + 1), hi)
        mid = jnp.where(it < value_steps, mid_v, mid_k)
        active = (c_lo > k_eff) & (lo < hi)
        c_mid = count_ge(mid)
        go_up = active & (c_mid >= k_eff)
        go_dn = active & (c_mid < k_eff)
        lo = jnp.where(go_up, mid, lo)
        c_lo = jnp.where(go_up, c_mid, c_lo)
        hi = jnp.where(go_dn, mid - 1, hi)
        c_above = jnp.where(go_dn, c_mid, c_above)
        return it + 1, lo, hi, c_lo, c_above

    _, thr, _, c_thr, c_above = lax.while_loop(
        bis_cond, bis_body, (jnp.int32(0), lo0, hi0, c_lo0, c_hi0))
    set_rows(thr_ref, thr)
    need = k_eff - c_above
    has_tie = live & (c_thr > k_eff)
    any_tie = jnp.max(jnp.where(has_tie, 1.0, 0.0)) > 0.0

    def bias_pass(with_ties):
        def bias_tile(kt, carry):
            for c in range(nch):
                keys = key_ref[kt * nch + c]
                thr_b = thr_ref[...]
                tie = keys == thr_b
                if with_ties:
                    hits = jnp.dot(jnp.where(tie, 1.0, 0.0).astype(BF16), tri_ones,
                                   preferred_element_type=F32)
                    seen = cnt_ref[...]
                    tie = tie & (hits[:, :LANES] + seen <= need_ref[...])
                    cnt_ref[...] = seen + hits[:, LANES:]
                key_ref[kt * nch + c] = jnp.where((keys > thr_b) | tie, 0, NEG_BITS)
            return carry

        lax.fori_loop(0, n_tiles, bias_tile, 0)

    @pl.when(any_tie)
    def _():
        set_rows(need_ref, jnp.where(has_tie, need, 2.0 ** 30))
        cnt_ref[...] = jnp.zeros((tq, LANES), F32)
        bias_pass(True)

    @pl.when(jnp.logical_not(any_tie))
    def _():
        bias_pass(False)

    n_wide = n_tiles >> (att_ratio.bit_length() - 1)

    for hp in range(A_HEADS // 2):
        ls = slice(hp * LANES, (hp + 1) * LANES)
        q_st = _split_lane_halves(q_ref[:, ls].astype(F32), lane)
        m_ref[...] = jnp.full(m_ref.shape, NEG, F32)
        l_ref[...] = jnp.zeros(l_ref.shape, F32)
        acc_ref[...] = jnp.zeros(acc_ref.shape, F32)

        def att_step(kt, width):
            r0 = pl.multiple_of(kt * (width * tk), width * tk)
            s = lax.dot_general(q_st, k_ref[pl.ds(r0, width * tk), ls], _NT,
                                preferred_element_type=F32)
            chunks = []
            for c in range(width * nch):
                bias = lax.bitcast_convert_type(key_ref[kt * (width * nch) + c], F32)
                sc = s[:, c * LANES:(c + 1) * LANES]
                chunks.append(jnp.concatenate([sc[:tq] + bias, sc[tq:] + bias], axis=0))
            _online_softmax_step(chunks, v_ref[pl.ds(r0, width * tk), ls], m_ref, l_ref, acc_ref)

        def wide_step(kt, carry):
            att_step(kt, att_ratio)
            return carry

        def single_step(kt, carry):
            att_step(kt, 1)
            return carry

        lax.fori_loop(0, n_wide, wide_step, 0)
        lax.fori_loop(n_wide * att_ratio, n_tiles, single_step, 0)
        o = _softmax_finish(l_ref, acc_ref)
        o_ref[:, ls] = jnp.where(lane < LANES // 2, o[:tq], o[tq:]).astype(o_ref.dtype)


def _dsa_call(qa, ka, va, iq, ik, iw, tq):
    b, s, _ = qa.shape
    topk = min(TOPK_MAX, s // 4)
    qrow = lambda width: pl.BlockSpec((None, tq, width), lambda bi, i: (bi, i, 0))
    seq = lambda width: pl.BlockSpec((None, s, width), lambda bi, i: (bi, 0, 0),
                                     pipeline_mode=pl.Buffered(1))
    att_ratio = 2 if s % (2 * tq) == 0 else 1
    assert s % tq == 0 and tq % LANES == 0 and tq // LANES <= 8 and s // LANES <= 256
    return pl.pallas_call(
        functools.partial(_dsa_kernel, tq=tq, topk=topk, att_ratio=att_ratio),
        grid=(b, s // tq),
        in_specs=[qrow(D_A), seq(D_A), seq(D_A), qrow(IDX_HEADS * IDX_DIM), seq(LANES), qrow(LANES)],
        out_specs=qrow(D_A),
        out_shape=jax.ShapeDtypeStruct((b, s, D_A), BF16),
        scratch_shapes=[
            pltpu.VMEM((s // LANES, tq, LANES), I32),
            pltpu.VMEM((IDX_HEADS * tq, LANES), BF16),
            pltpu.VMEM((IDX_HEADS * tq, LANES), F32),
            pltpu.VMEM((tq, LANES), I32),
            pltpu.VMEM((tq, LANES), I32),
            pltpu.VMEM((tq, LANES), I32),
            pltpu.VMEM((tq, LANES), F32),
            pltpu.VMEM((tq, LANES), F32),
            pltpu.VMEM((2 * tq, LANES), F32),
            pltpu.VMEM((2 * tq, LANES), F32),
            pltpu.VMEM((2 * tq, LANES), F32),
        ],
        compiler_params=pltpu.CompilerParams(
            dimension_semantics=("arbitrary", "arbitrary"), vmem_limit_bytes=VMEM_LIMIT),
        name="dsa",
    )(qa, ka, va, iq, ik, iw)


def _diff_kernel(q_ref, k_ref, v_ref, lam_ref, g_ref, o_ref, m_ref, l_ref, acc_ref,
                 *, tq, tk, lam_init):
    j = pl.program_id(2)
    q0 = j * tq
    lane = lax.broadcasted_iota(I32, (tq, LANES), 1)
    row = lax.broadcasted_iota(I32, (tq, 1), 0)
    q_end = _chunk_end(q0 + row)

    q_st = _split_lane_halves(q_ref[...].astype(F32), lane)
    m_ref[...] = jnp.full(m_ref.shape, NEG, F32)
    l_ref[...] = jnp.zeros(l_ref.shape, F32)
    acc_ref[...] = jnp.zeros(acc_ref.shape, F32)

    def logits(kt):
        r0 = pl.multiple_of(kt * tk, tk)
        return lax.dot_general(q_st, k_ref[pl.ds(r0, tk), :], _NT, preferred_element_type=F32)

    def att_tile(kt, masked):
        s = logits(kt)
        chunks = []
        for c in range(tk // LANES):
            sc = s[:, c * LANES:(c + 1) * LANES]
            if masked:
                adm = (kt * tk + c * LANES + lane) < q_end
                sc = jnp.concatenate(
                    [jnp.where(adm, sc[:tq], NEG), jnp.where(adm, sc[tq:], NEG)], axis=0)
            chunks.append(sc)
        r0 = pl.multiple_of(kt * tk, tk)
        _online_softmax_step(chunks, v_ref[pl.ds(r0, tk), :], m_ref, l_ref, acc_ref)

    def full_tile(kt, carry):
        att_tile(kt, False)
        return carry

    n_full = j * (tq // tk)
    lax.fori_loop(0, n_full, full_tile, 0)
    for t in range(tq // tk):
        att_tile(n_full + t, True)

    lam_p = lam_ref[...]
    lam = (jnp.exp(jnp.sum(lam_p[0:1] * lam_p[1:2], axis=1, keepdims=True))
           - jnp.exp(jnp.sum(lam_p[2:3] * lam_p[3:4], axis=1, keepdims=True)) + lam_init)
    o = _softmax_finish(l_ref, acc_ref)
    o = o[:tq] - lam * o[tq:]
    o_ref[...] = (_rms(o, g_ref[...]) * (1.0 - lam_init)).astype(o_ref.dtype)


def _diff_call(qb, kb, vb, lam_p, g_subln, lam_init, tq, tk):
    b, s, _ = qb.shape
    return pl.pallas_call(
        functools.partial(_diff_kernel, tq=tq, tk=tk, lam_init=lam_init),
        grid=(b, B_HEADS, s // tq),
        in_specs=[
            pl.BlockSpec((None, tq, LANES), lambda bi, h, i: (bi, i, h)),
            pl.BlockSpec((None, s, LANES), lambda bi, h, i: (bi, 0, h)),
            pl.BlockSpec((None, s, LANES), lambda bi, h, i: (bi, 0, h)),
            pl.BlockSpec((4, B_HEAD_DIM), lambda bi, h, i: (0, 0)),
            pl.BlockSpec((1, LANES), lambda bi, h, i: (0, 0)),
        ],
        out_specs=pl.BlockSpec((None, tq, LANES), lambda bi, h, i: (bi, i, h)),
        out_shape=jax.ShapeDtypeStruct((b, s, D_B), BF16),
        scratch_shapes=[
            pltpu.VMEM((2 * tq, LANES), F32),
            pltpu.VMEM((2 * tq, LANES), F32),
            pltpu.VMEM((2 * tq, LANES), F32),
        ],
        compiler_params=pltpu.CompilerParams(
            dimension_semantics=("arbitrary", "arbitrary", "arbitrary"),
            vmem_limit_bytes=VMEM_LIMIT),
        name="diff",
    )(qb, kb, vb, lam_p, g_subln)


def _ffn_kernel(x_ref, a_ref, b_ref, wo_ref, g_ref, wg_ref, wu_ref, wd_ref, gf_ref, o_ref,
                *, ff_chunk, final_norm):
    d_a = a_ref.shape[1]
    x1 = (x_ref[...]
          + jnp.dot(a_ref[...], wo_ref[:d_a, :], preferred_element_type=F32)
          + jnp.dot(b_ref[...], wo_ref[d_a:, :], preferred_element_type=F32))
    h2 = _rms(x1, g_ref[...]).astype(BF16)
    y = None
    for c in range(wg_ref.shape[1] // ff_chunk):
        cs = slice(c * ff_chunk, (c + 1) * ff_chunk)
        gate = jnp.dot(h2, wg_ref[:, cs], preferred_element_type=F32)
        up = jnp.dot(h2, wu_ref[:, cs], preferred_element_type=F32)
        act = (gate / (1.0 + jnp.exp(-gate)) * up).astype(BF16)
        down = jnp.dot(act, wd_ref[cs, :], preferred_element_type=F32)
        y = down if y is None else y + down
    x2 = x1 + y
    if final_norm:
        x2 = _rms(x2, gf_ref[...])
    o_ref[...] = x2


def _ffn_call(x, out_a, out_b, wo, g, wg, wu, wd, g_final, final_norm, tm, ff_chunk):
    t, d = x.shape
    d_ff = wg.shape[1]
    row = lambda width: pl.BlockSpec((tm, width), lambda i: (i, 0))
    const = lambda shape: pl.BlockSpec(shape, lambda i: (0, 0), pipeline_mode=pl.Buffered(1))
    return pl.pallas_call(
        functools.partial(_ffn_kernel, ff_chunk=ff_chunk, final_norm=final_norm),
        grid=(t // tm,),
        in_specs=[row(d), row(out_a.shape[1]), row(out_b.shape[1]), const(wo.shape), const((1, d)),
                  const((d, d_ff)), const((d, d_ff)), const((d_ff, d)), const((1, d))],
        out_specs=row(d),
        out_shape=jax.ShapeDtypeStruct((t, d), F32),
        compiler_params=pltpu.CompilerParams(
            dimension_semantics=("arbitrary",), vmem_limit_bytes=VMEM_LIMIT),
        name="ffn",
    )(x, out_a, out_b, wo, g, wg, wu, wd, g_final)


def _rope_tables(seq_len, dim):
    pos = jnp.arange(seq_len, dtype=F32)
    inv = ROPE_THETA ** (-jnp.arange(0, dim, 2, dtype=F32) / dim)
    ang = pos[:, None] * inv[None, :]
    cos, sin = jnp.cos(ang), jnp.sin(ang)
    reps = LANES // dim
    return (jnp.tile(jnp.concatenate([cos, cos], axis=1), (1, reps)),
            jnp.tile(jnp.concatenate([-sin, sin], axis=1), (1, reps)))


def _pack_w_in(w):
    splits = (D_A, D_A, D_A, IDX_HEADS * IDX_DIM, IDX_DIM, IDX_HEADS, D_B, D_B, D_B)
    offs = [0]
    for sz in splits:
        offs.append(offs[-1] + sz)
    qa, ka, va, iq, ik, iw, qb, kb, vb = (w[:, offs[i]:offs[i + 1]] for i in range(len(splits)))
    ik4 = jnp.tile(ik, (1, LANES // IDX_DIM))
    iw_p = jnp.pad(iw, ((0, 0), (0, LANES - IDX_HEADS)))
    return jnp.concatenate([qa, ka, va, iq, ik4, iw_p, qb, kb, vb], axis=1).astype(BF16)


def kernel(x, w_in, w_out, g_mix, lam_q1, lam_k1, lam_q2, lam_k2, g_subln, g_ffn, w_gate, w_up,
           w_down, g_final):
    b, s, d = x.shape
    depth = w_in.shape[0]
    tm_proj, tq, tm_ffn = 512, 512, 512
    tq_diff = 2 * tq if s % (2 * tq) == 0 else tq
    d_ff = w_gate.shape[-1]
    ff_chunk = d_ff // 2
    assert s % tm_proj == 0 and s % tq == 0 and (b * s) % tm_ffn == 0 and ff_chunk % LANES == 0

    tabs = _rope_tables(s, A_HEAD_DIM) + _rope_tables(s, IDX_DIM)
    for layer in range(depth):
        qa, ka, va, iq, ik, iw, qb, kb, vb = _proj_call(
            x, g_mix[layer][None, :], _pack_w_in(w_in[layer]), tabs, tm_proj)
        out_a = _dsa_call(qa, ka, va, iq, ik, iw, tq)
        lam_init = 0.8 - 0.6 * math.exp(-0.3 * layer)
        lam_p = jnp.stack([lam_q1[layer], lam_k1[layer], lam_q2[layer], lam_k2[layer]]).astype(F32)
        out_b = _diff_call(qb, kb, vb, lam_p, g_subln[layer][None, :], lam_init, tq_diff, tq_diff)
        x = _ffn_call(
            x.reshape(b * s, d), out_a.reshape(b * s, D_A), out_b.reshape(b * s, D_B),
            w_out[layer].astype(BF16), g_ffn[layer][None, :], w_gate[layer].astype(BF16),
            w_up[layer].astype(BF16), w_down[layer].astype(BF16), g_final[None, :],
            layer == depth - 1, tm_ffn, ff_chunk).reshape(b, s, d)
    return x
```

```python
import functools
import math

import jax
import jax.numpy as jnp
import numpy as np
from jax import lax
from jax.experimental import pallas as pl
from jax.experimental.pallas import tpu as pltpu

F32 = jnp.float32
BF16 = jnp.bfloat16
I32 = jnp.int32

CHUNK = 64
CHUNK_SHIFT = CHUNK.bit_length() - 1
ROPE_THETA = 10000.0
RMS_EPS = 1e-6
A_HEADS = 8
A_HEAD_DIM = 64
D_A = A_HEADS * A_HEAD_DIM
IDX_HEADS = 8
IDX_DIM = 32
TOPK_MAX = 256
B_HEADS = 4
B_HEAD_DIM = 64
D_B = B_HEADS * 2 * B_HEAD_DIM

LANES = 128
VMEM_LIMIT = 56 * 1024 * 1024

LOG2E = 1.4426950408889634
NEG = -1e30
NEG_BITS = int(np.float32(NEG).view(np.int32))
KEY_MIN = -(2 ** 31)
KEY_MAX = 2 ** 31 - 1

_OFF_QA, _OFF_KA, _OFF_VA = 0, 512, 1024
_OFF_IQ, _OFF_IK, _OFF_IW = 1536, 1792, 1920
_OFF_QB, _OFF_KB, _OFF_VB = 2048, 2560, 3072
_W_CAT = 3584

_NT = (((1,), (1,)), ((), ()))


def _rms(x, g):
    r = lax.rsqrt(jnp.mean(x * x, axis=-1, keepdims=True) + RMS_EPS)
    return x * r * g


def _proj_kernel(x_ref, g_ref, w_ref, cos_a_ref, sin_a_ref, cos_i_ref, sin_i_ref,
                 qa_ref, ka_ref, va_ref, iq_ref, ik_ref, iw_ref, qb_ref, kb_ref, vb_ref):
    h = _rms(x_ref[...], g_ref[...]).astype(BF16)
    tm = h.shape[0]
    lane = lax.broadcasted_iota(I32, (tm, LANES), 1)

    def proj(off, width):
        return jnp.dot(h, w_ref[:, off:off + width], preferred_element_type=F32)

    def rope_tile(y, cos, sin_signed, half):
        first = (lane & (2 * half - 1)) < half
        rot = jnp.where(first, pltpu.roll(y, LANES - half, 1), pltpu.roll(y, half, 1))
        return y * cos + rot * sin_signed

    def rope_store(out_ref, off, width, cos_ref, sin_ref, half, scale):
        y = proj(off, width)
        cos, sin = cos_ref[...], sin_ref[...]
        for t in range(width // LANES):
            r = rope_tile(y[:, t * LANES:(t + 1) * LANES], cos, sin, half)
            if scale != 1.0:
                r = r * scale
            out_ref[:, t * LANES:(t + 1) * LANES] = r.astype(out_ref.dtype)

    half_a, half_i = A_HEAD_DIM // 2, IDX_DIM // 2
    rope_store(qa_ref, _OFF_QA, D_A, cos_a_ref, sin_a_ref, half_a, A_HEAD_DIM ** -0.5 * LOG2E)
    rope_store(ka_ref, _OFF_KA, D_A, cos_a_ref, sin_a_ref, half_a, 1.0)
    va_ref[...] = proj(_OFF_VA, D_A).astype(va_ref.dtype)
    rope_store(iq_ref, _OFF_IQ, IDX_HEADS * IDX_DIM, cos_i_ref, sin_i_ref, half_i, 1.0)
    rope_store(ik_ref, _OFF_IK, LANES, cos_i_ref, sin_i_ref, half_i, 1.0)
    iw_ref[...] = proj(_OFF_IW, LANES) * (IDX_HEADS ** -0.5 * IDX_DIM ** -0.5)
    rope_store(qb_ref, _OFF_QB, D_B, cos_a_ref, sin_a_ref, half_a, B_HEAD_DIM ** -0.5 * LOG2E)
    rope_store(kb_ref, _OFF_KB, D_B, cos_a_ref, sin_a_ref, half_a, 1.0)
    vb_ref[...] = proj(_OFF_VB, D_B).astype(vb_ref.dtype)


def _proj_call(x, g, w_cat, tabs, tm):
    b, s, d = x.shape
    cos_a, sin_a, cos_i, sin_i = tabs
    row = lambda width: pl.BlockSpec((None, tm, width), lambda bi, i: (bi, i, 0))
    tab = pl.BlockSpec((tm, LANES), lambda bi, i: (i, 0))
    const = lambda shape: pl.BlockSpec(shape, lambda bi, i: (0, 0), pipeline_mode=pl.Buffered(1))
    widths = (D_A, D_A, D_A, IDX_HEADS * IDX_DIM, LANES, LANES, D_B, D_B, D_B)
    dtypes = (BF16, BF16, BF16, BF16, BF16, F32, BF16, BF16, BF16)
    return pl.pallas_call(
        _proj_kernel,
        grid=(b, s // tm),
        in_specs=[row(d), const((1, d)), const((d, _W_CAT)), tab, tab, tab, tab],
        out_specs=[row(w) for w in widths],
        out_shape=[jax.ShapeDtypeStruct((b, s, w), dt) for w, dt in zip(widths, dtypes)],
        compiler_params=pltpu.CompilerParams(
            dimension_semantics=("arbitrary", "arbitrary"), vmem_limit_bytes=VMEM_LIMIT),
        name="proj",
    )(x, g, w_cat, cos_a, sin_a, cos_i, sin_i)


def _online_softmax_step(chunks, v_t, m_ref, l_ref, acc_ref):
    m_prev = m_ref[...]
    m_cur = functools.reduce(jnp.maximum, chunks)
    m_new = jnp.maximum(m_prev, jnp.max(m_cur, axis=1, keepdims=True))
    alpha = jnp.exp2(m_prev - m_new)
    p = [jnp.exp2(c - m_new) for c in chunks]
    l_ref[...] = alpha * l_ref[...] + functools.reduce(jnp.add, p)
    pv = jnp.dot(jnp.concatenate(p, axis=1).astype(BF16), v_t, preferred_element_type=F32)
    acc_ref[...] = alpha * acc_ref[...] + pv
    m_ref[...] = m_new


def _softmax_finish(l_ref, acc_ref):
    return acc_ref[...] / jnp.sum(l_ref[...], axis=1, keepdims=True)


def _chunk_end(pos):
    return ((pos >> CHUNK_SHIFT) + 1) << CHUNK_SHIFT


def _split_lane_halves(tile_f32, lane):
    lo = jnp.where(lane < LANES // 2, tile_f32, 0.0)
    hi = jnp.where(lane >= LANES // 2, tile_f32, 0.0)
    return jnp.concatenate([lo, hi], axis=0).astype(BF16)


def _dsa_kernel(q_ref, k_ref, v_ref, iq_ref, ik_ref, iw_ref, o_ref,
                key_ref, iqm_ref, wb_ref, mx2_ref, mx_ref, thr_ref, hi_ref, need_ref, cnt_ref,
                e1_ref, e2_ref, m_ref, l_ref, acc_ref,
                *, tq, topk, att_ratio):
    tk = tq
    nch = tk // LANES
    tki = 512
    rb = 128
    j = pl.program_id(1)
    q0 = j * tq
    n_tiles = j + 1
    lane = lax.broadcasted_iota(I32, (tq, LANES), 1)
    lane_rb = lax.broadcasted_iota(I32, (rb, LANES), 1)
    row = lax.broadcasted_iota(I32, (tq, 1), 0)
    q_end = _chunk_end(q0 + row)

    nb = tq // rb
    sub8 = lax.broadcasted_iota(I32, (8, LANES), 0)
    lane8 = lax.broadcasted_iota(I32, (8, LANES), 1)
    n_adm = _chunk_end(q0 + sub8 * rb + lane8).astype(F32)
    k_eff = jnp.minimum(n_adm, float(topk))
    ones8 = jnp.ones((8, LANES), BF16)
    tri_r = lax.broadcasted_iota(I32, (LANES, 2 * LANES), 0)
    tri_c = lax.broadcasted_iota(I32, (LANES, 2 * LANES), 1)
    tri_ones = jnp.where((tri_c >= LANES) | (tri_r <= tri_c), 1.0, 0.0).astype(BF16)

    def rows_of(x8, r):
        x = jnp.broadcast_to(x8[r:r + 1, :], (rb, LANES))
        if x8.dtype == I32:
            return lax.bitcast_convert_type(jnp.transpose(lax.bitcast_convert_type(x, F32)), I32)
        return jnp.transpose(x)

    def lanes_of(col):
        out = jnp.zeros((8, LANES), col.dtype)
        for r in range(nb):
            x = jnp.broadcast_to(col[r * rb:(r + 1) * rb], (rb, LANES))
            if col.dtype == I32:
                x = lax.bitcast_convert_type(jnp.transpose(lax.bitcast_convert_type(x, F32)), I32)
            else:
                x = jnp.transpose(x)
            out = jnp.where(sub8 == r, x[0:8, :], out)
        return out

    iw = iw_ref[...]
    for h in range(IDX_HEADS):
        g = h % 4
        tile = iq_ref[:, (h // 4) * LANES:(h // 4 + 1) * LANES].astype(F32)
        keep = (lane >= g * IDX_DIM) & (lane < (g + 1) * IDX_DIM)
        iq_h = jnp.where(keep, tile, 0.0).astype(BF16)
        w_h = jnp.broadcast_to(iw[:, h:h + 1], (tq, LANES))
        for r in range(nb):
            dst = slice((r * IDX_HEADS + h) * rb, (r * IDX_HEADS + h + 1) * rb)
            iqm_ref[dst, :] = iq_h[r * rb:(r + 1) * rb]
            wb_ref[dst, :] = w_h[r * rb:(r + 1) * rb]

    mx2_ref[...] = jnp.full((tq, LANES), KEY_MIN, I32)
    mx_ref[...] = jnp.full((tq, LANES), KEY_MIN, I32)

    def index_tile(it, masked):
        r0 = pl.multiple_of(it * tki, tki)
        ik_t = ik_ref[pl.ds(r0, tki), :]
        for r in range(nb):
            rs = slice(r * rb, (r + 1) * rb)
            base = r * IDX_HEADS * rb
            d = lax.dot_general(iqm_ref[base:base + IDX_HEADS * rb, :], ik_t, _NT,
                                preferred_element_type=F32)
            for c in range(tki // LANES):
                a = jnp.zeros((rb, LANES), F32)
                for h in range(IDX_HEADS):
                    hs = slice(h * rb, (h + 1) * rb)
                    w_h = wb_ref[base + h * rb:base + (h + 1) * rb, :]
                    a = a + w_h * jnp.maximum(d[hs, c * LANES:(c + 1) * LANES], 0.0)
                bits = lax.bitcast_convert_type(a, I32)
                key = bits ^ ((bits >> 31) & 0x7FFFFFFF)
                kidx = it * tki + c * LANES + lane_rb
                key = jnp.where(key == 0, -1 - kidx, key)
                if masked:
                    key = jnp.where(kidx < q_end[rs], key, KEY_MIN)
                top1 = mx_ref[rs, :]
                mx2_ref[rs, :] = jnp.maximum(mx2_ref[rs, :], jnp.minimum(top1, key))
                mx_ref[rs, :] = jnp.maximum(top1, key)
                key_ref[it * (tki // LANES) + c, rs, :] = key

    def full_tile(it, carry):
        index_tile(it, False)
        return carry

    n_full = q0 // tki
    lax.fori_loop(0, n_full, full_tile, 0)
    for t in range(tq // tki):
        index_tile(n_full + t, True)

    def count_pass(per_block):
        cnt_ref[...] = jnp.zeros((tq, LANES), F32)

        def body(kt, carry):
            for r in range(nb):
                rs = slice(r * rb, (r + 1) * rb)
                pred = per_block(rs)
                acc = cnt_ref[rs, :]
                for c in range(nch):
                    hit = pred(key_ref[kt * nch + c, rs, :], kt * tk + c * LANES)
                    acc = acc + jnp.where(hit, 1.0, 0.0)
                cnt_ref[rs, :] = acc
            return carry

        lax.fori_loop(0, n_tiles, body, 0)
        c8 = lax.dot_general(ones8, cnt_ref[...].astype(BF16), _NT, preferred_element_type=F32)
        cnt = jnp.zeros((8, LANES), F32)
        for r in range(nb):
            cnt = jnp.where(sub8 == r, c8[:, r * rb:(r + 1) * rb], cnt)
        return cnt

    def set_rows(ref, x8):
        for r in range(nb):
            ref[r * rb:(r + 1) * rb, :] = rows_of(x8, r)

    def count_ge(t8):
        set_rows(thr_ref, t8)

        def per_block(rs):
            tb = thr_ref[rs, :]
            return lambda keys, kidx0: keys >= tb

        return count_pass(per_block)

    def flip(k):
        return k ^ ((k >> 31) & 0x7FFFFFFF)

    live = sub8 < nb
    many = 2.0 ** 30
    small = n_adm <= float(topk)
    lo0 = lanes_of(jnp.min(mx2_ref[...], axis=1, keepdims=True))
    lo0 = jnp.where(live, jnp.where(small, KEY_MIN + 1, lo0), 0)
    hi0 = jnp.where(live, lanes_of(jnp.max(mx_ref[...], axis=1, keepdims=True)), 0)
    c_lo0 = jnp.where(small, n_adm, many)
    c_hi0 = jnp.zeros((8, LANES), F32)
    value_steps = 26
    zero_lo = -(key_ref.shape[0] * LANES)

    def bis_cond(st):
        it, lo, hi, c_lo, c_above = st
        open_rows = jnp.where((c_lo > k_eff) & (lo < hi) & (c_lo - c_above > 2.0), 1.0, 0.0)
        return jnp.max(open_rows) > 0.0

    def bis_body(st):
        it, lo, hi, c_lo, c_above = st
        mid_k = (lo >> 1) + (hi >> 1) + ((lo | hi) & 1)
        v_lo = lax.bitcast_convert_type(flip(lo), F32)
        v_hi = lax.bitcast_convert_type(flip(hi), F32)
        mid_v = flip(lax.bitcast_convert_type(0.5 * v_lo + 0.5 * v_hi, I32))
        probe_lo = (lo < zero_lo) & (hi >= zero_lo)
        probe_hi = (lo >= zero_lo) & (lo < 1) & (hi >= 1)
        in_zero = (lo >= zero_lo) & (hi < 1)
        mid_v = jnp.where(probe_lo, zero_lo, jnp.where(probe_hi, 1, jnp.where(in_zero, mid_k, mid_v)))
        mid_v = jnp.minimum(jnp.maximum(mid_v, lo + 1), hi)
        mid = jnp.where(it < value_steps, mid_v, mid_k)
        active = (c_lo > k_eff) & (lo < hi)
        c_mid = count_ge(mid)
        go_up = active & (c_mid >= k_eff)
        go_dn = active & (c_mid < k_eff)
        lo = jnp.where(go_up, mid, lo)
        c_lo = jnp.where(go_up, c_mid, c_lo)
        hi = jnp.where(go_dn, mid - 1, hi)
        c_above = jnp.where(go_dn, c_mid, c_above)
        return it + 1, lo, hi, c_lo, c_above

    _, lo_f, hi_f, c_lo_f, c_above = lax.while_loop(
        bis_cond, bis_body, (jnp.int32(0), lo0, hi0, c_lo0, c_hi0))

    pending = live & (c_lo_f > k_eff) & (lo_f < hi_f)
    e1_ref[...] = lo_f
    e2_ref[...] = lo_f

    @pl.when(jnp.max(jnp.where(pending, 1.0, 0.0)) > 0.0)
    def _():
        set_rows(thr_ref, lo_f)
        set_rows(hi_ref, hi_f)
        mx_ref[...] = jnp.full((tq, LANES), KEY_MIN, I32)
        mx2_ref[...] = jnp.full((tq, LANES), KEY_MAX, I32)

        def body(kt, carry):
            for r in range(nb):
                rs = slice(r * rb, (r + 1) * rb)
                lo_b, hi_b = thr_ref[rs, :], hi_ref[rs, :]
                top, bot = mx_ref[rs, :], mx2_ref[rs, :]
                for c in range(nch):
                    keys = key_ref[kt * nch + c, rs, :]
                    cand = (keys >= lo_b) & (keys <= hi_b)
                    top = jnp.maximum(top, jnp.where(cand, keys, KEY_MIN))
                    bot = jnp.minimum(bot, jnp.where(cand, keys, KEY_MAX))
                mx_ref[rs, :] = top
                mx2_ref[rs, :] = bot
            return carry

        lax.fori_loop(0, n_tiles, body, 0)
        e1_ref[...] = lanes_of(jnp.max(mx_ref[...], axis=1, keepdims=True))
        e2_ref[...] = lanes_of(jnp.min(mx2_ref[...], axis=1, keepdims=True))

    e1, e2 = e1_ref[...], e2_ref[...]
    thr = jnp.where(pending, e1, lo_f)
    c_thr = jnp.where(pending, c_above + jnp.where(e1 == e2, 2.0, 1.0), c_lo_f)
    set_rows(thr_ref, thr)
    need = k_eff - c_above
    has_tie = live & (c_thr > k_eff)
    any_tie = jnp.max(jnp.where(has_tie, 1.0, 0.0)) > 0.0

    def bias_pass(with_ties):
        def bias_tile(kt, carry):
            for c in range(nch):
                keys = key_ref[kt * nch + c]
                thr_b = thr_ref[...]
                tie = keys == thr_b
                if with_ties:
                    hits = jnp.dot(jnp.where(tie, 1.0, 0.0).astype(BF16), tri_ones,
                                   preferred_element_type=F32)
                    seen = cnt_ref[...]
                    tie = tie & (hits[:, :LANES] + seen <= need_ref[...])
                    cnt_ref[...] = seen + hits[:, LANES:]
                key_ref[kt * nch + c] = jnp.where((keys > thr_b) | tie, 0, NEG_BITS)
            return carry

        lax.fori_loop(0, n_tiles, bias_tile, 0)

    @pl.when(any_tie)
    def _():
        set_rows(need_ref, jnp.where(has_tie, need, 2.0 ** 30))
        cnt_ref[...] = jnp.zeros((tq, LANES), F32)
        bias_pass(True)

    @pl.when(jnp.logical_not(any_tie))
    def _():
        bias_pass(False)

    n_wide = n_tiles >> (att_ratio.bit_length() - 1)

    for hp in range(A_HEADS // 2):
        ls = slice(hp * LANES, (hp + 1) * LANES)
        q_st = _split_lane_halves(q_ref[:, ls].astype(F32), lane)
        m_ref[...] = jnp.full(m_ref.shape, NEG, F32)
        l_ref[...] = jnp.zeros(l_ref.shape, F32)
        acc_ref[...] = jnp.zeros(acc_ref.shape, F32)

        def att_step(kt, width):
            r0 = pl.multiple_of(kt * (width * tk), width * tk)
            s = lax.dot_general(q_st, k_ref[pl.ds(r0, width * tk), ls], _NT,
                                preferred_element_type=F32)
            chunks = []
            for c in range(width * nch):
                bias = lax.bitcast_convert_type(key_ref[kt * (width * nch) + c], F32)
                sc = s[:, c * LANES:(c + 1) * LANES]
                chunks.append(jnp.concatenate([sc[:tq] + bias, sc[tq:] + bias], axis=0))
            _online_softmax_step(chunks, v_ref[pl.ds(r0, width * tk), ls], m_ref, l_ref, acc_ref)

        def wide_step(kt, carry):
            att_step(kt, att_ratio)
            return carry

        def single_step(kt, carry):
            att_step(kt, 1)
            return carry

        lax.fori_loop(0, n_wide, wide_step, 0)
        lax.fori_loop(n_wide * att_ratio, n_tiles, single_step, 0)
        o = _softmax_finish(l_ref, acc_ref)
        o_ref[:, ls] = jnp.where(lane < LANES // 2, o[:tq], o[tq:]).astype(o_ref.dtype)


def _dsa_call(qa, ka, va, iq, ik, iw, tq):
    b, s, _ = qa.shape
    topk = min(TOPK_MAX, s // 4)
    qrow = lambda width: pl.BlockSpec((None, tq, width), lambda bi, i: (bi, i, 0))
    seq = lambda width: pl.BlockSpec((None, s, width), lambda bi, i: (bi, 0, 0),
                                     pipeline_mode=pl.Buffered(1))
    att_ratio = 2 if s % (2 * tq) == 0 else 1
    assert s % tq == 0 and tq % LANES == 0 and tq // LANES <= 8 and s // LANES <= 256
    return pl.pallas_call(
        functools.partial(_dsa_kernel, tq=tq, topk=topk, att_ratio=att_ratio),
        grid=(b, s // tq),
        in_specs=[qrow(D_A), seq(D_A), seq(D_A), qrow(IDX_HEADS * IDX_DIM), seq(LANES), qrow(LANES)],
        out_specs=qrow(D_A),
        out_shape=jax.ShapeDtypeStruct((b, s, D_A), BF16),
        scratch_shapes=[
            pltpu.VMEM((s // LANES, tq, LANES), I32),
            pltpu.VMEM((IDX_HEADS * tq, LANES), BF16),
            pltpu.VMEM((IDX_HEADS * tq, LANES), F32),
            pltpu.VMEM((tq, LANES), I32),
            pltpu.VMEM((tq, LANES), I32),
            pltpu.VMEM((tq, LANES), I32),
            pltpu.VMEM((tq, LANES), I32),
            pltpu.VMEM((tq, LANES), F32),
            pltpu.VMEM((tq, LANES), F32),
            pltpu.VMEM((8, LANES), I32),
            pltpu.VMEM((8, LANES), I32),
            pltpu.VMEM((2 * tq, LANES), F32),
            pltpu.VMEM((2 * tq, LANES), F32),
            pltpu.VMEM((2 * tq, LANES), F32),
        ],
        compiler_params=pltpu.CompilerParams(
            dimension_semantics=("arbitrary", "arbitrary"), vmem_limit_bytes=VMEM_LIMIT),
        name="dsa",
    )(qa, ka, va, iq, ik, iw)


def _diff_kernel(q_ref, k_ref, v_ref, lam_ref, g_ref, o_ref, m_ref, l_ref, acc_ref,
                 *, tq, tk, lam_init):
    j = pl.program_id(2)
    q0 = j * tq
    lane = lax.broadcasted_iota(I32, (tq, LANES), 1)
    row = lax.broadcasted_iota(I32, (tq, 1), 0)
    q_end = _chunk_end(q0 + row)

    q_st = _split_lane_halves(q_ref[...].astype(F32), lane)
    m_ref[...] = jnp.full(m_ref.shape, NEG, F32)
    l_ref[...] = jnp.zeros(l_ref.shape, F32)
    acc_ref[...] = jnp.zeros(acc_ref.shape, F32)

    def logits(kt):
        r0 = pl.multiple_of(kt * tk, tk)
        return lax.dot_general(q_st, k_ref[pl.ds(r0, tk), :], _NT, preferred_element_type=F32)

    def att_tile(kt, masked):
        s = logits(kt)
        chunks = []
        for c in range(tk // LANES):
            sc = s[:, c * LANES:(c + 1) * LANES]
            if masked:
                adm = (kt * tk + c * LANES + lane) < q_end
                sc = jnp.concatenate(
                    [jnp.where(adm, sc[:tq], NEG), jnp.where(adm, sc[tq:], NEG)], axis=0)
            chunks.append(sc)
        r0 = pl.multiple_of(kt * tk, tk)
        _online_softmax_step(chunks, v_ref[pl.ds(r0, tk), :], m_ref, l_ref, acc_ref)

    def full_tile(kt, carry):
        att_tile(kt, False)
        return carry

    n_full = j * (tq // tk)
    lax.fori_loop(0, n_full, full_tile, 0)
    for t in range(tq // tk):
        att_tile(n_full + t, True)

    lam_p = lam_ref[...]
    lam = (jnp.exp(jnp.sum(lam_p[0:1] * lam_p[1:2], axis=1, keepdims=True))
           - jnp.exp(jnp.sum(lam_p[2:3] * lam_p[3:4], axis=1, keepdims=True)) + lam_init)
    o = _softmax_finish(l_ref, acc_ref)
    o = o[:tq] - lam * o[tq:]
    o_ref[...] = (_rms(o, g_ref[...]) * (1.0 - lam_init)).astype(o_ref.dtype)


def _diff_call(qb, kb, vb, lam_p, g_subln, lam_init, tq, tk):
    b, s, _ = qb.shape
    return pl.pallas_call(
        functools.partial(_diff_kernel, tq=tq, tk=tk, lam_init=lam_init),
        grid=(b, B_HEADS, s // tq),
        in_specs=[
            pl.BlockSpec((None, tq, LANES), lambda bi, h, i: (bi, i, h)),
            pl.BlockSpec((None, s, LANES), lambda bi, h, i: (bi, 0, h)),
            pl.BlockSpec((None, s, LANES), lambda bi, h, i: (bi, 0, h)),
            pl.BlockSpec((4, B_HEAD_DIM), lambda bi, h, i: (0, 0)),
            pl.BlockSpec((1, LANES), lambda bi, h, i: (0, 0)),
        ],
        out_specs=pl.BlockSpec((None, tq, LANES), lambda bi, h, i: (bi, i, h)),
        out_shape=jax.ShapeDtypeStruct((b, s, D_B), BF16),
        scratch_shapes=[
            pltpu.VMEM((2 * tq, LANES), F32),
            pltpu.VMEM((2 * tq, LANES), F32),
            pltpu.VMEM((2 * tq, LANES), F32),
        ],
        compiler_params=pltpu.CompilerParams(
            dimension_semantics=("arbitrary", "arbitrary", "arbitrary"),
            vmem_limit_bytes=VMEM_LIMIT),
        name="diff",
    )(qb, kb, vb, lam_p, g_subln)


def _ffn_kernel(x_ref, a_ref, b_ref, wo_ref, g_ref, wg_ref, wu_ref, wd_ref, gf_ref, o_ref,
                *, ff_chunk, final_norm):
    d_a = a_ref.shape[1]
    x1 = (x_ref[...]
          + jnp.dot(a_ref[...], wo_ref[:d_a, :], preferred_element_type=F32)
          + jnp.dot(b_ref[...], wo_ref[d_a:, :], preferred_element_type=F32))
    h2 = _rms(x1, g_ref[...]).astype(BF16)
    y = None
    for c in range(wg_ref.shape[1] // ff_chunk):
        cs = slice(c * ff_chunk, (c + 1) * ff_chunk)
        gate = jnp.dot(h2, wg_ref[:, cs], preferred_element_type=F32)
        up = jnp.dot(h2, wu_ref[:, cs], preferred_element_type=F32)
        act = (gate / (1.0 + jnp.exp(-gate)) * up).astype(BF16)
        down = jnp.dot(act, wd_ref[cs, :], preferred_element_type=F32)
        y = down if y is None else y + down
    x2 = x1 + y
    if final_norm:
        x2 = _rms(x2, gf_ref[...])
    o_ref[...] = x2


def _ffn_call(x, out_a, out_b, wo, g, wg, wu, wd, g_final, final_norm, tm, ff_chunk):
    t, d = x.shape
    d_ff = wg.shape[1]
    row = lambda width: pl.BlockSpec((tm, width), lambda i: (i, 0))
    const = lambda shape: pl.BlockSpec(shape, lambda i: (0, 0), pipeline_mode=pl.Buffered(1))
    return pl.pallas_call(
        functools.partial(_ffn_kernel, ff_chunk=ff_chunk, final_norm=final_norm),
        grid=(t // tm,),
        in_specs=[row(d), row(out_a.shape[1]), row(out_b.shape[1]), const(wo.shape), const((1, d)),
                  const((d, d_ff)), const((d, d_ff)), const((d_ff, d)), const((1, d))],
        out_specs=row(d),
        out_shape=jax.ShapeDtypeStruct((t, d), F32),
        compiler_params=pltpu.CompilerParams(
            dimension_semantics=("arbitrary",), vmem_limit_bytes=VMEM_LIMIT),
        name="ffn",
    )(x, out_a, out_b, wo, g, wg, wu, wd, g_final)


def _rope_tables(seq_len, dim):
    pos = jnp.arange(seq_len, dtype=F32)
    inv = ROPE_THETA ** (-jnp.arange(0, dim, 2, dtype=F32) / dim)
    ang = pos[:, None] * inv[None, :]
    cos, sin = jnp.cos(ang), jnp.sin(ang)
    reps = LANES // dim
    return (jnp.tile(jnp.concatenate([cos, cos], axis=1), (1, reps)),
            jnp.tile(jnp.concatenate([-sin, sin], axis=1), (1, reps)))


def _pack_w_in(w):
    splits = (D_A, D_A, D_A, IDX_HEADS * IDX_DIM, IDX_DIM, IDX_HEADS, D_B, D_B, D_B)
    offs = [0]
    for sz in splits:
        offs.append(offs[-1] + sz)
    qa, ka, va, iq, ik, iw, qb, kb, vb = (w[:, offs[i]:offs[i + 1]] for i in range(len(splits)))
    ik4 = jnp.tile(ik, (1, LANES // IDX_DIM))
    iw_p = jnp.pad(iw, ((0, 0), (0, LANES - IDX_HEADS)))
    return jnp.concatenate([qa, ka, va, iq, ik4, iw_p, qb, kb, vb], axis=1).astype(BF16)


def kernel(x, w_in, w_out, g_mix, lam_q1, lam_k1, lam_q2, lam_k2, g_subln, g_ffn, w_gate, w_up,
           w_down, g_final):
    b, s, d = x.shape
    depth = w_in.shape[0]
    tm_proj, tq, tm_ffn = 512, 512, 512
    tq_diff = 2 * tq if s % (2 * tq) == 0 else tq
    d_ff = w_gate.shape[-1]
    ff_chunk = d_ff // 2
    assert s % tm_proj == 0 and s % tq == 0 and (b * s) % tm_ffn == 0 and ff_chunk % LANES == 0

    tabs = _rope_tables(s, A_HEAD_DIM) + _rope_tables(s, IDX_DIM)
    for layer in range(depth):
        qa, ka, va, iq, ik, iw, qb, kb, vb = _proj_call(
            x, g_mix[layer][None, :], _pack_w_in(w_in[layer]), tabs, tm_proj)
        out_a = _dsa_call(qa, ka, va, iq, ik, iw, tq)
        lam_init = 0.8 - 0.6 * math.exp(-0.3 * layer)
        lam_p = jnp.stack([lam_q1[layer], lam_k1[layer], lam_q2[layer], lam_k2[layer]]).astype(F32)
        out_b = _diff_call(qb, kb, vb, lam_p, g_subln[layer][None, :], lam_init, tq_diff, tq_diff)
        x = _ffn_call(
            x.reshape(b * s, d), out_a.reshape(b * s, D_A), out_b.reshape(b * s, D_B),
            w_out[layer].astype(BF16), g_ffn[layer][None, :], w_gate[layer].astype(BF16),
            w_up[layer].astype(BF16), w_down[layer].astype(BF16), g_final[None, :],
            layer == depth - 1, tm_ffn, ff_chunk).reshape(b, s, d)
    return x
```

```python
import functools
import math

import jax
import jax.numpy as jnp
import numpy as np
from jax import lax
from jax.experimental import pallas as pl
from jax.experimental.pallas import tpu as pltpu

F32 = jnp.float32
BF16 = jnp.bfloat16
I32 = jnp.int32

CHUNK = 64
CHUNK_SHIFT = CHUNK.bit_length() - 1
ROPE_THETA = 10000.0
RMS_EPS = 1e-6
A_HEADS = 8
A_HEAD_DIM = 64
D_A = A_HEADS * A_HEAD_DIM
IDX_HEADS = 8
IDX_DIM = 32
TOPK_MAX = 256
B_HEADS = 4
B_HEAD_DIM = 64
D_B = B_HEADS * 2 * B_HEAD_DIM

LANES = 128
VMEM_LIMIT = 56 * 1024 * 1024

LOG2E = 1.4426950408889634
NEG = -1e30
NEG_BITS = int(np.float32(NEG).view(np.int32))
KEY_MIN = -(2 ** 31)
KEY_MAX = 2 ** 31 - 1

_OFF_QA, _OFF_KA, _OFF_VA = 0, 512, 1024
_OFF_IQ, _OFF_IK, _OFF_IW = 1536, 1792, 1920
_OFF_QB, _OFF_KB, _OFF_VB = 2048, 2560, 3072
_W_CAT = 3584

_NT = (((1,), (1,)), ((), ()))


def _rms(x, g):
    r = lax.rsqrt(jnp.mean(x * x, axis=-1, keepdims=True) + RMS_EPS)
    return x * r * g


def _proj_kernel(x_ref, g_ref, w_ref, cos_a_ref, sin_a_ref, cos_i_ref, sin_i_ref,
                 qa_ref, ka_ref, va_ref, iq_ref, ik_ref, iw_ref, qb_ref, kb_ref, vb_ref):
    h = _rms(x_ref[...], g_ref[...]).astype(BF16)
    tm = h.shape[0]
    lane = lax.broadcasted_iota(I32, (tm, LANES), 1)

    def proj(off, width):
        return jnp.dot(h, w_ref[:, off:off + width], preferred_element_type=F32)

    def rope_tile(y, cos, sin_signed, half):
        first = (lane & (2 * half - 1)) < half
        rot = jnp.where(first, pltpu.roll(y, LANES - half, 1), pltpu.roll(y, half, 1))
        return y * cos + rot * sin_signed

    def rope_store(out_ref, off, width, cos_ref, sin_ref, half, scale):
        y = proj(off, width)
        cos, sin = cos_ref[...], sin_ref[...]
        for t in range(width // LANES):
            r = rope_tile(y[:, t * LANES:(t + 1) * LANES], cos, sin, half)
            if scale != 1.0:
                r = r * scale
            out_ref[:, t * LANES:(t + 1) * LANES] = r.astype(out_ref.dtype)

    half_a, half_i = A_HEAD_DIM // 2, IDX_DIM // 2
    rope_store(qa_ref, _OFF_QA, D_A, cos_a_ref, sin_a_ref, half_a, A_HEAD_DIM ** -0.5 * LOG2E)
    rope_store(ka_ref, _OFF_KA, D_A, cos_a_ref, sin_a_ref, half_a, 1.0)
    va_ref[...] = proj(_OFF_VA, D_A).astype(va_ref.dtype)
    rope_store(iq_ref, _OFF_IQ, IDX_HEADS * IDX_DIM, cos_i_ref, sin_i_ref, half_i, 1.0)
    rope_store(ik_ref, _OFF_IK, LANES, cos_i_ref, sin_i_ref, half_i, 1.0)
    iw_ref[...] = proj(_OFF_IW, LANES) * (IDX_HEADS ** -0.5 * IDX_DIM ** -0.5)
    rope_store(qb_ref, _OFF_QB, D_B, cos_a_ref, sin_a_ref, half_a, B_HEAD_DIM ** -0.5 * LOG2E)
    rope_store(kb_ref, _OFF_KB, D_B, cos_a_ref, sin_a_ref, half_a, 1.0)
    vb_ref[...] = proj(_OFF_VB, D_B).astype(vb_ref.dtype)


def _proj_call(x, g, w_cat, tabs, tm):
    b, s, d = x.shape
    cos_a, sin_a, cos_i, sin_i = tabs
    row = lambda width: pl.BlockSpec((None, tm, width), lambda bi, i: (bi, i, 0))
    tab = pl.BlockSpec((tm, LANES), lambda bi, i: (i, 0))
    const = lambda shape: pl.BlockSpec(shape, lambda bi, i: (0, 0), pipeline_mode=pl.Buffered(1))
    widths = (D_A, D_A, D_A, IDX_HEADS * IDX_DIM, LANES, LANES, D_B, D_B, D_B)
    dtypes = (BF16, BF16, BF16, BF16, BF16, F32, BF16, BF16, BF16)
    return pl.pallas_call(
        _proj_kernel,
        grid=(b, s // tm),
        in_specs=[row(d), const((1, d)), const((d, _W_CAT)), tab, tab, tab, tab],
        out_specs=[row(w) for w in widths],
        out_shape=[jax.ShapeDtypeStruct((b, s, w), dt) for w, dt in zip(widths, dtypes)],
        compiler_params=pltpu.CompilerParams(
            dimension_semantics=("arbitrary", "arbitrary"), vmem_limit_bytes=VMEM_LIMIT),
        name="proj",
    )(x, g, w_cat, cos_a, sin_a, cos_i, sin_i)


def _softmax_update(chunks, v_t, m_prev, l_prev, acc_prev):
    m_cur = functools.reduce(jnp.maximum, chunks)
    m_new = jnp.maximum(m_prev, jnp.max(m_cur, axis=1, keepdims=True))
    alpha = jnp.exp2(m_prev - m_new)
    p = [jnp.exp2(c - m_new) for c in chunks]
    l_new = alpha * l_prev + functools.reduce(jnp.add, p)
    pv = jnp.dot(jnp.concatenate(p, axis=1).astype(BF16), v_t, preferred_element_type=F32)
    return m_new, l_new, alpha * acc_prev + pv


def _online_softmax_step(chunks, v_t, m_ref, l_ref, acc_ref):
    m_ref[...], l_ref[...], acc_ref[...] = _softmax_update(
        chunks, v_t, m_ref[...], l_ref[...], acc_ref[...])


def _softmax_finish(l_ref, acc_ref):
    return acc_ref[...] / jnp.sum(l_ref[...], axis=1, keepdims=True)


def _chunk_end(pos):
    return ((pos >> CHUNK_SHIFT) + 1) << CHUNK_SHIFT


def _split_lane_halves(tile_f32, lane):
    lo = jnp.where(lane < LANES // 2, tile_f32, 0.0)
    hi = jnp.where(lane >= LANES // 2, tile_f32, 0.0)
    return jnp.concatenate([lo, hi], axis=0).astype(BF16)


def _dsa_kernel(q_ref, k_ref, v_ref, iq_ref, ik_ref, iw_ref, o_ref,
                key_ref, iqm_ref, wb_ref, mx2_ref, mx_ref, thr_ref, hi_ref, need_ref, cnt_ref,
                e1_ref, e2_ref, m_ref, l_ref, acc_ref,
                *, tq, topk, att_ratio):
    tk = tq
    nch = tk // LANES
    tki = 512
    rb = 128
    j = pl.program_id(1)
    q0 = j * tq
    n_tiles = j + 1
    lane = lax.broadcasted_iota(I32, (tq, LANES), 1)
    lane_rb = lax.broadcasted_iota(I32, (rb, LANES), 1)
    row = lax.broadcasted_iota(I32, (tq, 1), 0)
    q_end = _chunk_end(q0 + row)

    nb = tq // rb
    sub8 = lax.broadcasted_iota(I32, (8, LANES), 0)
    lane8 = lax.broadcasted_iota(I32, (8, LANES), 1)
    n_adm = _chunk_end(q0 + sub8 * rb + lane8).astype(F32)
    k_eff = jnp.minimum(n_adm, float(topk))
    ones8 = jnp.ones((8, LANES), BF16)
    tri_r = lax.broadcasted_iota(I32, (LANES, 2 * LANES), 0)
    tri_c = lax.broadcasted_iota(I32, (LANES, 2 * LANES), 1)
    tri_ones = jnp.where((tri_c >= LANES) | (tri_r <= tri_c), 1.0, 0.0).astype(BF16)

    def rows_of(x8, r):
        x = jnp.broadcast_to(x8[r:r + 1, :], (rb, LANES))
        if x8.dtype == I32:
            return lax.bitcast_convert_type(jnp.transpose(lax.bitcast_convert_type(x, F32)), I32)
        return jnp.transpose(x)

    def lanes_of(col):
        out = jnp.zeros((8, LANES), col.dtype)
        for r in range(nb):
            x = jnp.broadcast_to(col[r * rb:(r + 1) * rb], (rb, LANES))
            if col.dtype == I32:
                x = lax.bitcast_convert_type(jnp.transpose(lax.bitcast_convert_type(x, F32)), I32)
            else:
                x = jnp.transpose(x)
            out = jnp.where(sub8 == r, x[0:8, :], out)
        return out

    iw = iw_ref[...]
    for h in range(IDX_HEADS):
        g = h % 4
        tile = iq_ref[:, (h // 4) * LANES:(h // 4 + 1) * LANES].astype(F32)
        keep = (lane >= g * IDX_DIM) & (lane < (g + 1) * IDX_DIM)
        iq_h = jnp.where(keep, tile, 0.0).astype(BF16)
        w_h = jnp.broadcast_to(iw[:, h:h + 1], (tq, LANES))
        for r in range(nb):
            dst = slice((r * IDX_HEADS + h) * rb, (r * IDX_HEADS + h + 1) * rb)
            iqm_ref[dst, :] = iq_h[r * rb:(r + 1) * rb]
            wb_ref[dst, :] = w_h[r * rb:(r + 1) * rb]

    mx2_ref[...] = jnp.full((tq, LANES), KEY_MIN, I32)
    mx_ref[...] = jnp.full((tq, LANES), KEY_MIN, I32)

    def index_tile(it, masked):
        r0 = pl.multiple_of(it * tki, tki)
        ik_t = ik_ref[pl.ds(r0, tki), :]
        for r in range(nb):
            rs = slice(r * rb, (r + 1) * rb)
            base = r * IDX_HEADS * rb
            d = lax.dot_general(iqm_ref[base:base + IDX_HEADS * rb, :], ik_t, _NT,
                                preferred_element_type=F32)
            for c in range(tki // LANES):
                a = jnp.zeros((rb, LANES), F32)
                for h in range(IDX_HEADS):
                    hs = slice(h * rb, (h + 1) * rb)
                    w_h = wb_ref[base + h * rb:base + (h + 1) * rb, :]
                    a = a + w_h * jnp.maximum(d[hs, c * LANES:(c + 1) * LANES], 0.0)
                bits = lax.bitcast_convert_type(a, I32)
                key = bits ^ ((bits >> 31) & 0x7FFFFFFF)
                kidx = it * tki + c * LANES + lane_rb
                key = jnp.where(key == 0, -1 - kidx, key)
                if masked:
                    key = jnp.where(kidx < q_end[rs], key, KEY_MIN)
                top1 = mx_ref[rs, :]
                mx2_ref[rs, :] = jnp.maximum(mx2_ref[rs, :], jnp.minimum(top1, key))
                mx_ref[rs, :] = jnp.maximum(top1, key)
                key_ref[it * (tki // LANES) + c, rs, :] = key

    def full_tile(it, carry):
        index_tile(it, False)
        return carry

    n_full = q0 // tki
    lax.fori_loop(0, n_full, full_tile, 0)
    for t in range(tq // tki):
        index_tile(n_full + t, True)

    def count_pass(per_block):
        cnt_ref[...] = jnp.zeros((tq, LANES), F32)

        def body(kt, carry):
            for r in range(nb):
                rs = slice(r * rb, (r + 1) * rb)
                pred = per_block(rs)
                acc = cnt_ref[rs, :]
                for c in range(nch):
                    hit = pred(key_ref[kt * nch + c, rs, :], kt * tk + c * LANES)
                    acc = acc + jnp.where(hit, 1.0, 0.0)
                cnt_ref[rs, :] = acc
            return carry

        lax.fori_loop(0, n_tiles, body, 0)
        c8 = lax.dot_general(ones8, cnt_ref[...].astype(BF16), _NT, preferred_element_type=F32)
        cnt = jnp.zeros((8, LANES), F32)
        for r in range(nb):
            cnt = jnp.where(sub8 == r, c8[:, r * rb:(r + 1) * rb], cnt)
        return cnt

    def set_rows(ref, x8):
        for r in range(nb):
            ref[r * rb:(r + 1) * rb, :] = rows_of(x8, r)

    def count_ge(t8):
        set_rows(thr_ref, t8)

        def per_block(rs):
            tb = thr_ref[rs, :]
            return lambda keys, kidx0: keys >= tb

        return count_pass(per_block)

    def flip(k):
        return k ^ ((k >> 31) & 0x7FFFFFFF)

    live = sub8 < nb
    many = 2.0 ** 30
    small = n_adm <= float(topk)
    lo0 = lanes_of(jnp.min(mx2_ref[...], axis=1, keepdims=True))
    lo0 = jnp.where(live, jnp.where(small, KEY_MIN + 1, lo0), 0)
    hi0 = jnp.where(live, lanes_of(jnp.max(mx_ref[...], axis=1, keepdims=True)), 0)
    c_lo0 = jnp.where(small, n_adm, many)
    c_hi0 = jnp.zeros((8, LANES), F32)
    value_steps = 26
    zero_lo = -(key_ref.shape[0] * LANES)

    def bis_cond(st):
        it, lo, hi, c_lo, c_above = st
        open_rows = jnp.where((c_lo > k_eff) & (lo < hi) & (c_lo - c_above > 2.0), 1.0, 0.0)
        return jnp.max(open_rows) > 0.0

    def bis_body(st):
        it, lo, hi, c_lo, c_above = st
        mid_k = (lo >> 1) + (hi >> 1) + ((lo | hi) & 1)
        v_lo = lax.bitcast_convert_type(flip(lo), F32)
        v_hi = lax.bitcast_convert_type(flip(hi), F32)
        mid_v = flip(lax.bitcast_convert_type(0.5 * v_lo + 0.5 * v_hi, I32))
        probe_lo = (lo < zero_lo) & (hi >= zero_lo)
        probe_hi = (lo >= zero_lo) & (lo < 1) & (hi >= 1)
        in_zero = (lo >= zero_lo) & (hi < 1)
        mid_v = jnp.where(probe_lo, zero_lo, jnp.where(probe_hi, 1, jnp.where(in_zero, mid_k, mid_v)))
        mid_v = jnp.minimum(jnp.maximum(mid_v, lo + 1), hi)
        mid = jnp.where(it < value_steps, mid_v, mid_k)
        active = (c_lo > k_eff) & (lo < hi)
        c_mid = count_ge(mid)
        go_up = active & (c_mid >= k_eff)
        go_dn = active & (c_mid < k_eff)
        lo = jnp.where(go_up, mid, lo)
        c_lo = jnp.where(go_up, c_mid, c_lo)
        hi = jnp.where(go_dn, mid - 1, hi)
        c_above = jnp.where(go_dn, c_mid, c_above)
        return it + 1, lo, hi, c_lo, c_above

    _, lo_f, hi_f, c_lo_f, c_above = lax.while_loop(
        bis_cond, bis_body, (jnp.int32(0), lo0, hi0, c_lo0, c_hi0))

    pending = live & (c_lo_f > k_eff) & (lo_f < hi_f)
    e1_ref[...] = lo_f
    e2_ref[...] = lo_f

    @pl.when(jnp.max(jnp.where(pending, 1.0, 0.0)) > 0.0)
    def _():
        set_rows(thr_ref, lo_f)
        set_rows(hi_ref, hi_f)
        mx_ref[...] = jnp.full((tq, LANES), KEY_MIN, I32)
        mx2_ref[...] = jnp.full((tq, LANES), KEY_MAX, I32)

        def body(kt, carry):
            for r in range(nb):
                rs = slice(r * rb, (r + 1) * rb)
                lo_b, hi_b = thr_ref[rs, :], hi_ref[rs, :]
                top, bot = mx_ref[rs, :], mx2_ref[rs, :]
                for c in range(nch):
                    keys = key_ref[kt * nch + c, rs, :]
                    cand = (keys >= lo_b) & (keys <= hi_b)
                    top = jnp.maximum(top, jnp.where(cand, keys, KEY_MIN))
                    bot = jnp.minimum(bot, jnp.where(cand, keys, KEY_MAX))
                mx_ref[rs, :] = top
                mx2_ref[rs, :] = bot
            return carry

        lax.fori_loop(0, n_tiles, body, 0)
        e1_ref[...] = lanes_of(jnp.max(mx_ref[...], axis=1, keepdims=True))
        e2_ref[...] = lanes_of(jnp.min(mx2_ref[...], axis=1, keepdims=True))

    e1, e2 = e1_ref[...], e2_ref[...]
    thr = jnp.where(pending, e1, lo_f)
    c_thr = jnp.where(pending, c_above + jnp.where(e1 == e2, 2.0, 1.0), c_lo_f)
    set_rows(thr_ref, thr)
    need = k_eff - c_above
    has_tie = live & (c_thr > k_eff)
    any_tie = jnp.max(jnp.where(has_tie, 1.0, 0.0)) > 0.0

    def bias_pass(with_ties):
        def bias_tile(kt, carry):
            for c in range(nch):
                keys = key_ref[kt * nch + c]
                thr_b = thr_ref[...]
                tie = keys == thr_b
                if with_ties:
                    hits = jnp.dot(jnp.where(tie, 1.0, 0.0).astype(BF16), tri_ones,
                                   preferred_element_type=F32)
                    seen = cnt_ref[...]
                    tie = tie & (hits[:, :LANES] + seen <= need_ref[...])
                    cnt_ref[...] = seen + hits[:, LANES:]
                key_ref[kt * nch + c] = jnp.where((keys > thr_b) | tie, 0, NEG_BITS)
            return carry

        lax.fori_loop(0, n_tiles, bias_tile, 0)

    @pl.when(any_tie)
    def _():
        set_rows(need_ref, jnp.where(has_tie, need, 2.0 ** 30))
        cnt_ref[...] = jnp.zeros((tq, LANES), F32)
        bias_pass(True)

    @pl.when(jnp.logical_not(any_tie))
    def _():
        bias_pass(False)

    n_wide = n_tiles >> (att_ratio.bit_length() - 1)

    for hp in range(A_HEADS // 2):
        ls = slice(hp * LANES, (hp + 1) * LANES)
        q_st = _split_lane_halves(q_ref[:, ls].astype(F32), lane)
        m_ref[...] = jnp.full(m_ref.shape, NEG, F32)
        l_ref[...] = jnp.zeros(l_ref.shape, F32)
        acc_ref[...] = jnp.zeros(acc_ref.shape, F32)

        def att_step(kt, width):
            r0 = pl.multiple_of(kt * (width * tk), width * tk)
            s = lax.dot_general(q_st, k_ref[pl.ds(r0, width * tk), ls], _NT,
                                preferred_element_type=F32)
            chunks = []
            for c in range(width * nch):
                bias = lax.bitcast_convert_type(key_ref[kt * (width * nch) + c], F32)
                sc = s[:, c * LANES:(c + 1) * LANES]
                chunks.append(jnp.concatenate([sc[:tq] + bias, sc[tq:] + bias], axis=0))
            _online_softmax_step(chunks, v_ref[pl.ds(r0, width * tk), ls], m_ref, l_ref, acc_ref)

        def wide_step(kt, carry):
            att_step(kt, att_ratio)
            return carry

        def single_step(kt, carry):
            att_step(kt, 1)
            return carry

        lax.fori_loop(0, n_wide, wide_step, 0)
        lax.fori_loop(n_wide * att_ratio, n_tiles, single_step, 0)
        o = _softmax_finish(l_ref, acc_ref)
        o_ref[:, ls] = jnp.where(lane < LANES // 2, o[:tq], o[tq:]).astype(o_ref.dtype)


def _dsa_call(qa, ka, va, iq, ik, iw, tq):
    b, s, _ = qa.shape
    topk = min(TOPK_MAX, s // 4)
    qrow = lambda width: pl.BlockSpec((None, tq, width), lambda bi, i: (bi, i, 0))
    seq = lambda width: pl.BlockSpec((None, s, width), lambda bi, i: (bi, 0, 0),
                                     pipeline_mode=pl.Buffered(1))
    att_ratio = 2 if s % (2 * tq) == 0 else 1
    assert s % tq == 0 and tq % LANES == 0 and tq // LANES <= 8 and s // LANES <= 256
    return pl.pallas_call(
        functools.partial(_dsa_kernel, tq=tq, topk=topk, att_ratio=att_ratio),
        grid=(b, s // tq),
        in_specs=[qrow(D_A), seq(D_A), seq(D_A), qrow(IDX_HEADS * IDX_DIM), seq(LANES), qrow(LANES)],
        out_specs=qrow(D_A),
        out_shape=jax.ShapeDtypeStruct((b, s, D_A), BF16),
        scratch_shapes=[
            pltpu.VMEM((s // LANES, tq, LANES), I32),
            pltpu.VMEM((IDX_HEADS * tq, LANES), BF16),
            pltpu.VMEM((IDX_HEADS * tq, LANES), F32),
            pltpu.VMEM((tq, LANES), I32),
            pltpu.VMEM((tq, LANES), I32),
            pltpu.VMEM((tq, LANES), I32),
            pltpu.VMEM((tq, LANES), I32),
            pltpu.VMEM((tq, LANES), F32),
            pltpu.VMEM((tq, LANES), F32),
            pltpu.VMEM((8, LANES), I32),
            pltpu.VMEM((8, LANES), I32),
            pltpu.VMEM((2 * tq, LANES), F32),
            pltpu.VMEM((2 * tq, LANES), F32),
            pltpu.VMEM((2 * tq, LANES), F32),
        ],
        compiler_params=pltpu.CompilerParams(
            dimension_semantics=("arbitrary", "arbitrary"), vmem_limit_bytes=VMEM_LIMIT),
        name="dsa",
    )(qa, ka, va, iq, ik, iw)


def _diff_kernel(q_ref, k_ref, v_ref, lam_ref, g_ref, o_ref, m_ref, l_ref, acc_ref,
                 *, tq, tk, lam_init):
    j = pl.program_id(2)
    q0 = j * tq
    lane = lax.broadcasted_iota(I32, (tq, LANES), 1)
    row = lax.broadcasted_iota(I32, (tq, 1), 0)
    q_end = _chunk_end(q0 + row)

    q_st = _split_lane_halves(q_ref[...].astype(F32), lane)
    m_ref[...] = jnp.full(m_ref.shape, NEG, F32)
    l_ref[...] = jnp.zeros(l_ref.shape, F32)
    acc_ref[...] = jnp.zeros(acc_ref.shape, F32)

    def logits(kt):
        r0 = pl.multiple_of(kt * tk, tk)
        return lax.dot_general(q_st, k_ref[pl.ds(r0, tk), :], _NT, preferred_element_type=F32)

    def att_tile(kt, masked):
        s = logits(kt)
        chunks = []
        for c in range(tk // LANES):
            sc = s[:, c * LANES:(c + 1) * LANES]
            if masked:
                adm = (kt * tk + c * LANES + lane) < q_end
                sc = jnp.concatenate(
                    [jnp.where(adm, sc[:tq], NEG), jnp.where(adm, sc[tq:], NEG)], axis=0)
            chunks.append(sc)
        r0 = pl.multiple_of(kt * tk, tk)
        _online_softmax_step(chunks, v_ref[pl.ds(r0, tk), :], m_ref, l_ref, acc_ref)

    def full_tile(kt, carry):
        att_tile(kt, False)
        return carry

    def diag_step(key0, width, r_lo):
        rowsets = (slice(r_lo, tq), slice(tq + r_lo, 2 * tq))
        nrow = tq - r_lo
        k0 = pl.multiple_of(q0 + key0, width)
        q_sub = jnp.concatenate([q_st[rs] for rs in rowsets], axis=0)
        s = lax.dot_general(q_sub, k_ref[pl.ds(k0, width), :], _NT, preferred_element_type=F32)
        chunks = []
        for c in range(width // LANES):
            sc = s[:, c * LANES:(c + 1) * LANES]
            lane_sub = lax.broadcasted_iota(I32, (nrow, LANES), 1)
            adm = (q0 + key0 + c * LANES + lane_sub) < q_end[r_lo:]
            chunks.append(jnp.concatenate(
                [jnp.where(adm, sc[:nrow], NEG), jnp.where(adm, sc[nrow:], NEG)], axis=0))
        gather = lambda ref: jnp.concatenate([ref[rs, :] for rs in rowsets], axis=0)
        m_new, l_new, acc_new = _softmax_update(
            chunks, v_ref[pl.ds(k0, width), :], gather(m_ref), gather(l_ref), gather(acc_ref))
        for ref, new in ((m_ref, m_new), (l_ref, l_new), (acc_ref, acc_new)):
            ref[rowsets[0], :] = new[:nrow]
            ref[rowsets[1], :] = new[nrow:]

    n_full = j * (tq // tk)
    lax.fori_loop(0, n_full, full_tile, 0)
    if tq == tk:
        diag_step(0, tq // 2, 0)
        diag_step(tq // 2, tq // 2, tq // 2)
    else:
        for t in range(tq // tk):
            att_tile(n_full + t, True)

    lam_p = lam_ref[...]
    lam = (jnp.exp(jnp.sum(lam_p[0:1] * lam_p[1:2], axis=1, keepdims=True))
           - jnp.exp(jnp.sum(lam_p[2:3] * lam_p[3:4], axis=1, keepdims=True)) + lam_init)
    o = _softmax_finish(l_ref, acc_ref)
    o = o[:tq] - lam * o[tq:]
    o_ref[...] = (_rms(o, g_ref[...]) * (1.0 - lam_init)).astype(o_ref.dtype)


def _diff_call(qb, kb, vb, lam_p, g_subln, lam_init, tq, tk):
    b, s, _ = qb.shape
    return pl.pallas_call(
        functools.partial(_diff_kernel, tq=tq, tk=tk, lam_init=lam_init),
        grid=(b, B_HEADS, s // tq),
        in_specs=[
            pl.BlockSpec((None, tq, LANES), lambda bi, h, i: (bi, i, h)),
            pl.BlockSpec((None, s, LANES), lambda bi, h, i: (bi, 0, h)),
            pl.BlockSpec((None, s, LANES), lambda bi, h, i: (bi, 0, h)),
            pl.BlockSpec((4, B_HEAD_DIM), lambda bi, h, i: (0, 0)),
            pl.BlockSpec((1, LANES), lambda bi, h, i: (0, 0)),
        ],
        out_specs=pl.BlockSpec((None, tq, LANES), lambda bi, h, i: (bi, i, h)),
        out_shape=jax.ShapeDtypeStruct((b, s, D_B), BF16),
        scratch_shapes=[
            pltpu.VMEM((2 * tq, LANES), F32),
            pltpu.VMEM((2 * tq, LANES), F32),
            pltpu.VMEM((2 * tq, LANES), F32),
        ],
        compiler_params=pltpu.CompilerParams(
            dimension_semantics=("arbitrary", "arbitrary", "arbitrary"),
            vmem_limit_bytes=VMEM_LIMIT),
        name="diff",
    )(qb, kb, vb, lam_p, g_subln)


def _ffn_kernel(x_ref, a_ref, b_ref, wo_ref, g_ref, wg_ref, wu_ref, wd_ref, gf_ref, o_ref,
                *, ff_chunk, final_norm):
    d_a = a_ref.shape[1]
    x1 = (x_ref[...]
          + jnp.dot(a_ref[...], wo_ref[:d_a, :], preferred_element_type=F32)
          + jnp.dot(b_ref[...], wo_ref[d_a:, :], preferred_element_type=F32))
    h2 = _rms(x1, g_ref[...]).astype(BF16)
    y = None
    for c in range(wg_ref.shape[1] // ff_chunk):
        cs = slice(c * ff_chunk, (c + 1) * ff_chunk)
        gate = jnp.dot(h2, wg_ref[:, cs], preferred_element_type=F32)
        up = jnp.dot(h2, wu_ref[:, cs], preferred_element_type=F32)
        act = (gate / (1.0 + jnp.exp(-gate)) * up).astype(BF16)
        down = jnp.dot(act, wd_ref[cs, :], preferred_element_type=F32)
        y = down if y is None else y + down
    x2 = x1 + y
    if final_norm:
        x2 = _rms(x2, gf_ref[...])
    o_ref[...] = x2


def _ffn_call(x, out_a, out_b, wo, g, wg, wu, wd, g_final, final_norm, tm, ff_chunk):
    t, d = x.shape
    d_ff = wg.shape[1]
    row = lambda width: pl.BlockSpec((tm, width), lambda i: (i, 0))
    const = lambda shape: pl.BlockSpec(shape, lambda i: (0, 0), pipeline_mode=pl.Buffered(1))
    return pl.pallas_call(
        functools.partial(_ffn_kernel, ff_chunk=ff_chunk, final_norm=final_norm),
        grid=(t // tm,),
        in_specs=[row(d), row(out_a.shape[1]), row(out_b.shape[1]), const(wo.shape), const((1, d)),
                  const((d, d_ff)), const((d, d_ff)), const((d_ff, d)), const((1, d))],
        out_specs=row(d),
        out_shape=jax.ShapeDtypeStruct((t, d), F32),
        compiler_params=pltpu.CompilerParams(
            dimension_semantics=("arbitrary",), vmem_limit_bytes=VMEM_LIMIT),
        name="ffn",
    )(x, out_a, out_b, wo, g, wg, wu, wd, g_final)


def _rope_tables(seq_len, dim):
    pos = jnp.arange(seq_len, dtype=F32)
    inv = ROPE_THETA ** (-jnp.arange(0, dim, 2, dtype=F32) / dim)
    ang = pos[:, None] * inv[None, :]
    cos, sin = jnp.cos(ang), jnp.sin(ang)
    reps = LANES // dim
    return (jnp.tile(jnp.concatenate([cos, cos], axis=1), (1, reps)),
            jnp.tile(jnp.concatenate([-sin, sin], axis=1), (1, reps)))


def _pack_w_in(w):
    splits = (D_A, D_A, D_A, IDX_HEADS * IDX_DIM, IDX_DIM, IDX_HEADS, D_B, D_B, D_B)
    offs = [0]
    for sz in splits:
        offs.append(offs[-1] + sz)
    qa, ka, va, iq, ik, iw, qb, kb, vb = (w[:, offs[i]:offs[i + 1]] for i in range(len(splits)))
    ik4 = jnp.tile(ik, (1, LANES // IDX_DIM))
    iw_p = jnp.pad(iw, ((0, 0), (0, LANES - IDX_HEADS)))
    return jnp.concatenate([qa, ka, va, iq, ik4, iw_p, qb, kb, vb], axis=1).astype(BF16)


def kernel(x, w_in, w_out, g_mix, lam_q1, lam_k1, lam_q2, lam_k2, g_subln, g_ffn, w_gate, w_up,
           w_down, g_final):
    b, s, d = x.shape
    depth = w_in.shape[0]
    tm_proj, tq, tm_ffn = 512, 512, 512
    tq_diff = 2 * tq if s % (2 * tq) == 0 else tq
    d_ff = w_gate.shape[-1]
    ff_chunk = d_ff // 2
    assert s % tm_proj == 0 and s % tq == 0 and (b * s) % tm_ffn == 0 and ff_chunk % LANES == 0

    tabs = _rope_tables(s, A_HEAD_DIM) + _rope_tables(s, IDX_DIM)
    for layer in range(depth):
        qa, ka, va, iq, ik, iw, qb, kb, vb = _proj_call(
            x, g_mix[layer][None, :], _pack_w_in(w_in[layer]), tabs, tm_proj)
        out_a = _dsa_call(qa, ka, va, iq, ik, iw, tq)
        lam_init = 0.8 - 0.6 * math.exp(-0.3 * layer)
        lam_p = jnp.stack([lam_q1[layer], lam_k1[layer], lam_q2[layer], lam_k2[layer]]).astype(F32)
        out_b = _diff_call(qb, kb, vb, lam_p, g_subln[layer][None, :], lam_init, tq_diff, tq_diff)
        x = _ffn_call(
            x.reshape(b * s, d), out_a.reshape(b * s, D_A), out_b.reshape(b * s, D_B),
            w_out[layer].astype(BF16), g_ffn[layer][None, :], w_gate[layer].astype(BF16),
            w_up[layer].astype(BF16), w_down[layer].astype(BF16), g_final[None, :],
            layer == depth - 1, tm_ffn, ff_chunk).reshape(b, s, d)
    return x
```

```python
import functools
import math

import jax
import jax.numpy as jnp
import numpy as np
from jax import lax
from jax.experimental import pallas as pl
from jax.experimental.pallas import tpu as pltpu

F32 = jnp.float32
BF16 = jnp.bfloat16
I32 = jnp.int32

CHUNK = 64
CHUNK_SHIFT = CHUNK.bit_length() - 1
ROPE_THETA = 10000.0
RMS_EPS = 1e-6
A_HEADS = 8
A_HEAD_DIM = 64
D_A = A_HEADS * A_HEAD_DIM
IDX_HEADS = 8
IDX_DIM = 32
TOPK_MAX = 256
B_HEADS = 4
B_HEAD_DIM = 64
D_B = B_HEADS * 2 * B_HEAD_DIM

LANES = 128
VMEM_LIMIT = 56 * 1024 * 1024

LOG2E = 1.4426950408889634
NEG = -1e30
NEG_BITS = int(np.float32(NEG).view(np.int32))
KEY_MIN = -(2 ** 31)
KEY_MAX = 2 ** 31 - 1

_OFF_QA, _OFF_KA, _OFF_VA = 0, 512, 1024
_OFF_IQ, _OFF_IK, _OFF_IW = 1536, 1792, 1920
_OFF_QB, _OFF_KB, _OFF_VB = 2048, 2560, 3072
_W_CAT = 3584

_NT = (((1,), (1,)), ((), ()))


def _rms(x, g):
    r = lax.rsqrt(jnp.mean(x * x, axis=-1, keepdims=True) + RMS_EPS)
    return x * r * g


def _proj_kernel(x_ref, g_ref, w_ref, cos_a_ref, sin_a_ref, cos_i_ref, sin_i_ref,
                 qa_ref, ka_ref, va_ref, iq_ref, ik_ref, iw_ref, qb_ref, kb_ref, vb_ref):
    h = _rms(x_ref[...], g_ref[...]).astype(BF16)
    tm = h.shape[0]
    lane = lax.broadcasted_iota(I32, (tm, LANES), 1)

    def proj(off, width):
        return jnp.dot(h, w_ref[:, off:off + width], preferred_element_type=F32)

    def rope_tile(y, cos, sin_signed, half):
        first = (lane & (2 * half - 1)) < half
        rot = jnp.where(first, pltpu.roll(y, LANES - half, 1), pltpu.roll(y, half, 1))
        return y * cos + rot * sin_signed

    def rope_store(out_ref, off, width, cos_ref, sin_ref, half, scale):
        y = proj(off, width)
        cos, sin = cos_ref[...], sin_ref[...]
        for t in range(width // LANES):
            r = rope_tile(y[:, t * LANES:(t + 1) * LANES], cos, sin, half)
            if scale != 1.0:
                r = r * scale
            out_ref[:, t * LANES:(t + 1) * LANES] = r.astype(out_ref.dtype)

    half_a, half_i = A_HEAD_DIM // 2, IDX_DIM // 2
    rope_store(qa_ref, _OFF_QA, D_A, cos_a_ref, sin_a_ref, half_a, A_HEAD_DIM ** -0.5 * LOG2E)
    rope_store(ka_ref, _OFF_KA, D_A, cos_a_ref, sin_a_ref, half_a, 1.0)
    va_ref[...] = proj(_OFF_VA, D_A).astype(va_ref.dtype)
    rope_store(iq_ref, _OFF_IQ, IDX_HEADS * IDX_DIM, cos_i_ref, sin_i_ref, half_i, 1.0)
    rope_store(ik_ref, _OFF_IK, LANES, cos_i_ref, sin_i_ref, half_i, 1.0)
    iw_ref[...] = proj(_OFF_IW, LANES) * (IDX_HEADS ** -0.5 * IDX_DIM ** -0.5)
    rope_store(qb_ref, _OFF_QB, D_B, cos_a_ref, sin_a_ref, half_a, B_HEAD_DIM ** -0.5 * LOG2E)
    rope_store(kb_ref, _OFF_KB, D_B, cos_a_ref, sin_a_ref, half_a, 1.0)
    vb_ref[...] = proj(_OFF_VB, D_B).astype(vb_ref.dtype)


def _proj_call(x, g, w_cat, tabs, tm):
    b, s, d = x.shape
    cos_a, sin_a, cos_i, sin_i = tabs
    row = lambda width: pl.BlockSpec((None, tm, width), lambda bi, i: (bi, i, 0))
    tab = pl.BlockSpec((tm, LANES), lambda bi, i: (i, 0))
    const = lambda shape: pl.BlockSpec(shape, lambda bi, i: (0, 0), pipeline_mode=pl.Buffered(1))
    widths = (D_A, D_A, D_A, IDX_HEADS * IDX_DIM, LANES, LANES, D_B, D_B, D_B)
    dtypes = (BF16, BF16, BF16, BF16, BF16, F32, BF16, BF16, BF16)
    return pl.pallas_call(
        _proj_kernel,
        grid=(b, s // tm),
        in_specs=[row(d), const((1, d)), const((d, _W_CAT)), tab, tab, tab, tab],
        out_specs=[row(w) for w in widths],
        out_shape=[jax.ShapeDtypeStruct((b, s, w), dt) for w, dt in zip(widths, dtypes)],
        compiler_params=pltpu.CompilerParams(
            dimension_semantics=("arbitrary", "arbitrary"), vmem_limit_bytes=VMEM_LIMIT),
        name="proj",
    )(x, g, w_cat, cos_a, sin_a, cos_i, sin_i)


def _softmax_update(chunks, v_t, m_prev, l_prev, acc_prev):
    m_cur = functools.reduce(jnp.maximum, chunks)
    m_new = jnp.maximum(m_prev, jnp.max(m_cur, axis=1, keepdims=True))
    alpha = jnp.exp2(m_prev - m_new)
    p = [jnp.exp2(c - m_new) for c in chunks]
    l_new = alpha * l_prev + functools.reduce(jnp.add, p)
    pv = jnp.dot(jnp.concatenate(p, axis=1).astype(BF16), v_t, preferred_element_type=F32)
    return m_new, l_new, alpha * acc_prev + pv


def _online_softmax_step(chunks, v_t, m_ref, l_ref, acc_ref):
    m_ref[...], l_ref[...], acc_ref[...] = _softmax_update(
        chunks, v_t, m_ref[...], l_ref[...], acc_ref[...])


def _softmax_finish(l_ref, acc_ref):
    return acc_ref[...] / jnp.sum(l_ref[...], axis=1, keepdims=True)


def _chunk_end(pos):
    return ((pos >> CHUNK_SHIFT) + 1) << CHUNK_SHIFT


def _split_lane_halves(tile_f32, lane):
    lo = jnp.where(lane < LANES // 2, tile_f32, 0.0)
    hi = jnp.where(lane >= LANES // 2, tile_f32, 0.0)
    return jnp.concatenate([lo, hi], axis=0).astype(BF16)


def _dsa_kernel(q_ref, k_ref, v_ref, iq_ref, ik_ref, iw_ref, o_ref,
                key_ref, iqm_ref, wb_ref, mx2_ref, mx_ref, thr_ref, hi_ref, need_ref, cnt_ref,
                t2_ref, e1_ref, e2_ref, e3_ref, m_ref, l_ref, acc_ref,
                *, tq, topk, att_ratio):
    tk = tq
    nch = tk // LANES
    tki = 512
    rb = 128
    j = pl.program_id(1)
    q0 = j * tq
    n_tiles = j + 1
    lane = lax.broadcasted_iota(I32, (tq, LANES), 1)
    lane_rb = lax.broadcasted_iota(I32, (rb, LANES), 1)
    row = lax.broadcasted_iota(I32, (tq, 1), 0)
    q_end = _chunk_end(q0 + row)

    nb = tq // rb
    sub8 = lax.broadcasted_iota(I32, (8, LANES), 0)
    lane8 = lax.broadcasted_iota(I32, (8, LANES), 1)
    n_adm = _chunk_end(q0 + sub8 * rb + lane8).astype(F32)
    k_eff = jnp.minimum(n_adm, float(topk))
    ones8 = jnp.ones((8, LANES), BF16)
    tri_r = lax.broadcasted_iota(I32, (LANES, 2 * LANES), 0)
    tri_c = lax.broadcasted_iota(I32, (LANES, 2 * LANES), 1)
    tri_ones = jnp.where((tri_c >= LANES) | (tri_r <= tri_c), 1.0, 0.0).astype(BF16)

    def rows_of(x8, r):
        x = jnp.broadcast_to(x8[r:r + 1, :], (rb, LANES))
        if x8.dtype == I32:
            return lax.bitcast_convert_type(jnp.transpose(lax.bitcast_convert_type(x, F32)), I32)
        return jnp.transpose(x)

    def lanes_of(col):
        out = jnp.zeros((8, LANES), col.dtype)
        for r in range(nb):
            x = jnp.broadcast_to(col[r * rb:(r + 1) * rb], (rb, LANES))
            if col.dtype == I32:
                x = lax.bitcast_convert_type(jnp.transpose(lax.bitcast_convert_type(x, F32)), I32)
            else:
                x = jnp.transpose(x)
            out = jnp.where(sub8 == r, x[0:8, :], out)
        return out

    iw = iw_ref[...]
    for h in range(IDX_HEADS):
        g = h % 4
        tile = iq_ref[:, (h // 4) * LANES:(h // 4 + 1) * LANES].astype(F32)
        keep = (lane >= g * IDX_DIM) & (lane < (g + 1) * IDX_DIM)
        iq_h = jnp.where(keep, tile, 0.0).astype(BF16)
        w_h = jnp.broadcast_to(iw[:, h:h + 1], (tq, LANES))
        for r in range(nb):
            dst = slice((r * IDX_HEADS + h) * rb, (r * IDX_HEADS + h + 1) * rb)
            iqm_ref[dst, :] = iq_h[r * rb:(r + 1) * rb]
            wb_ref[dst, :] = w_h[r * rb:(r + 1) * rb]

    mx2_ref[...] = jnp.full((tq, LANES), KEY_MIN, I32)
    mx_ref[...] = jnp.full((tq, LANES), KEY_MIN, I32)

    def index_tile(it, masked):
        r0 = pl.multiple_of(it * tki, tki)
        ik_t = ik_ref[pl.ds(r0, tki), :]
        for r in range(nb):
            rs = slice(r * rb, (r + 1) * rb)
            base = r * IDX_HEADS * rb
            d = lax.dot_general(iqm_ref[base:base + IDX_HEADS * rb, :], ik_t, _NT,
                                preferred_element_type=F32)
            for c in range(tki // LANES):
                a = jnp.zeros((rb, LANES), F32)
                for h in range(IDX_HEADS):
                    hs = slice(h * rb, (h + 1) * rb)
                    w_h = wb_ref[base + h * rb:base + (h + 1) * rb, :]
                    a = a + w_h * jnp.maximum(d[hs, c * LANES:(c + 1) * LANES], 0.0)
                bits = lax.bitcast_convert_type(a, I32)
                key = bits ^ ((bits >> 31) & 0x7FFFFFFF)
                kidx = it * tki + c * LANES + lane_rb
                key = jnp.where(key == 0, -1 - kidx, key)
                if masked:
                    key = jnp.where(kidx < q_end[rs], key, KEY_MIN)
                top1 = mx_ref[rs, :]
                mx2_ref[rs, :] = jnp.maximum(mx2_ref[rs, :], jnp.minimum(top1, key))
                mx_ref[rs, :] = jnp.maximum(top1, key)
                key_ref[it * (tki // LANES) + c, rs, :] = key

    def full_tile(it, carry):
        index_tile(it, False)
        return carry

    n_full = q0 // tki
    lax.fori_loop(0, n_full, full_tile, 0)
    for t in range(tq // tki):
        index_tile(n_full + t, True)

    def count_pass(per_block):
        cnt_ref[...] = jnp.zeros((tq, LANES), F32)

        def body(kt, carry):
            for r in range(nb):
                rs = slice(r * rb, (r + 1) * rb)
                pred = per_block(rs)
                acc = cnt_ref[rs, :]
                for c in range(nch):
                    hit = pred(key_ref[kt * nch + c, rs, :], kt * tk + c * LANES)
                    acc = acc + jnp.where(hit, 1.0, 0.0)
                cnt_ref[rs, :] = acc
            return carry

        lax.fori_loop(0, n_tiles, body, 0)
        c8 = lax.dot_general(ones8, cnt_ref[...].astype(BF16), _NT, preferred_element_type=F32)
        cnt = jnp.zeros((8, LANES), F32)
        for r in range(nb):
            cnt = jnp.where(sub8 == r, c8[:, r * rb:(r + 1) * rb], cnt)
        return cnt

    def set_rows(ref, x8):
        for r in range(nb):
            ref[r * rb:(r + 1) * rb, :] = rows_of(x8, r)

    def count_ge(t8):
        set_rows(thr_ref, t8)

        def per_block(rs):
            tb = thr_ref[rs, :]
            return lambda keys, kidx0: keys >= tb

        return count_pass(per_block)

    def flip(k):
        return k ^ ((k >> 31) & 0x7FFFFFFF)

    live = sub8 < nb
    many = 2.0 ** 30
    small = n_adm <= float(topk)
    lo0 = lanes_of(jnp.min(mx2_ref[...], axis=1, keepdims=True))
    lo0 = jnp.where(live, jnp.where(small, KEY_MIN + 1, lo0), 0)
    hi0 = jnp.where(live, lanes_of(jnp.max(mx_ref[...], axis=1, keepdims=True)), 0)
    c_lo0 = jnp.where(small, n_adm, many)
    c_hi0 = jnp.zeros((8, LANES), F32)
    value_steps = 26
    zero_lo = -(key_ref.shape[0] * LANES)

    def bis_cond(st):
        it, lo, hi, c_lo, c_above = st
        open_rows = jnp.where((c_lo > k_eff) & (lo < hi) & (c_lo - c_above > 3.0), 1.0, 0.0)
        return jnp.max(open_rows) > 0.0

    def bis_body(st):
        it, lo, hi, c_lo, c_above = st
        mid_k = (lo >> 1) + (hi >> 1) + ((lo | hi) & 1)
        v_lo = lax.bitcast_convert_type(flip(lo), F32)
        v_hi = lax.bitcast_convert_type(flip(hi), F32)
        mid_v = flip(lax.bitcast_convert_type(0.5 * v_lo + 0.5 * v_hi, I32))
        probe_lo = (lo < zero_lo) & (hi >= zero_lo)
        probe_hi = (lo >= zero_lo) & (lo < 1) & (hi >= 1)
        in_zero = (lo >= zero_lo) & (hi < 1)
        mid_v = jnp.where(probe_lo, zero_lo, jnp.where(probe_hi, 1, jnp.where(in_zero, mid_k, mid_v)))
        mid_v = jnp.minimum(jnp.maximum(mid_v, lo + 1), hi)
        mid = jnp.where(it < value_steps, mid_v, mid_k)
        active = (c_lo > k_eff) & (lo < hi)
        c_mid = count_ge(mid)
        go_up = active & (c_mid >= k_eff)
        go_dn = active & (c_mid < k_eff)
        lo = jnp.where(go_up, mid, lo)
        c_lo = jnp.where(go_up, c_mid, c_lo)
        hi = jnp.where(go_dn, mid - 1, hi)
        c_above = jnp.where(go_dn, c_mid, c_above)
        return it + 1, lo, hi, c_lo, c_above

    _, lo_f, hi_f, c_lo_f, c_above = lax.while_loop(
        bis_cond, bis_body, (jnp.int32(0), lo0, hi0, c_lo0, c_hi0))

    pending = live & (c_lo_f > k_eff) & (lo_f < hi_f)
    e1_ref[...] = lo_f
    e2_ref[...] = lo_f
    e3_ref[...] = lo_f

    @pl.when(jnp.max(jnp.where(pending, 1.0, 0.0)) > 0.0)
    def _():
        set_rows(thr_ref, lo_f)
        set_rows(hi_ref, hi_f)
        mx_ref[...] = jnp.full((tq, LANES), KEY_MIN, I32)
        t2_ref[...] = jnp.full((tq, LANES), KEY_MIN, I32)
        mx2_ref[...] = jnp.full((tq, LANES), KEY_MAX, I32)

        def body(kt, carry):
            for r in range(nb):
                rs = slice(r * rb, (r + 1) * rb)
                lo_b, hi_b = thr_ref[rs, :], hi_ref[rs, :]
                top, sec, bot = mx_ref[rs, :], t2_ref[rs, :], mx2_ref[rs, :]
                for c in range(nch):
                    keys = key_ref[kt * nch + c, rs, :]
                    cand = (keys >= lo_b) & (keys <= hi_b)
                    e = jnp.where(cand, keys, KEY_MIN)
                    sec = jnp.maximum(sec, jnp.minimum(top, e))
                    top = jnp.maximum(top, e)
                    bot = jnp.minimum(bot, jnp.where(cand, keys, KEY_MAX))
                mx_ref[rs, :] = top
                t2_ref[rs, :] = sec
                mx2_ref[rs, :] = bot
            return carry

        lax.fori_loop(0, n_tiles, body, 0)
        lt1 = mx_ref[...]
        g1 = jnp.max(lt1, axis=1, keepdims=True)
        at_max = lt1 == g1
        n_max = jnp.sum(jnp.where(at_max, 1.0, 0.0), axis=1, keepdims=True)
        below = jnp.max(jnp.where(at_max, KEY_MIN, lt1), axis=1, keepdims=True)
        g2 = jnp.maximum(below, jnp.max(t2_ref[...], axis=1, keepdims=True))
        g2 = jnp.where(n_max >= 2.0, g1, g2)
        e1_ref[...] = lanes_of(g1)
        e2_ref[...] = lanes_of(g2)
        e3_ref[...] = lanes_of(jnp.min(mx2_ref[...], axis=1, keepdims=True))

    ka, kb, kc = e1_ref[...], e2_ref[...], e3_ref[...]
    first = (k_eff - c_above) == 1.0
    three = (c_lo_f - c_above) == 3.0
    one = jnp.ones((8, LANES), F32)
    ge_a = c_above + 1.0 + jnp.where(kb == ka, one, 0.0) + jnp.where(three & (kc == ka), one, 0.0)
    ge_b = c_above + 2.0 + jnp.where(three & (kc == kb), one, 0.0)
    thr = jnp.where(pending, jnp.where(first, ka, kb), lo_f)
    c_thr = jnp.where(pending, jnp.where(first, ge_a, ge_b), c_lo_f)
    c_gt = jnp.where(pending & jnp.logical_not(first) & (ka > kb), c_above + 1.0, c_above)
    set_rows(thr_ref, thr)
    need = k_eff - c_gt
    has_tie = live & (c_thr > k_eff)
    any_tie = jnp.max(jnp.where(has_tie, 1.0, 0.0)) > 0.0

    def bias_pass(with_ties):
        def bias_tile(kt, carry):
            for c in range(nch):
                keys = key_ref[kt * nch + c]
                thr_b = thr_ref[...]
                tie = keys == thr_b
                if with_ties:
                    hits = jnp.dot(jnp.where(tie, 1.0, 0.0).astype(BF16), tri_ones,
                                   preferred_element_type=F32)
                    seen = cnt_ref[...]
                    tie = tie & (hits[:, :LANES] + seen <= need_ref[...])
                    cnt_ref[...] = seen + hits[:, LANES:]
                key_ref[kt * nch + c] = jnp.where((keys > thr_b) | tie, 0, NEG_BITS)
            return carry

        lax.fori_loop(0, n_tiles, bias_tile, 0)

    @pl.when(any_tie)
    def _():
        set_rows(need_ref, jnp.where(has_tie, need, 2.0 ** 30))
        cnt_ref[...] = jnp.zeros((tq, LANES), F32)
        bias_pass(True)

    @pl.when(jnp.logical_not(any_tie))
    def _():
        bias_pass(False)

    n_wide = n_tiles >> (att_ratio.bit_length() - 1)

    for hp in range(A_HEADS // 2):
        ls = slice(hp * LANES, (hp + 1) * LANES)
        q_st = _split_lane_halves(q_ref[:, ls].astype(F32), lane)
        m_ref[...] = jnp.full(m_ref.shape, NEG, F32)
        l_ref[...] = jnp.zeros(l_ref.shape, F32)
        acc_ref[...] = jnp.zeros(acc_ref.shape, F32)

        def att_step(kt, width):
            r0 = pl.multiple_of(kt * (width * tk), width * tk)
            s = lax.dot_general(q_st, k_ref[pl.ds(r0, width * tk), ls], _NT,
                                preferred_element_type=F32)
            chunks = []
            for c in range(width * nch):
                bias = lax.bitcast_convert_type(key_ref[kt * (width * nch) + c], F32)
                sc = s[:, c * LANES:(c + 1) * LANES]
                chunks.append(jnp.concatenate([sc[:tq] + bias, sc[tq:] + bias], axis=0))
            _online_softmax_step(chunks, v_ref[pl.ds(r0, width * tk), ls], m_ref, l_ref, acc_ref)

        def wide_step(kt, carry):
            att_step(kt, att_ratio)
            return carry

        def single_step(kt, carry):
            att_step(kt, 1)
            return carry

        lax.fori_loop(0, n_wide, wide_step, 0)
        lax.fori_loop(n_wide * att_ratio, n_tiles, single_step, 0)
        o = _softmax_finish(l_ref, acc_ref)
        o_ref[:, ls] = jnp.where(lane < LANES // 2, o[:tq], o[tq:]).astype(o_ref.dtype)


def _dsa_call(qa, ka, va, iq, ik, iw, tq):
    b, s, _ = qa.shape
    topk = min(TOPK_MAX, s // 4)
    qrow = lambda width: pl.BlockSpec((None, tq, width), lambda bi, i: (bi, i, 0))
    seq = lambda width: pl.BlockSpec((None, s, width), lambda bi, i: (bi, 0, 0),
                                     pipeline_mode=pl.Buffered(1))
    att_ratio = 2 if s % (2 * tq) == 0 else 1
    assert s % tq == 0 and tq % LANES == 0 and tq // LANES <= 8 and s // LANES <= 256
    return pl.pallas_call(
        functools.partial(_dsa_kernel, tq=tq, topk=topk, att_ratio=att_ratio),
        grid=(b, s // tq),
        in_specs=[qrow(D_A), seq(D_A), seq(D_A), qrow(IDX_HEADS * IDX_DIM), seq(LANES), qrow(LANES)],
        out_specs=qrow(D_A),
        out_shape=jax.ShapeDtypeStruct((b, s, D_A), BF16),
        scratch_shapes=[
            pltpu.VMEM((s // LANES, tq, LANES), I32),
            pltpu.VMEM((IDX_HEADS * tq, LANES), BF16),
            pltpu.VMEM((IDX_HEADS * tq, LANES), F32),
            pltpu.VMEM((tq, LANES), I32),
            pltpu.VMEM((tq, LANES), I32),
            pltpu.VMEM((tq, LANES), I32),
            pltpu.VMEM((tq, LANES), I32),
            pltpu.VMEM((tq, LANES), F32),
            pltpu.VMEM((tq, LANES), F32),
            pltpu.VMEM((tq, LANES), I32),
            pltpu.VMEM((8, LANES), I32),
            pltpu.VMEM((8, LANES), I32),
            pltpu.VMEM((8, LANES), I32),
            pltpu.VMEM((2 * tq, LANES), F32),
            pltpu.VMEM((2 * tq, LANES), F32),
            pltpu.VMEM((2 * tq, LANES), F32),
        ],
        compiler_params=pltpu.CompilerParams(
            dimension_semantics=("arbitrary", "arbitrary"), vmem_limit_bytes=VMEM_LIMIT),
        name="dsa",
    )(qa, ka, va, iq, ik, iw)


def _diff_kernel(q_ref, k_ref, v_ref, lam_ref, g_ref, o_ref, m_ref, l_ref, acc_ref,
                 *, tq, tk, lam_init):
    j = pl.program_id(2)
    q0 = j * tq
    lane = lax.broadcasted_iota(I32, (tq, LANES), 1)
    row = lax.broadcasted_iota(I32, (tq, 1), 0)
    q_end = _chunk_end(q0 + row)

    q_st = _split_lane_halves(q_ref[...].astype(F32), lane)
    m_ref[...] = jnp.full(m_ref.shape, NEG, F32)
    l_ref[...] = jnp.zeros(l_ref.shape, F32)
    acc_ref[...] = jnp.zeros(acc_ref.shape, F32)

    def logits(kt):
        r0 = pl.multiple_of(kt * tk, tk)
        return lax.dot_general(q_st, k_ref[pl.ds(r0, tk), :], _NT, preferred_element_type=F32)

    def att_tile(kt, masked):
        s = logits(kt)
        chunks = []
        for c in range(tk // LANES):
            sc = s[:, c * LANES:(c + 1) * LANES]
            if masked:
                adm = (kt * tk + c * LANES + lane) < q_end
                sc = jnp.concatenate(
                    [jnp.where(adm, sc[:tq], NEG), jnp.where(adm, sc[tq:], NEG)], axis=0)
            chunks.append(sc)
        r0 = pl.multiple_of(kt * tk, tk)
        _online_softmax_step(chunks, v_ref[pl.ds(r0, tk), :], m_ref, l_ref, acc_ref)

    def full_tile(kt, carry):
        att_tile(kt, False)
        return carry

    def diag_step(key0, width, r_lo):
        rowsets = (slice(r_lo, tq), slice(tq + r_lo, 2 * tq))
        nrow = tq - r_lo
        k0 = pl.multiple_of(q0 + key0, width)
        q_sub = jnp.concatenate([q_st[rs] for rs in rowsets], axis=0)
        s = lax.dot_general(q_sub, k_ref[pl.ds(k0, width), :], _NT, preferred_element_type=F32)
        chunks = []
        for c in range(width // LANES):
            sc = s[:, c * LANES:(c + 1) * LANES]
            lane_sub = lax.broadcasted_iota(I32, (nrow, LANES), 1)
            adm = (q0 + key0 + c * LANES + lane_sub) < q_end[r_lo:]
            chunks.append(jnp.concatenate(
                [jnp.where(adm, sc[:nrow], NEG), jnp.where(adm, sc[nrow:], NEG)], axis=0))
        gather = lambda ref: jnp.concatenate([ref[rs, :] for rs in rowsets], axis=0)
        m_new, l_new, acc_new = _softmax_update(
            chunks, v_ref[pl.ds(k0, width), :], gather(m_ref), gather(l_ref), gather(acc_ref))
        for ref, new in ((m_ref, m_new), (l_ref, l_new), (acc_ref, acc_new)):
            ref[rowsets[0], :] = new[:nrow]
            ref[rowsets[1], :] = new[nrow:]

    n_full = j * (tq // tk)
    lax.fori_loop(0, n_full, full_tile, 0)
    if tq == tk:
        diag_step(0, tq // 2, 0)
        diag_step(tq // 2, tq // 2, tq // 2)
    else:
        for t in range(tq // tk):
            att_tile(n_full + t, True)

    lam_p = lam_ref[...]
    lam = (jnp.exp(jnp.sum(lam_p[0:1] * lam_p[1:2], axis=1, keepdims=True))
           - jnp.exp(jnp.sum(lam_p[2:3] * lam_p[3:4], axis=1, keepdims=True)) + lam_init)
    o = _softmax_finish(l_ref, acc_ref)
    o = o[:tq] - lam * o[tq:]
    o_ref[...] = (_rms(o, g_ref[...]) * (1.0 - lam_init)).astype(o_ref.dtype)


def _diff_call(qb, kb, vb, lam_p, g_subln, lam_init, tq, tk):
    b, s, _ = qb.shape
    return pl.pallas_call(
        functools.partial(_diff_kernel, tq=tq, tk=tk, lam_init=lam_init),
        grid=(b, B_HEADS, s // tq),
        in_specs=[
            pl.BlockSpec((None, tq, LANES), lambda bi, h, i: (bi, i, h)),
            pl.BlockSpec((None, s, LANES), lambda bi, h, i: (bi, 0, h)),
            pl.BlockSpec((None, s, LANES), lambda bi, h, i: (bi, 0, h)),
            pl.BlockSpec((4, B_HEAD_DIM), lambda bi, h, i: (0, 0)),
            pl.BlockSpec((1, LANES), lambda bi, h, i: (0, 0)),
        ],
        out_specs=pl.BlockSpec((None, tq, LANES), lambda bi, h, i: (bi, i, h)),
        out_shape=jax.ShapeDtypeStruct((b, s, D_B), BF16),
        scratch_shapes=[
            pltpu.VMEM((2 * tq, LANES), F32),
            pltpu.VMEM((2 * tq, LANES), F32),
            pltpu.VMEM((2 * tq, LANES), F32),
        ],
        compiler_params=pltpu.CompilerParams(
            dimension_semantics=("arbitrary", "arbitrary", "arbitrary"),
            vmem_limit_bytes=VMEM_LIMIT),
        name="diff",
    )(qb, kb, vb, lam_p, g_subln)


def _ffn_kernel(x_ref, a_ref, b_ref, wo_ref, g_ref, wg_ref, wu_ref, wd_ref, gf_ref, o_ref,
                *, ff_chunk, final_norm):
    d_a = a_ref.shape[1]
    x1 = (x_ref[...]
          + jnp.dot(a_ref[...], wo_ref[:d_a, :], preferred_element_type=F32)
          + jnp.dot(b_ref[...], wo_ref[d_a:, :], preferred_element_type=F32))
    h2 = _rms(x1, g_ref[...]).astype(BF16)
    y = None
    for c in range(wg_ref.shape[1] // ff_chunk):
        cs = slice(c * ff_chunk, (c + 1) * ff_chunk)
        gate = jnp.dot(h2, wg_ref[:, cs], preferred_element_type=F32)
        up = jnp.dot(h2, wu_ref[:, cs], preferred_element_type=F32)
        act = (gate / (1.0 + jnp.exp(-gate)) * up).astype(BF16)
        down = jnp.dot(act, wd_ref[cs, :], preferred_element_type=F32)
        y = down if y is None else y + down
    x2 = x1 + y
    if final_norm:
        x2 = _rms(x2, gf_ref[...])
    o_ref[...] = x2


def _ffn_call(x, out_a, out_b, wo, g, wg, wu, wd, g_final, final_norm, tm, ff_chunk):
    t, d = x.shape
    d_ff = wg.shape[1]
    row = lambda width: pl.BlockSpec((tm, width), lambda i: (i, 0))
    const = lambda shape: pl.BlockSpec(shape, lambda i: (0, 0), pipeline_mode=pl.Buffered(1))
    return pl.pallas_call(
        functools.partial(_ffn_kernel, ff_chunk=ff_chunk, final_norm=final_norm),
        grid=(t // tm,),
        in_specs=[row(d), row(out_a.shape[1]), row(out_b.shape[1]), const(wo.shape), const((1, d)),
                  const((d, d_ff)), const((d, d_ff)), const((d_ff, d)), const((1, d))],
        out_specs=row(d),
        out_shape=jax.ShapeDtypeStruct((t, d), F32),
        compiler_params=pltpu.CompilerParams(
            dimension_semantics=("arbitrary",), vmem_limit_bytes=VMEM_LIMIT),
        name="ffn",
    )(x, out_a, out_b, wo, g, wg, wu, wd, g_final)


def _rope_tables(seq_len, dim):
    pos = jnp.arange(seq_len, dtype=F32)
    inv = ROPE_THETA ** (-jnp.arange(0, dim, 2, dtype=F32) / dim)
    ang = pos[:, None] * inv[None, :]
    cos, sin = jnp.cos(ang), jnp.sin(ang)
    reps = LANES // dim
    return (jnp.tile(jnp.concatenate([cos, cos], axis=1), (1, reps)),
            jnp.tile(jnp.concatenate([-sin, sin], axis=1), (1, reps)))


def _pack_w_in(w):
    splits = (D_A, D_A, D_A, IDX_HEADS * IDX_DIM, IDX_DIM, IDX_HEADS, D_B, D_B, D_B)
    offs = [0]
    for sz in splits:
        offs.append(offs[-1] + sz)
    qa, ka, va, iq, ik, iw, qb, kb, vb = (w[:, offs[i]:offs[i + 1]] for i in range(len(splits)))
    ik4 = jnp.tile(ik, (1, LANES // IDX_DIM))
    iw_p = jnp.pad(iw, ((0, 0), (0, LANES - IDX_HEADS)))
    return jnp.concatenate([qa, ka, va, iq, ik4, iw_p, qb, kb, vb], axis=1).astype(BF16)


def kernel(x, w_in, w_out, g_mix, lam_q1, lam_k1, lam_q2, lam_k2, g_subln, g_ffn, w_gate, w_up,
           w_down, g_final):
    b, s, d = x.shape
    depth = w_in.shape[0]
    tm_proj, tq, tm_ffn = 512, 512, 512
    tq_diff = 2 * tq if s % (2 * tq) == 0 else tq
    d_ff = w_gate.shape[-1]
    ff_chunk = d_ff // 2
    assert s % tm_proj == 0 and s % tq == 0 and (b * s) % tm_ffn == 0 and ff_chunk % LANES == 0

    tabs = _rope_tables(s, A_HEAD_DIM) + _rope_tables(s, IDX_DIM)
    for layer in range(depth):
        qa, ka, va, iq, ik, iw, qb, kb, vb = _proj_call(
            x, g_mix[layer][None, :], _pack_w_in(w_in[layer]), tabs, tm_proj)
        out_a = _dsa_call(qa, ka, va, iq, ik, iw, tq)
        lam_init = 0.8 - 0.6 * math.exp(-0.3 * layer)
        lam_p = jnp.stack([lam_q1[layer], lam_k1[layer], lam_q2[layer], lam_k2[layer]]).astype(F32)
        out_b = _diff_call(qb, kb, vb, lam_p, g_subln[layer][None, :], lam_init, tq_diff, tq_diff)
        x = _ffn_call(
            x.reshape(b * s, d), out_a.reshape(b * s, D_A), out_b.reshape(b * s, D_B),
            w_out[layer].astype(BF16), g_ffn[layer][None, :], w_gate[layer].astype(BF16),
            w_up[layer].astype(BF16), w_down[layer].astype(BF16), g_final[None, :],
            layer == depth - 1, tm_ffn, ff_chunk).reshape(b, s, d)
    return x
```

```python
import functools
import math

import jax
import jax.numpy as jnp
import numpy as np
from jax import lax
from jax.experimental import pallas as pl
from jax.experimental.pallas import tpu as pltpu

F32 = jnp.float32
BF16 = jnp.bfloat16
I32 = jnp.int32

CHUNK = 64
CHUNK_SHIFT = CHUNK.bit_length() - 1
ROPE_THETA = 10000.0
RMS_EPS = 1e-6
A_HEADS = 8
A_HEAD_DIM = 64
D_A = A_HEADS * A_HEAD_DIM
IDX_HEADS = 8
IDX_DIM = 32
TOPK_MAX = 256
B_HEADS = 4
B_HEAD_DIM = 64
D_B = B_HEADS * 2 * B_HEAD_DIM

LANES = 128
VMEM_LIMIT = 56 * 1024 * 1024

LOG2E = 1.4426950408889634
NEG = -1e30
NEG_BITS = int(np.float32(NEG).view(np.int32))
KEY_MIN = -(2 ** 31)
KEY_MAX = 2 ** 31 - 1

_OFF_QA, _OFF_KA, _OFF_VA = 0, 512, 1024
_OFF_IQ, _OFF_IK, _OFF_IW = 1536, 1792, 1920
_OFF_QB, _OFF_KB, _OFF_VB = 2048, 2560, 3072
_W_CAT = 3584

_NT = (((1,), (1,)), ((), ()))


def _rms(x, g):
    r = lax.rsqrt(jnp.mean(x * x, axis=-1, keepdims=True) + RMS_EPS)
    return x * r * g


def _proj_kernel(x_ref, g_ref, w_ref, cos_a_ref, sin_a_ref, cos_i_ref, sin_i_ref,
                 qa_ref, ka_ref, va_ref, iq_ref, ik_ref, iw_ref, qb_ref, kb_ref, vb_ref):
    h = _rms(x_ref[...], g_ref[...]).astype(BF16)
    tm = h.shape[0]
    lane = lax.broadcasted_iota(I32, (tm, LANES), 1)

    def proj(off, width):
        return jnp.dot(h, w_ref[:, off:off + width], preferred_element_type=F32)

    def rope_tile(y, cos, sin_signed, half):
        first = (lane & (2 * half - 1)) < half
        rot = jnp.where(first, pltpu.roll(y, LANES - half, 1), pltpu.roll(y, half, 1))
        return y * cos + rot * sin_signed

    def rope_store(out_ref, off, width, cos_ref, sin_ref, half, scale):
        y = proj(off, width)
        cos, sin = cos_ref[...], sin_ref[...]
        for t in range(width // LANES):
            r = rope_tile(y[:, t * LANES:(t + 1) * LANES], cos, sin, half)
            if scale != 1.0:
                r = r * scale
            out_ref[:, t * LANES:(t + 1) * LANES] = r.astype(out_ref.dtype)

    half_a, half_i = A_HEAD_DIM // 2, IDX_DIM // 2
    rope_store(qa_ref, _OFF_QA, D_A, cos_a_ref, sin_a_ref, half_a, A_HEAD_DIM ** -0.5 * LOG2E)
    rope_store(ka_ref, _OFF_KA, D_A, cos_a_ref, sin_a_ref, half_a, 1.0)
    va_ref[...] = proj(_OFF_VA, D_A).astype(va_ref.dtype)
    rope_store(iq_ref, _OFF_IQ, IDX_HEADS * IDX_DIM, cos_i_ref, sin_i_ref, half_i, 1.0)
    rope_store(ik_ref, _OFF_IK, LANES, cos_i_ref, sin_i_ref, half_i, 1.0)
    iw_ref[...] = proj(_OFF_IW, LANES) * (IDX_HEADS ** -0.5 * IDX_DIM ** -0.5)
    rope_store(qb_ref, _OFF_QB, D_B, cos_a_ref, sin_a_ref, half_a, B_HEAD_DIM ** -0.5 * LOG2E)
    rope_store(kb_ref, _OFF_KB, D_B, cos_a_ref, sin_a_ref, half_a, 1.0)
    vb_ref[...] = proj(_OFF_VB, D_B).astype(vb_ref.dtype)


def _proj_call(x, g, w_cat, tabs, tm):
    b, s, d = x.shape
    cos_a, sin_a, cos_i, sin_i = tabs
    row = lambda width: pl.BlockSpec((None, tm, width), lambda bi, i: (bi, i, 0))
    tab = pl.BlockSpec((tm, LANES), lambda bi, i: (i, 0))
    const = lambda shape: pl.BlockSpec(shape, lambda bi, i: (0, 0), pipeline_mode=pl.Buffered(1))
    widths = (D_A, D_A, D_A, IDX_HEADS * IDX_DIM, LANES, LANES, D_B, D_B, D_B)
    dtypes = (BF16, BF16, BF16, BF16, BF16, F32, BF16, BF16, BF16)
    return pl.pallas_call(
        _proj_kernel,
        grid=(b, s // tm),
        in_specs=[row(d), const((1, d)), const((d, _W_CAT)), tab, tab, tab, tab],
        out_specs=[row(w) for w in widths],
        out_shape=[jax.ShapeDtypeStruct((b, s, w), dt) for w, dt in zip(widths, dtypes)],
        compiler_params=pltpu.CompilerParams(
            dimension_semantics=("arbitrary", "arbitrary"), vmem_limit_bytes=VMEM_LIMIT),
        name="proj",
    )(x, g, w_cat, cos_a, sin_a, cos_i, sin_i)


def _softmax_update(chunks, v_t, m_prev, l_prev, acc_prev, own=None):
    m_cur = functools.reduce(jnp.maximum, chunks)
    m_new = jnp.maximum(m_prev, jnp.max(m_cur, axis=1, keepdims=True))
    alpha = jnp.exp2(m_prev - m_new)
    p = [jnp.exp2(c - m_new) for c in chunks]
    l_new = alpha * l_prev + functools.reduce(jnp.add, p)
    pv = jnp.dot(jnp.concatenate(p, axis=1).astype(BF16), v_t, preferred_element_type=F32)
    if own is not None:
        pv = own(pv)
    return m_new, l_new, alpha * acc_prev + pv


def _online_softmax_step(chunks, v_t, m_ref, l_ref, acc_ref):
    m_ref[...], l_ref[...], acc_ref[...] = _softmax_update(
        chunks, v_t, m_ref[...], l_ref[...], acc_ref[...])


def _softmax_finish(l_ref, acc_ref):
    return acc_ref[...] / jnp.sum(l_ref[...], axis=1, keepdims=True)


def _chunk_end(pos):
    return ((pos >> CHUNK_SHIFT) + 1) << CHUNK_SHIFT


def _split_lane_halves(tile_f32, lane):
    lo = jnp.where(lane < LANES // 2, tile_f32, 0.0)
    hi = jnp.where(lane >= LANES // 2, tile_f32, 0.0)
    return jnp.concatenate([lo, hi], axis=0).astype(BF16)


def _dsa_kernel(q_ref, k_ref, v_ref, iq_ref, ik_ref, iw_ref, o_ref,
                key_ref, iqm_ref, wb_ref, mx2_ref, mx_ref, thr_ref, hi_ref, need_ref, cnt_ref,
                e1_ref, e2_ref, m_ref, l_ref, acc_ref,
                *, tq, topk, att_ratio):
    tk = tq
    nch = tk // LANES
    tki = 512
    rb = 128
    j = pl.program_id(1)
    q0 = j * tq
    n_tiles = j + 1
    lane = lax.broadcasted_iota(I32, (tq, LANES), 1)
    lane_rb = lax.broadcasted_iota(I32, (rb, LANES), 1)
    row = lax.broadcasted_iota(I32, (tq, 1), 0)
    q_end = _chunk_end(q0 + row)

    nb = tq // rb
    sub8 = lax.broadcasted_iota(I32, (8, LANES), 0)
    lane8 = lax.broadcasted_iota(I32, (8, LANES), 1)
    n_adm = _chunk_end(q0 + sub8 * rb + lane8).astype(F32)
    k_eff = jnp.minimum(n_adm, float(topk))
    ones8 = jnp.ones((8, LANES), BF16)
    tri_r = lax.broadcasted_iota(I32, (LANES, 2 * LANES), 0)
    tri_c = lax.broadcasted_iota(I32, (LANES, 2 * LANES), 1)
    tri_ones = jnp.where((tri_c >= LANES) | (tri_r <= tri_c), 1.0, 0.0).astype(BF16)

    def rows_of(x8, r):
        x = jnp.broadcast_to(x8[r:r + 1, :], (rb, LANES))
        if x8.dtype == I32:
            return lax.bitcast_convert_type(jnp.transpose(lax.bitcast_convert_type(x, F32)), I32)
        return jnp.transpose(x)

    def lanes_of(col):
        out = jnp.zeros((8, LANES), col.dtype)
        for r in range(nb):
            x = jnp.broadcast_to(col[r * rb:(r + 1) * rb], (rb, LANES))
            if col.dtype == I32:
                x = lax.bitcast_convert_type(jnp.transpose(lax.bitcast_convert_type(x, F32)), I32)
            else:
                x = jnp.transpose(x)
            out = jnp.where(sub8 == r, x[0:8, :], out)
        return out

    iw = iw_ref[...]
    for h in range(IDX_HEADS):
        g = h % 4
        tile = iq_ref[:, (h // 4) * LANES:(h // 4 + 1) * LANES].astype(F32)
        keep = (lane >= g * IDX_DIM) & (lane < (g + 1) * IDX_DIM)
        iq_h = jnp.where(keep, tile, 0.0).astype(BF16)
        w_h = jnp.broadcast_to(iw[:, h:h + 1], (tq, LANES))
        for r in range(nb):
            dst = slice((r * IDX_HEADS + h) * rb, (r * IDX_HEADS + h + 1) * rb)
            iqm_ref[dst, :] = iq_h[r * rb:(r + 1) * rb]
            wb_ref[dst, :] = w_h[r * rb:(r + 1) * rb]

    mx2_ref[...] = jnp.full((tq, LANES), KEY_MIN, I32)
    mx_ref[...] = jnp.full((tq, LANES), KEY_MIN, I32)

    def index_tile(it, masked):
        r0 = pl.multiple_of(it * tki, tki)
        ik_t = ik_ref[pl.ds(r0, tki), :]
        for r in range(nb):
            rs = slice(r * rb, (r + 1) * rb)
            base = r * IDX_HEADS * rb
            d = lax.dot_general(iqm_ref[base:base + IDX_HEADS * rb, :], ik_t, _NT,
                                preferred_element_type=F32)
            for c in range(tki // LANES):
                a = jnp.zeros((rb, LANES), F32)
                for h in range(IDX_HEADS):
                    hs = slice(h * rb, (h + 1) * rb)
                    w_h = wb_ref[base + h * rb:base + (h + 1) * rb, :]
                    a = a + w_h * jnp.maximum(d[hs, c * LANES:(c + 1) * LANES], 0.0)
                bits = lax.bitcast_convert_type(a, I32)
                key = bits ^ ((bits >> 31) & 0x7FFFFFFF)
                kidx = it * tki + c * LANES + lane_rb
                key = jnp.where(key == 0, -1 - kidx, key)
                if masked:
                    key = jnp.where(kidx < q_end[rs], key, KEY_MIN)
                top1 = mx_ref[rs, :]
                mx2_ref[rs, :] = jnp.maximum(mx2_ref[rs, :], jnp.minimum(top1, key))
                mx_ref[rs, :] = jnp.maximum(top1, key)
                key_ref[it * (tki // LANES) + c, rs, :] = key

    def full_tile(it, carry):
        index_tile(it, False)
        return carry

    n_full = q0 // tki
    lax.fori_loop(0, n_full, full_tile, 0)
    for t in range(tq // tki):
        index_tile(n_full + t, True)

    def count_pass(per_block):
        cnt_ref[...] = jnp.zeros((tq, LANES), F32)

        def body(kt, carry):
            for r in range(nb):
                rs = slice(r * rb, (r + 1) * rb)
                pred = per_block(rs)
                acc = cnt_ref[rs, :]
                for c in range(nch):
                    hit = pred(key_ref[kt * nch + c, rs, :], kt * tk + c * LANES)
                    acc = acc + jnp.where(hit, 1.0, 0.0)
                cnt_ref[rs, :] = acc
            return carry

        lax.fori_loop(0, n_tiles, body, 0)
        c8 = lax.dot_general(ones8, cnt_ref[...].astype(BF16), _NT, preferred_element_type=F32)
        cnt = jnp.zeros((8, LANES), F32)
        for r in range(nb):
            cnt = jnp.where(sub8 == r, c8[:, r * rb:(r + 1) * rb], cnt)
        return cnt

    def set_rows(ref, x8):
        for r in range(nb):
            ref[r * rb:(r + 1) * rb, :] = rows_of(x8, r)

    def count_ge(t8):
        set_rows(thr_ref, t8)

        def per_block(rs):
            tb = thr_ref[rs, :]
            return lambda keys, kidx0: keys >= tb

        return count_pass(per_block)

    def flip(k):
        return k ^ ((k >> 31) & 0x7FFFFFFF)

    live = sub8 < nb
    many = 2.0 ** 30
    small = n_adm <= float(topk)
    lo0 = lanes_of(jnp.min(mx2_ref[...], axis=1, keepdims=True))
    lo0 = jnp.where(live, jnp.where(small, KEY_MIN + 1, lo0), 0)
    hi0 = jnp.where(live, lanes_of(jnp.max(mx_ref[...], axis=1, keepdims=True)), 0)
    c_lo0 = jnp.where(small, n_adm, many)
    c_hi0 = jnp.zeros((8, LANES), F32)
    value_steps = 26
    zero_lo = -(key_ref.shape[0] * LANES)

    def bis_cond(st):
        it, lo, hi, c_lo, c_above = st
        open_rows = jnp.where((c_lo > k_eff) & (lo < hi) & (c_lo - c_above > 2.0), 1.0, 0.0)
        return jnp.max(open_rows) > 0.0

    def bis_body(st):
        it, lo, hi, c_lo, c_above = st
        mid_k = (lo >> 1) + (hi >> 1) + ((lo | hi) & 1)
        v_lo = lax.bitcast_convert_type(flip(lo), F32)
        v_hi = lax.bitcast_convert_type(flip(hi), F32)
        mid_v = flip(lax.bitcast_convert_type(0.5 * v_lo + 0.5 * v_hi, I32))
        probe_lo = (lo < zero_lo) & (hi >= zero_lo)
        probe_hi = (lo >= zero_lo) & (lo < 1) & (hi >= 1)
        in_zero = (lo >= zero_lo) & (hi < 1)
        mid_v = jnp.where(probe_lo, zero_lo, jnp.where(probe_hi, 1, jnp.where(in_zero, mid_k, mid_v)))
        mid_v = jnp.minimum(jnp.maximum(mid_v, lo + 1), hi)
        mid = jnp.where(it < value_steps, mid_v, mid_k)
        active = (c_lo > k_eff) & (lo < hi)
        c_mid = count_ge(mid)
        go_up = active & (c_mid >= k_eff)
        go_dn = active & (c_mid < k_eff)
        lo = jnp.where(go_up, mid, lo)
        c_lo = jnp.where(go_up, c_mid, c_lo)
        hi = jnp.where(go_dn, mid - 1, hi)
        c_above = jnp.where(go_dn, c_mid, c_above)
        return it + 1, lo, hi, c_lo, c_above

    _, lo_f, hi_f, c_lo_f, c_above = lax.while_loop(
        bis_cond, bis_body, (jnp.int32(0), lo0, hi0, c_lo0, c_hi0))

    pending = live & (c_lo_f > k_eff) & (lo_f < hi_f)
    e1_ref[...] = lo_f
    e2_ref[...] = lo_f

    @pl.when(jnp.max(jnp.where(pending, 1.0, 0.0)) > 0.0)
    def _():
        set_rows(thr_ref, lo_f)
        set_rows(hi_ref, hi_f)
        mx_ref[...] = jnp.full((tq, LANES), KEY_MIN, I32)
        mx2_ref[...] = jnp.full((tq, LANES), KEY_MAX, I32)

        def body(kt, carry):
            for r in range(nb):
                rs = slice(r * rb, (r + 1) * rb)
                lo_b, hi_b = thr_ref[rs, :], hi_ref[rs, :]
                top, bot = mx_ref[rs, :], mx2_ref[rs, :]
                for c in range(nch):
                    keys = key_ref[kt * nch + c, rs, :]
                    cand = (keys >= lo_b) & (keys <= hi_b)
                    top = jnp.maximum(top, jnp.where(cand, keys, KEY_MIN))
                    bot = jnp.minimum(bot, jnp.where(cand, keys, KEY_MAX))
                mx_ref[rs, :] = top
                mx2_ref[rs, :] = bot
            return carry

        lax.fori_loop(0, n_tiles, body, 0)
        e1_ref[...] = lanes_of(jnp.max(mx_ref[...], axis=1, keepdims=True))
        e2_ref[...] = lanes_of(jnp.min(mx2_ref[...], axis=1, keepdims=True))

    e1, e2 = e1_ref[...], e2_ref[...]
    thr = jnp.where(pending, e1, lo_f)
    c_thr = jnp.where(pending, c_above + jnp.where(e1 == e2, 2.0, 1.0), c_lo_f)
    set_rows(thr_ref, thr)
    need = k_eff - c_above
    has_tie = live & (c_thr > k_eff)
    any_tie = jnp.max(jnp.where(has_tie, 1.0, 0.0)) > 0.0

    def bias_pass(with_ties):
        def bias_tile(kt, carry):
            for c in range(nch):
                keys = key_ref[kt * nch + c]
                thr_b = thr_ref[...]
                tie = keys == thr_b
                if with_ties:
                    hits = jnp.dot(jnp.where(tie, 1.0, 0.0).astype(BF16), tri_ones,
                                   preferred_element_type=F32)
                    seen = cnt_ref[...]
                    tie = tie & (hits[:, :LANES] + seen <= need_ref[...])
                    cnt_ref[...] = seen + hits[:, LANES:]
                key_ref[kt * nch + c] = jnp.where((keys > thr_b) | tie, 0, NEG_BITS)
            return carry

        lax.fori_loop(0, n_tiles, bias_tile, 0)

    @pl.when(any_tie)
    def _():
        set_rows(need_ref, jnp.where(has_tie, need, 2.0 ** 30))
        cnt_ref[...] = jnp.zeros((tq, LANES), F32)
        bias_pass(True)

    @pl.when(jnp.logical_not(any_tie))
    def _():
        bias_pass(False)

    n_wide = n_tiles >> (att_ratio.bit_length() - 1)

    lane2 = lax.broadcasted_iota(I32, (tq, 2 * LANES), 1)
    hd = A_HEAD_DIM
    for hq in range(A_HEADS // 4):
        ls = slice(hq * 2 * LANES, (hq + 1) * 2 * LANES)
        q2 = q_ref[:, ls].astype(F32)
        q_st = jnp.concatenate(
            [jnp.where((lane2 >= g * hd) & (lane2 < (g + 1) * hd), q2, 0.0) for g in range(4)],
            axis=0).astype(BF16)
        m_ref[...] = jnp.full(m_ref.shape, NEG, F32)
        l_ref[...] = jnp.zeros(l_ref.shape, F32)
        acc_ref[...] = jnp.zeros(acc_ref.shape, F32)

        def att_step(kt, width):
            r0 = pl.multiple_of(kt * (width * tk), width * tk)
            s = lax.dot_general(q_st, k_ref[pl.ds(r0, width * tk), ls], _NT,
                                preferred_element_type=F32)
            chunks = []
            for c in range(width * nch):
                bias = lax.bitcast_convert_type(key_ref[kt * (width * nch) + c], F32)
                sc = s[:, c * LANES:(c + 1) * LANES]
                chunks.append(jnp.concatenate(
                    [sc[g * tq:(g + 1) * tq] + bias for g in range(4)], axis=0))
            own = lambda pv: jnp.concatenate([pv[:2 * tq, :LANES], pv[2 * tq:, LANES:]], axis=0)
            m_ref[...], l_ref[...], acc_ref[...] = _softmax_update(
                chunks, v_ref[pl.ds(r0, width * tk), ls], m_ref[...], l_ref[...], acc_ref[...],
                own)

        def wide_step(kt, carry):
            att_step(kt, att_ratio)
            return carry

        def single_step(kt, carry):
            att_step(kt, 1)
            return carry

        lax.fori_loop(0, n_wide, wide_step, 0)
        lax.fori_loop(n_wide * att_ratio, n_tiles, single_step, 0)
        o = _softmax_finish(l_ref, acc_ref)
        for pr in range(2):
            lo_h, hi_h = o[2 * pr * tq:(2 * pr + 1) * tq], o[(2 * pr + 1) * tq:(2 * pr + 2) * tq]
            o_ref[:, (2 * hq + pr) * LANES:(2 * hq + pr + 1) * LANES] = jnp.where(
                lane < LANES // 2, lo_h, hi_h).astype(o_ref.dtype)


def _dsa_call(qa, ka, va, iq, ik, iw, tq):
    b, s, _ = qa.shape
    topk = min(TOPK_MAX, s // 4)
    qrow = lambda width: pl.BlockSpec((None, tq, width), lambda bi, i: (bi, i, 0))
    seq = lambda width: pl.BlockSpec((None, s, width), lambda bi, i: (bi, 0, 0),
                                     pipeline_mode=pl.Buffered(1))
    att_ratio = 2 if s % (2 * tq) == 0 else 1
    assert s % tq == 0 and tq % LANES == 0 and tq // LANES <= 8 and s // LANES <= 256
    return pl.pallas_call(
        functools.partial(_dsa_kernel, tq=tq, topk=topk, att_ratio=att_ratio),
        grid=(b, s // tq),
        in_specs=[qrow(D_A), seq(D_A), seq(D_A), qrow(IDX_HEADS * IDX_DIM), seq(LANES), qrow(LANES)],
        out_specs=qrow(D_A),
        out_shape=jax.ShapeDtypeStruct((b, s, D_A), BF16),
        scratch_shapes=[
            pltpu.VMEM((s // LANES, tq, LANES), I32),
            pltpu.VMEM((IDX_HEADS * tq, LANES), BF16),
            pltpu.VMEM((IDX_HEADS * tq, LANES), F32),
            pltpu.VMEM((tq, LANES), I32),
            pltpu.VMEM((tq, LANES), I32),
            pltpu.VMEM((tq, LANES), I32),
            pltpu.VMEM((tq, LANES), I32),
            pltpu.VMEM((tq, LANES), F32),
            pltpu.VMEM((tq, LANES), F32),
            pltpu.VMEM((8, LANES), I32),
            pltpu.VMEM((8, LANES), I32),
            pltpu.VMEM((4 * tq, LANES), F32),
            pltpu.VMEM((4 * tq, LANES), F32),
            pltpu.VMEM((4 * tq, LANES), F32),
        ],
        compiler_params=pltpu.CompilerParams(
            dimension_semantics=("arbitrary", "arbitrary"), vmem_limit_bytes=VMEM_LIMIT),
        name="dsa",
    )(qa, ka, va, iq, ik, iw)


def _diff_kernel(q_ref, k_ref, v_ref, lam_ref, g_ref, o_ref, m_ref, l_ref, acc_ref,
                 *, tq, tk, lam_init):
    j = pl.program_id(2)
    q0 = j * tq
    lane = lax.broadcasted_iota(I32, (tq, LANES), 1)
    row = lax.broadcasted_iota(I32, (tq, 1), 0)
    q_end = _chunk_end(q0 + row)

    q_st = _split_lane_halves(q_ref[...].astype(F32), lane)
    m_ref[...] = jnp.full(m_ref.shape, NEG, F32)
    l_ref[...] = jnp.zeros(l_ref.shape, F32)
    acc_ref[...] = jnp.zeros(acc_ref.shape, F32)

    def logits(kt):
        r0 = pl.multiple_of(kt * tk, tk)
        return lax.dot_general(q_st, k_ref[pl.ds(r0, tk), :], _NT, preferred_element_type=F32)

    def att_tile(kt, masked):
        s = logits(kt)
        chunks = []
        for c in range(tk // LANES):
            sc = s[:, c * LANES:(c + 1) * LANES]
            if masked:
                adm = (kt * tk + c * LANES + lane) < q_end
                sc = jnp.concatenate(
                    [jnp.where(adm, sc[:tq], NEG), jnp.where(adm, sc[tq:], NEG)], axis=0)
            chunks.append(sc)
        r0 = pl.multiple_of(kt * tk, tk)
        _online_softmax_step(chunks, v_ref[pl.ds(r0, tk), :], m_ref, l_ref, acc_ref)

    def full_tile(kt, carry):
        att_tile(kt, False)
        return carry

    def diag_step(key0, width, r_lo):
        rowsets = (slice(r_lo, tq), slice(tq + r_lo, 2 * tq))
        nrow = tq - r_lo
        k0 = pl.multiple_of(q0 + key0, width)
        q_sub = jnp.concatenate([q_st[rs] for rs in rowsets], axis=0)
        s = lax.dot_general(q_sub, k_ref[pl.ds(k0, width), :], _NT, preferred_element_type=F32)
        chunks = []
        for c in range(width // LANES):
            sc = s[:, c * LANES:(c + 1) * LANES]
            lane_sub = lax.broadcasted_iota(I32, (nrow, LANES), 1)
            adm = (q0 + key0 + c * LANES + lane_sub) < q_end[r_lo:]
            chunks.append(jnp.concatenate(
                [jnp.where(adm, sc[:nrow], NEG), jnp.where(adm, sc[nrow:], NEG)], axis=0))
        gather = lambda ref: jnp.concatenate([ref[rs, :] for rs in rowsets], axis=0)
        m_new, l_new, acc_new = _softmax_update(
            chunks, v_ref[pl.ds(k0, width), :], gather(m_ref), gather(l_ref), gather(acc_ref))
        for ref, new in ((m_ref, m_new), (l_ref, l_new), (acc_ref, acc_new)):
            ref[rowsets[0], :] = new[:nrow]
            ref[rowsets[1], :] = new[nrow:]

    n_full = j * (tq // tk)
    lax.fori_loop(0, n_full, full_tile, 0)
    if tq == tk:
        diag_step(0, tq // 2, 0)
        diag_step(tq // 2, tq // 2, tq // 2)
    else:
        for t in range(tq // tk):
            att_tile(n_full + t, True)

    lam_p = lam_ref[...]
    lam = (jnp.exp(jnp.sum(lam_p[0:1] * lam_p[1:2], axis=1, keepdims=True))
           - jnp.exp(jnp.sum(lam_p[2:3] * lam_p[3:4], axis=1, keepdims=True)) + lam_init)
    o = _softmax_finish(l_ref, acc_ref)
    o = o[:tq] - lam * o[tq:]
    o_ref[...] = (_rms(o, g_ref[...]) * (1.0 - lam_init)).astype(o_ref.dtype)


def _diff_call(qb, kb, vb, lam_p, g_subln, lam_init, tq, tk):
    b, s, _ = qb.shape
    return pl.pallas_call(
        functools.partial(_diff_kernel, tq=tq, tk=tk, lam_init=lam_init),
        grid=(b, B_HEADS, s // tq),
        in_specs=[
            pl.BlockSpec((None, tq, LANES), lambda bi, h, i: (bi, i, h)),
            pl.BlockSpec((None, s, LANES), lambda bi, h, i: (bi, 0, h)),
            pl.BlockSpec((None, s, LANES), lambda bi, h, i: (bi, 0, h)),
            pl.BlockSpec((4, B_HEAD_DIM), lambda bi, h, i: (0, 0)),
            pl.BlockSpec((1, LANES), lambda bi, h, i: (0, 0)),
        ],
        out_specs=pl.BlockSpec((None, tq, LANES), lambda bi, h, i: (bi, i, h)),
        out_shape=jax.ShapeDtypeStruct((b, s, D_B), BF16),
        scratch_shapes=[
            pltpu.VMEM((2 * tq, LANES), F32),
            pltpu.VMEM((2 * tq, LANES), F32),
            pltpu.VMEM((2 * tq, LANES), F32),
        ],
        compiler_params=pltpu.CompilerParams(
            dimension_semantics=("arbitrary", "arbitrary", "arbitrary"),
            vmem_limit_bytes=VMEM_LIMIT),
        name="diff",
    )(qb, kb, vb, lam_p, g_subln)


def _ffn_kernel(x_ref, a_ref, b_ref, wo_ref, g_ref, wg_ref, wu_ref, wd_ref, gf_ref, o_ref,
                *, ff_chunk, final_norm):
    d_a = a_ref.shape[1]
    x1 = (x_ref[...]
          + jnp.dot(a_ref[...], wo_ref[:d_a, :], preferred_element_type=F32)
          + jnp.dot(b_ref[...], wo_ref[d_a:, :], preferred_element_type=F32))
    h2 = _rms(x1, g_ref[...]).astype(BF16)
    y = None
    for c in range(wg_ref.shape[1] // ff_chunk):
        cs = slice(c * ff_chunk, (c + 1) * ff_chunk)
        gate = jnp.dot(h2, wg_ref[:, cs], preferred_element_type=F32)
        up = jnp.dot(h2, wu_ref[:, cs], preferred_element_type=F32)
        act = (gate / (1.0 + jnp.exp(-gate)) * up).astype(BF16)
        down = jnp.dot(act, wd_ref[cs, :], preferred_element_type=F32)
        y = down if y is None else y + down
    x2 = x1 + y
    if final_norm:
        x2 = _rms(x2, gf_ref[...])
    o_ref[...] = x2


def _ffn_call(x, out_a, out_b, wo, g, wg, wu, wd, g_final, final_norm, tm, ff_chunk):
    t, d = x.shape
    d_ff = wg.shape[1]
    row = lambda width: pl.BlockSpec((tm, width), lambda i: (i, 0))
    const = lambda shape: pl.BlockSpec(shape, lambda i: (0, 0), pipeline_mode=pl.Buffered(1))
    return pl.pallas_call(
        functools.partial(_ffn_kernel, ff_chunk=ff_chunk, final_norm=final_norm),
        grid=(t // tm,),
        in_specs=[row(d), row(out_a.shape[1]), row(out_b.shape[1]), const(wo.shape), const((1, d)),
                  const((d, d_ff)), const((d, d_ff)), const((d_ff, d)), const((1, d))],
        out_specs=row(d),
        out_shape=jax.ShapeDtypeStruct((t, d), F32),
        compiler_params=pltpu.CompilerParams(
            dimension_semantics=("arbitrary",), vmem_limit_bytes=VMEM_LIMIT),
        name="ffn",
    )(x, out_a, out_b, wo, g, wg, wu, wd, g_final)


def _rope_tables(seq_len, dim):
    pos = jnp.arange(seq_len, dtype=F32)
    inv = ROPE_THETA ** (-jnp.arange(0, dim, 2, dtype=F32) / dim)
    ang = pos[:, None] * inv[None, :]
    cos, sin = jnp.cos(ang), jnp.sin(ang)
    reps = LANES // dim
    return (jnp.tile(jnp.concatenate([cos, cos], axis=1), (1, reps)),
            jnp.tile(jnp.concatenate([-sin, sin], axis=1), (1, reps)))


def _pack_w_in(w):
    splits = (D_A, D_A, D_A, IDX_HEADS * IDX_DIM, IDX_DIM, IDX_HEADS, D_B, D_B, D_B)
    offs = [0]
    for sz in splits:
        offs.append(offs[-1] + sz)
    qa, ka, va, iq, ik, iw, qb, kb, vb = (w[:, offs[i]:offs[i + 1]] for i in range(len(splits)))
    ik4 = jnp.tile(ik, (1, LANES // IDX_DIM))
    iw_p = jnp.pad(iw, ((0, 0), (0, LANES - IDX_HEADS)))
    return jnp.concatenate([qa, ka, va, iq, ik4, iw_p, qb, kb, vb], axis=1).astype(BF16)


def kernel(x, w_in, w_out, g_mix, lam_q1, lam_k1, lam_q2, lam_k2, g_subln, g_ffn, w_gate, w_up,
           w_down, g_final):
    b, s, d = x.shape
    depth = w_in.shape[0]
    tm_proj, tq, tm_ffn = 512, 512, 512
    tq_diff = 2 * tq if s % (2 * tq) == 0 else tq
    d_ff = w_gate.shape[-1]
    ff_chunk = d_ff // 2
    assert s % tm_proj == 0 and s % tq == 0 and (b * s) % tm_ffn == 0 and ff_chunk % LANES == 0

    tabs = _rope_tables(s, A_HEAD_DIM) + _rope_tables(s, IDX_DIM)
    for layer in range(depth):
        qa, ka, va, iq, ik, iw, qb, kb, vb = _proj_call(
            x, g_mix[layer][None, :], _pack_w_in(w_in[layer]), tabs, tm_proj)
        out_a = _dsa_call(qa, ka, va, iq, ik, iw, tq)
        lam_init = 0.8 - 0.6 * math.exp(-0.3 * layer)
        lam_p = jnp.stack([lam_q1[layer], lam_k1[layer], lam_q2[layer], lam_k2[layer]]).astype(F32)
        out_b = _diff_call(qb, kb, vb, lam_p, g_subln[layer][None, :], lam_init, tq_diff, tq_diff)
        x = _ffn_call(
            x.reshape(b * s, d), out_a.reshape(b * s, D_A), out_b.reshape(b * s, D_B),
            w_out[layer].astype(BF16), g_ffn[layer][None, :], w_gate[layer].astype(BF16),
            w_up[layer].astype(BF16), w_down[layer].astype(BF16), g_final[None, :],
            layer == depth - 1, tm_ffn, ff_chunk).reshape(b, s, d)
    return x
```

```python
import functools
import math

import jax
import jax.numpy as jnp
import numpy as np
from jax import lax
from jax.experimental import pallas as pl
from jax.experimental.pallas import tpu as pltpu

F32 = jnp.float32
BF16 = jnp.bfloat16
I32 = jnp.int32

CHUNK = 64
CHUNK_SHIFT = CHUNK.bit_length() - 1
ROPE_THETA = 10000.0
RMS_EPS = 1e-6
A_HEADS = 8
A_HEAD_DIM = 64
D_A = A_HEADS * A_HEAD_DIM
IDX_HEADS = 8
IDX_DIM = 32
TOPK_MAX = 256
B_HEADS = 4
B_HEAD_DIM = 64
D_B = B_HEADS * 2 * B_HEAD_DIM

LANES = 128
VMEM_LIMIT = 56 * 1024 * 1024

LOG2E = 1.4426950408889634
NEG = -1e30
NEG_BITS = int(np.float32(NEG).view(np.int32))
KEY_MIN = -(2 ** 31)
KEY_MAX = 2 ** 31 - 1

_OFF_QA, _OFF_KA, _OFF_VA = 0, 512, 1024
_OFF_IQ, _OFF_IK, _OFF_IW = 1536, 1792, 1920
_OFF_QB, _OFF_KB, _OFF_VB = 2048, 2560, 3072
_W_CAT = 3584

_NT = (((1,), (1,)), ((), ()))


def _rms(x, g):
    r = lax.rsqrt(jnp.mean(x * x, axis=-1, keepdims=True) + RMS_EPS)
    return x * r * g


def _proj_kernel(x_ref, g_ref, w_ref, cos_a_ref, sin_a_ref, cos_i_ref, sin_i_ref,
                 qa_ref, ka_ref, va_ref, iq_ref, ik_ref, iw_ref, qb_ref, kb_ref, vb_ref):
    h = _rms(x_ref[...], g_ref[...]).astype(BF16)
    tm = h.shape[0]
    lane = lax.broadcasted_iota(I32, (tm, LANES), 1)

    def proj(off, width):
        return jnp.dot(h, w_ref[:, off:off + width], preferred_element_type=F32)

    def rope_tile(y, cos, sin_signed, half):
        first = (lane & (2 * half - 1)) < half
        rot = jnp.where(first, pltpu.roll(y, LANES - half, 1), pltpu.roll(y, half, 1))
        return y * cos + rot * sin_signed

    def rope_store(out_ref, off, width, cos_ref, sin_ref, half, scale):
        y = proj(off, width)
        cos, sin = cos_ref[...], sin_ref[...]
        for t in range(width // LANES):
            r = rope_tile(y[:, t * LANES:(t + 1) * LANES], cos, sin, half)
            if scale != 1.0:
                r = r * scale
            out_ref[:, t * LANES:(t + 1) * LANES] = r.astype(out_ref.dtype)

    half_a, half_i = A_HEAD_DIM // 2, IDX_DIM // 2
    rope_store(qa_ref, _OFF_QA, D_A, cos_a_ref, sin_a_ref, half_a, A_HEAD_DIM ** -0.5 * LOG2E)
    rope_store(ka_ref, _OFF_KA, D_A, cos_a_ref, sin_a_ref, half_a, 1.0)
    va_ref[...] = proj(_OFF_VA, D_A).astype(va_ref.dtype)
    rope_store(iq_ref, _OFF_IQ, IDX_HEADS * IDX_DIM, cos_i_ref, sin_i_ref, half_i, 1.0)
    rope_store(ik_ref, _OFF_IK, LANES, cos_i_ref, sin_i_ref, half_i, 1.0)
    iw_ref[...] = proj(_OFF_IW, LANES) * (IDX_HEADS ** -0.5 * IDX_DIM ** -0.5)
    rope_store(qb_ref, _OFF_QB, D_B, cos_a_ref, sin_a_ref, half_a, B_HEAD_DIM ** -0.5 * LOG2E)
    rope_store(kb_ref, _OFF_KB, D_B, cos_a_ref, sin_a_ref, half_a, 1.0)
    vb_ref[...] = proj(_OFF_VB, D_B).astype(vb_ref.dtype)


def _proj_call(x, g, w_cat, tabs, tm):
    b, s, d = x.shape
    cos_a, sin_a, cos_i, sin_i = tabs
    row = lambda width: pl.BlockSpec((None, tm, width), lambda bi, i: (bi, i, 0))
    tab = pl.BlockSpec((tm, LANES), lambda bi, i: (i, 0))
    const = lambda shape: pl.BlockSpec(shape, lambda bi, i: (0, 0), pipeline_mode=pl.Buffered(1))
    widths = (D_A, D_A, D_A, IDX_HEADS * IDX_DIM, LANES, LANES, D_B, D_B, D_B)
    dtypes = (BF16, BF16, BF16, BF16, BF16, F32, BF16, BF16, BF16)
    return pl.pallas_call(
        _proj_kernel,
        grid=(b, s // tm),
        in_specs=[row(d), const((1, d)), const((d, _W_CAT)), tab, tab, tab, tab],
        out_specs=[row(w) for w in widths],
        out_shape=[jax.ShapeDtypeStruct((b, s, w), dt) for w, dt in zip(widths, dtypes)],
        compiler_params=pltpu.CompilerParams(
            dimension_semantics=("arbitrary", "arbitrary"), vmem_limit_bytes=VMEM_LIMIT),
        name="proj",
    )(x, g, w_cat, cos_a, sin_a, cos_i, sin_i)


def _softmax_update(chunks, v_t, m_prev, l_prev, acc_prev, own=None):
    m_cur = functools.reduce(jnp.maximum, chunks)
    m_new = jnp.maximum(m_prev, jnp.max(m_cur, axis=1, keepdims=True))
    alpha = jnp.exp2(m_prev - m_new)
    p = [jnp.exp2(c - m_new) for c in chunks]
    l_new = alpha * l_prev + functools.reduce(jnp.add, p)
    pv = jnp.dot(jnp.concatenate(p, axis=1).astype(BF16), v_t, preferred_element_type=F32)
    if own is not None:
        pv = own(pv)
    return m_new, l_new, alpha * acc_prev + pv


def _online_softmax_step(chunks, v_t, m_ref, l_ref, acc_ref):
    m_ref[...], l_ref[...], acc_ref[...] = _softmax_update(
        chunks, v_t, m_ref[...], l_ref[...], acc_ref[...])


def _softmax_finish(l_ref, acc_ref):
    return acc_ref[...] / jnp.sum(l_ref[...], axis=1, keepdims=True)


def _chunk_end(pos):
    return ((pos >> CHUNK_SHIFT) + 1) << CHUNK_SHIFT


def _split_lane_halves(tile_f32, lane):
    lo = jnp.where(lane < LANES // 2, tile_f32, 0.0)
    hi = jnp.where(lane >= LANES // 2, tile_f32, 0.0)
    return jnp.concatenate([lo, hi], axis=0).astype(BF16)


def _dsa_kernel(q_ref, k_ref, v_ref, iq_ref, ik_ref, iw_ref, o_ref,
                key_ref, iqm_ref, wb_ref, mx2_ref, mx_ref, thr_ref, hi_ref, need_ref, cnt_ref,
                e1_ref, e2_ref, m_ref, l_ref, acc_ref,
                *, tq, topk, att_ratio):
    tk = tq
    nch = tk // LANES
    tki = 512
    rb = 128
    j = pl.program_id(1)
    q0 = j * tq
    n_tiles = j + 1
    lane = lax.broadcasted_iota(I32, (tq, LANES), 1)
    lane_rb = lax.broadcasted_iota(I32, (rb, LANES), 1)
    row = lax.broadcasted_iota(I32, (tq, 1), 0)
    q_end = _chunk_end(q0 + row)

    nb = tq // rb
    sub8 = lax.broadcasted_iota(I32, (8, LANES), 0)
    lane8 = lax.broadcasted_iota(I32, (8, LANES), 1)
    n_adm = _chunk_end(q0 + sub8 * rb + lane8).astype(F32)
    k_eff = jnp.minimum(n_adm, float(topk))
    ones8 = jnp.ones((8, LANES), BF16)
    tri_r = lax.broadcasted_iota(I32, (LANES, 2 * LANES), 0)
    tri_c = lax.broadcasted_iota(I32, (LANES, 2 * LANES), 1)
    tri_ones = jnp.where((tri_c >= LANES) | (tri_r <= tri_c), 1.0, 0.0).astype(BF16)

    def rows_of(x8, r):
        x = jnp.broadcast_to(x8[r:r + 1, :], (rb, LANES))
        if x8.dtype == I32:
            return lax.bitcast_convert_type(jnp.transpose(lax.bitcast_convert_type(x, F32)), I32)
        return jnp.transpose(x)

    def lanes_of(col):
        out = jnp.zeros((8, LANES), col.dtype)
        for r in range(nb):
            x = jnp.broadcast_to(col[r * rb:(r + 1) * rb], (rb, LANES))
            if col.dtype == I32:
                x = lax.bitcast_convert_type(jnp.transpose(lax.bitcast_convert_type(x, F32)), I32)
            else:
                x = jnp.transpose(x)
            out = jnp.where(sub8 == r, x[0:8, :], out)
        return out

    iw = iw_ref[...]
    for h in range(IDX_HEADS):
        g = h % 4
        tile = iq_ref[:, (h // 4) * LANES:(h // 4 + 1) * LANES].astype(F32)
        keep = (lane >= g * IDX_DIM) & (lane < (g + 1) * IDX_DIM)
        iq_h = jnp.where(keep, tile, 0.0).astype(BF16)
        w_h = jnp.broadcast_to(iw[:, h:h + 1], (tq, LANES))
        for r in range(nb):
            dst = slice((r * IDX_HEADS + h) * rb, (r * IDX_HEADS + h + 1) * rb)
            iqm_ref[dst, :] = iq_h[r * rb:(r + 1) * rb]
            wb_ref[dst, :] = w_h[r * rb:(r + 1) * rb]

    mx2_ref[...] = jnp.full((tq, LANES), KEY_MIN, I32)
    mx_ref[...] = jnp.full((tq, LANES), KEY_MIN, I32)

    def index_tile(it, masked):
        r0 = pl.multiple_of(it * tki, tki)
        ik_t = ik_ref[pl.ds(r0, tki), :]
        for r in range(nb):
            rs = slice(r * rb, (r + 1) * rb)
            base = r * IDX_HEADS * rb
            d = lax.dot_general(iqm_ref[base:base + IDX_HEADS * rb, :], ik_t, _NT,
                                preferred_element_type=F32)
            for c in range(tki // LANES):
                a = jnp.zeros((rb, LANES), F32)
                for h in range(IDX_HEADS):
                    hs = slice(h * rb, (h + 1) * rb)
                    w_h = wb_ref[base + h * rb:base + (h + 1) * rb, :]
                    a = a + w_h * jnp.maximum(d[hs, c * LANES:(c + 1) * LANES], 0.0)
                bits = lax.bitcast_convert_type(a, I32)
                key = bits ^ ((bits >> 31) & 0x7FFFFFFF)
                kidx = it * tki + c * LANES + lane_rb
                key = jnp.where(key == 0, -1 - kidx, key)
                if masked:
                    key = jnp.where(kidx < q_end[rs], key, KEY_MIN)
                top1 = mx_ref[rs, :]
                mx2_ref[rs, :] = jnp.maximum(mx2_ref[rs, :], jnp.minimum(top1, key))
                mx_ref[rs, :] = jnp.maximum(top1, key)
                key_ref[it * (tki // LANES) + c, rs, :] = key

    def full_tile(it, carry):
        index_tile(it, False)
        return carry

    n_full = q0 // tki
    lax.fori_loop(0, n_full, full_tile, 0)
    for t in range(tq // tki):
        index_tile(n_full + t, True)

    def count_pass(per_block):
        cnt_ref[...] = jnp.zeros((tq, LANES), F32)

        def body(kt, carry):
            for r in range(nb):
                rs = slice(r * rb, (r + 1) * rb)
                pred = per_block(rs)
                acc = cnt_ref[rs, :]
                for c in range(nch):
                    hit = pred(key_ref[kt * nch + c, rs, :], kt * tk + c * LANES)
                    acc = acc + jnp.where(hit, 1.0, 0.0)
                cnt_ref[rs, :] = acc
            return carry

        lax.fori_loop(0, n_tiles, body, 0)
        c8 = lax.dot_general(ones8, cnt_ref[...].astype(BF16), _NT, preferred_element_type=F32)
        cnt = jnp.zeros((8, LANES), F32)
        for r in range(nb):
            cnt = jnp.where(sub8 == r, c8[:, r * rb:(r + 1) * rb], cnt)
        return cnt

    def set_rows(ref, x8):
        for r in range(nb):
            ref[r * rb:(r + 1) * rb, :] = rows_of(x8, r)

    def count_ge(t8):
        set_rows(thr_ref, t8)

        def per_block(rs):
            tb = thr_ref[rs, :]
            return lambda keys, kidx0: keys >= tb

        return count_pass(per_block)

    def flip(k):
        return k ^ ((k >> 31) & 0x7FFFFFFF)

    live = sub8 < nb
    many = 2.0 ** 30
    small = n_adm <= float(topk)
    lo0 = lanes_of(jnp.min(mx2_ref[...], axis=1, keepdims=True))
    lo0 = jnp.where(live, jnp.where(small, KEY_MIN + 1, lo0), 0)
    hi0 = jnp.where(live, lanes_of(jnp.max(mx_ref[...], axis=1, keepdims=True)), 0)
    c_lo0 = jnp.where(small, n_adm, many)
    c_hi0 = jnp.zeros((8, LANES), F32)
    value_steps = 26
    zero_lo = -(key_ref.shape[0] * LANES)

    def bis_cond(st):
        it, lo, hi, c_lo, c_above = st
        open_rows = jnp.where((c_lo > k_eff) & (lo < hi) & (c_lo - c_above > 2.0), 1.0, 0.0)
        return jnp.max(open_rows) > 0.0

    def bis_body(st):
        it, lo, hi, c_lo, c_above = st
        mid_k = (lo >> 1) + (hi >> 1) + ((lo | hi) & 1)
        v_lo = lax.bitcast_convert_type(flip(lo), F32)
        v_hi = lax.bitcast_convert_type(flip(hi), F32)
        mid_v = flip(lax.bitcast_convert_type(0.5 * v_lo + 0.5 * v_hi, I32))
        probe_lo = (lo < zero_lo) & (hi >= zero_lo)
        probe_hi = (lo >= zero_lo) & (lo < 1) & (hi >= 1)
        in_zero = (lo >= zero_lo) & (hi < 1)
        mid_v = jnp.where(probe_lo, zero_lo, jnp.where(probe_hi, 1, jnp.where(in_zero, mid_k, mid_v)))
        mid_v = jnp.minimum(jnp.maximum(mid_v, lo + 1), hi)
        mid = jnp.where(it < value_steps, mid_v, mid_k)
        active = (c_lo > k_eff) & (lo < hi)
        c_mid = count_ge(mid)
        go_up = active & (c_mid >= k_eff)
        go_dn = active & (c_mid < k_eff)
        lo = jnp.where(go_up, mid, lo)
        c_lo = jnp.where(go_up, c_mid, c_lo)
        hi = jnp.where(go_dn, mid - 1, hi)
        c_above = jnp.where(go_dn, c_mid, c_above)
        return it + 1, lo, hi, c_lo, c_above

    _, lo_f, hi_f, c_lo_f, c_above = lax.while_loop(
        bis_cond, bis_body, (jnp.int32(0), lo0, hi0, c_lo0, c_hi0))

    pending = live & (c_lo_f > k_eff) & (lo_f < hi_f)
    e1_ref[...] = lo_f
    e2_ref[...] = lo_f

    @pl.when(jnp.max(jnp.where(pending, 1.0, 0.0)) > 0.0)
    def _():
        set_rows(thr_ref, lo_f)
        set_rows(hi_ref, hi_f)
        mx_ref[...] = jnp.full((tq, LANES), KEY_MIN, I32)
        mx2_ref[...] = jnp.full((tq, LANES), KEY_MAX, I32)

        def body(kt, carry):
            for r in range(nb):
                rs = slice(r * rb, (r + 1) * rb)
                lo_b, hi_b = thr_ref[rs, :], hi_ref[rs, :]
                top, bot = mx_ref[rs, :], mx2_ref[rs, :]
                for c in range(nch):
                    keys = key_ref[kt * nch + c, rs, :]
                    cand = (keys >= lo_b) & (keys <= hi_b)
                    top = jnp.maximum(top, jnp.where(cand, keys, KEY_MIN))
                    bot = jnp.minimum(bot, jnp.where(cand, keys, KEY_MAX))
                mx_ref[rs, :] = top
                mx2_ref[rs, :] = bot
            return carry

        lax.fori_loop(0, n_tiles, body, 0)
        e1_ref[...] = lanes_of(jnp.max(mx_ref[...], axis=1, keepdims=True))
        e2_ref[...] = lanes_of(jnp.min(mx2_ref[...], axis=1, keepdims=True))

    e1, e2 = e1_ref[...], e2_ref[...]
    thr = jnp.where(pending, e1, lo_f)
    c_thr = jnp.where(pending, c_above + jnp.where(e1 == e2, 2.0, 1.0), c_lo_f)
    set_rows(thr_ref, thr)
    need = k_eff - c_above
    has_tie = live & (c_thr > k_eff)
    any_tie = jnp.max(jnp.where(has_tie, 1.0, 0.0)) > 0.0

    def bias_pass(with_ties):
        def bias_tile(kt, carry):
            for c in range(nch):
                keys = key_ref[kt * nch + c]
                thr_b = thr_ref[...]
                tie = keys == thr_b
                if with_ties:
                    hits = jnp.dot(jnp.where(tie, 1.0, 0.0).astype(BF16), tri_ones,
                                   preferred_element_type=F32)
                    seen = cnt_ref[...]
                    tie = tie & (hits[:, :LANES] + seen <= need_ref[...])
                    cnt_ref[...] = seen + hits[:, LANES:]
                key_ref[kt * nch + c] = jnp.where((keys > thr_b) | tie, 0, NEG_BITS)
            return carry

        lax.fori_loop(0, n_tiles, bias_tile, 0)

    @pl.when(any_tie)
    def _():
        set_rows(need_ref, jnp.where(has_tie, need, 2.0 ** 30))
        cnt_ref[...] = jnp.zeros((tq, LANES), F32)
        bias_pass(True)

    @pl.when(jnp.logical_not(any_tie))
    def _():
        bias_pass(False)

    n_wide = n_tiles >> (att_ratio.bit_length() - 1)

    lane2 = lax.broadcasted_iota(I32, (tq, 2 * LANES), 1)
    hd = A_HEAD_DIM
    for hq in range(A_HEADS // 4):
        ls = slice(hq * 2 * LANES, (hq + 1) * 2 * LANES)
        q2 = q_ref[:, ls].astype(F32)
        q_st = jnp.concatenate(
            [jnp.where((lane2 >= g * hd) & (lane2 < (g + 1) * hd), q2, 0.0) for g in range(4)],
            axis=0).astype(BF16)
        m_ref[...] = jnp.full(m_ref.shape, NEG, F32)
        l_ref[...] = jnp.zeros(l_ref.shape, F32)
        acc_ref[...] = jnp.zeros(acc_ref.shape, F32)

        def att_step(kt, width):
            r0 = pl.multiple_of(kt * (width * tk), width * tk)
            s = lax.dot_general(q_st, k_ref[pl.ds(r0, width * tk), ls], _NT,
                                preferred_element_type=F32)
            chunks = []
            for c in range(width * nch):
                bias = lax.bitcast_convert_type(key_ref[kt * (width * nch) + c], F32)
                sc = s[:, c * LANES:(c + 1) * LANES]
                chunks.append(jnp.concatenate(
                    [sc[g * tq:(g + 1) * tq] + bias for g in range(4)], axis=0))
            own = lambda pv: jnp.concatenate([pv[:2 * tq, :LANES], pv[2 * tq:, LANES:]], axis=0)
            m_ref[...], l_ref[...], acc_ref[...] = _softmax_update(
                chunks, v_ref[pl.ds(r0, width * tk), ls], m_ref[...], l_ref[...], acc_ref[...],
                own)

        def wide_step(kt, carry):
            att_step(kt, att_ratio)
            return carry

        def single_step(kt, carry):
            att_step(kt, 1)
            return carry

        lax.fori_loop(0, n_wide, wide_step, 0)
        lax.fori_loop(n_wide * att_ratio, n_tiles, single_step, 0)
        o = _softmax_finish(l_ref, acc_ref)
        for pr in range(2):
            lo_h, hi_h = o[2 * pr * tq:(2 * pr + 1) * tq], o[(2 * pr + 1) * tq:(2 * pr + 2) * tq]
            o_ref[:, (2 * hq + pr) * LANES:(2 * hq + pr + 1) * LANES] = jnp.where(
                lane < LANES // 2, lo_h, hi_h).astype(o_ref.dtype)


def _dsa_call(qa, ka, va, iq, ik, iw, tq):
    b, s, _ = qa.shape
    topk = min(TOPK_MAX, s // 4)
    qrow = lambda width: pl.BlockSpec((None, tq, width), lambda bi, i: (bi, i, 0))
    seq = lambda width: pl.BlockSpec((None, s, width), lambda bi, i: (bi, 0, 0),
                                     pipeline_mode=pl.Buffered(1))
    att_ratio = 2 if s % (2 * tq) == 0 else 1
    assert s % tq == 0 and tq % LANES == 0 and tq // LANES <= 8 and s // LANES <= 256
    return pl.pallas_call(
        functools.partial(_dsa_kernel, tq=tq, topk=topk, att_ratio=att_ratio),
        grid=(b, s // tq),
        in_specs=[qrow(D_A), seq(D_A), seq(D_A), qrow(IDX_HEADS * IDX_DIM), seq(LANES), qrow(LANES)],
        out_specs=qrow(D_A),
        out_shape=jax.ShapeDtypeStruct((b, s, D_A), BF16),
        scratch_shapes=[
            pltpu.VMEM((s // LANES, tq, LANES), I32),
            pltpu.VMEM((IDX_HEADS * tq, LANES), BF16),
            pltpu.VMEM((IDX_HEADS * tq, LANES), F32),
            pltpu.VMEM((tq, LANES), I32),
            pltpu.VMEM((tq, LANES), I32),
            pltpu.VMEM((tq, LANES), I32),
            pltpu.VMEM((tq, LANES), I32),
            pltpu.VMEM((tq, LANES), F32),
            pltpu.VMEM((tq, LANES), F32),
            pltpu.VMEM((8, LANES), I32),
            pltpu.VMEM((8, LANES), I32),
            pltpu.VMEM((4 * tq, LANES), F32),
            pltpu.VMEM((4 * tq, LANES), F32),
            pltpu.VMEM((4 * tq, LANES), F32),
        ],
        compiler_params=pltpu.CompilerParams(
            dimension_semantics=("arbitrary", "arbitrary"), vmem_limit_bytes=VMEM_LIMIT),
        name="dsa",
    )(qa, ka, va, iq, ik, iw)


def _diff_kernel(q_ref, k_ref, v_ref, lam_ref, g_ref, o_ref, m_ref, l_ref, acc_ref,
                 *, tq, tk, nh, lam_init):
    j = pl.program_id(2)
    q0 = j * tq
    ng = 2 * nh
    lane = lax.broadcasted_iota(I32, (tq, LANES), 1)
    lane_w = lax.broadcasted_iota(I32, (tq, nh * LANES), 1)
    row = lax.broadcasted_iota(I32, (tq, 1), 0)
    q_end = _chunk_end(q0 + row)
    hd = B_HEAD_DIM

    q_w = q_ref[...].astype(F32)
    q_st = jnp.concatenate(
        [jnp.where((lane_w >= g * hd) & (lane_w < (g + 1) * hd), q_w, 0.0) for g in range(ng)],
        axis=0).astype(BF16)
    m_ref[...] = jnp.full(m_ref.shape, NEG, F32)
    l_ref[...] = jnp.zeros(l_ref.shape, F32)
    acc_ref[...] = jnp.zeros(acc_ref.shape, F32)

    def own(nrow):
        return lambda pv: jnp.concatenate(
            [pv[2 * h * nrow:(2 * h + 2) * nrow, h * LANES:(h + 1) * LANES] for h in range(nh)],
            axis=0)

    def att_tile(kt, masked):
        r0 = pl.multiple_of(kt * tk, tk)
        s = lax.dot_general(q_st, k_ref[pl.ds(r0, tk), :], _NT, preferred_element_type=F32)
        chunks = []
        for c in range(tk // LANES):
            sc = s[:, c * LANES:(c + 1) * LANES]
            if masked:
                adm = (kt * tk + c * LANES + lane) < q_end
                sc = jnp.concatenate(
                    [jnp.where(adm, sc[g * tq:(g + 1) * tq], NEG) for g in range(ng)], axis=0)
            chunks.append(sc)
        m_ref[...], l_ref[...], acc_ref[...] = _softmax_update(
            chunks, v_ref[pl.ds(r0, tk), :], m_ref[...], l_ref[...], acc_ref[...], own(tq))

    def full_tile(kt, carry):
        att_tile(kt, False)
        return carry

    def diag_step(key0, width, r_lo):
        rowsets = tuple(slice(g * tq + r_lo, (g + 1) * tq) for g in range(ng))
        nrow = tq - r_lo
        k0 = pl.multiple_of(q0 + key0, width)
        q_sub = jnp.concatenate([q_st[rs] for rs in rowsets], axis=0)
        s = lax.dot_general(q_sub, k_ref[pl.ds(k0, width), :], _NT, preferred_element_type=F32)
        chunks = []
        for c in range(width // LANES):
            sc = s[:, c * LANES:(c + 1) * LANES]
            lane_sub = lax.broadcasted_iota(I32, (nrow, LANES), 1)
            adm = (q0 + key0 + c * LANES + lane_sub) < q_end[r_lo:]
            chunks.append(jnp.concatenate(
                [jnp.where(adm, sc[g * nrow:(g + 1) * nrow], NEG) for g in range(ng)], axis=0))
        gather = lambda ref: jnp.concatenate([ref[rs, :] for rs in rowsets], axis=0)
        new = _softmax_update(chunks, v_ref[pl.ds(k0, width), :],
                              gather(m_ref), gather(l_ref), gather(acc_ref), own(nrow))
        for ref, val in zip((m_ref, l_ref, acc_ref), new):
            for g, rs in enumerate(rowsets):
                ref[rs, :] = val[g * nrow:(g + 1) * nrow]

    n_full = j * (tq // tk)
    lax.fori_loop(0, n_full, full_tile, 0)
    diag_step(0, tq // 2, 0)
    diag_step(tq // 2, tq // 2, tq // 2)

    lam_p = lam_ref[...]
    lam = (jnp.exp(jnp.sum(lam_p[0:1] * lam_p[1:2], axis=1, keepdims=True))
           - jnp.exp(jnp.sum(lam_p[2:3] * lam_p[3:4], axis=1, keepdims=True)) + lam_init)
    o = _softmax_finish(l_ref, acc_ref)
    for h in range(nh):
        oh = o[2 * h * tq:(2 * h + 1) * tq] - lam * o[(2 * h + 1) * tq:(2 * h + 2) * tq]
        o_ref[:, h * LANES:(h + 1) * LANES] = (
            _rms(oh, g_ref[...]) * (1.0 - lam_init)).astype(o_ref.dtype)


def _diff_call(qb, kb, vb, lam_p, g_subln, lam_init, tq, tk, nh):
    b, s, _ = qb.shape
    w = nh * LANES
    return pl.pallas_call(
        functools.partial(_diff_kernel, tq=tq, tk=tk, nh=nh, lam_init=lam_init),
        grid=(b, B_HEADS // nh, s // tq),
        in_specs=[
            pl.BlockSpec((None, tq, w), lambda bi, h, i: (bi, i, h)),
            pl.BlockSpec((None, s, w), lambda bi, h, i: (bi, 0, h)),
            pl.BlockSpec((None, s, w), lambda bi, h, i: (bi, 0, h)),
            pl.BlockSpec((4, B_HEAD_DIM), lambda bi, h, i: (0, 0)),
            pl.BlockSpec((1, LANES), lambda bi, h, i: (0, 0)),
        ],
        out_specs=pl.BlockSpec((None, tq, w), lambda bi, h, i: (bi, i, h)),
        out_shape=jax.ShapeDtypeStruct((b, s, D_B), BF16),
        scratch_shapes=[
            pltpu.VMEM((2 * nh * tq, LANES), F32),
            pltpu.VMEM((2 * nh * tq, LANES), F32),
            pltpu.VMEM((2 * nh * tq, LANES), F32),
        ],
        compiler_params=pltpu.CompilerParams(
            dimension_semantics=("arbitrary", "arbitrary", "arbitrary"),
            vmem_limit_bytes=VMEM_LIMIT),
        name="diff",
    )(qb, kb, vb, lam_p, g_subln)


def _ffn_kernel(x_ref, a_ref, b_ref, wo_ref, g_ref, wg_ref, wu_ref, wd_ref, gf_ref, o_ref,
                *, ff_chunk, final_norm):
    d_a = a_ref.shape[1]
    x1 = (x_ref[...]
          + jnp.dot(a_ref[...], wo_ref[:d_a, :], preferred_element_type=F32)
          + jnp.dot(b_ref[...], wo_ref[d_a:, :], preferred_element_type=F32))
    h2 = _rms(x1, g_ref[...]).astype(BF16)
    y = None
    for c in range(wg_ref.shape[1] // ff_chunk):
        cs = slice(c * ff_chunk, (c + 1) * ff_chunk)
        gate = jnp.dot(h2, wg_ref[:, cs], preferred_element_type=F32)
        up = jnp.dot(h2, wu_ref[:, cs], preferred_element_type=F32)
        act = (gate / (1.0 + jnp.exp(-gate)) * up).astype(BF16)
        down = jnp.dot(act, wd_ref[cs, :], preferred_element_type=F32)
        y = down if y is None else y + down
    x2 = x1 + y
    if final_norm:
        x2 = _rms(x2, gf_ref[...])
    o_ref[...] = x2


def _ffn_call(x, out_a, out_b, wo, g, wg, wu, wd, g_final, final_norm, tm, ff_chunk):
    t, d = x.shape
    d_ff = wg.shape[1]
    row = lambda width: pl.BlockSpec((tm, width), lambda i: (i, 0))
    const = lambda shape: pl.BlockSpec(shape, lambda i: (0, 0), pipeline_mode=pl.Buffered(1))
    return pl.pallas_call(
        functools.partial(_ffn_kernel, ff_chunk=ff_chunk, final_norm=final_norm),
        grid=(t // tm,),
        in_specs=[row(d), row(out_a.shape[1]), row(out_b.shape[1]), const(wo.shape), const((1, d)),
                  const((d, d_ff)), const((d, d_ff)), const((d_ff, d)), const((1, d))],
        out_specs=row(d),
        out_shape=jax.ShapeDtypeStruct((t, d), F32),
        compiler_params=pltpu.CompilerParams(
            dimension_semantics=("arbitrary",), vmem_limit_bytes=VMEM_LIMIT),
        name="ffn",
    )(x, out_a, out_b, wo, g, wg, wu, wd, g_final)


def _rope_tables(seq_len, dim):
    pos = jnp.arange(seq_len, dtype=F32)
    inv = ROPE_THETA ** (-jnp.arange(0, dim, 2, dtype=F32) / dim)
    ang = pos[:, None] * inv[None, :]
    cos, sin = jnp.cos(ang), jnp.sin(ang)
    reps = LANES // dim
    return (jnp.tile(jnp.concatenate([cos, cos], axis=1), (1, reps)),
            jnp.tile(jnp.concatenate([-sin, sin], axis=1), (1, reps)))


def _pack_w_in(w):
    splits = (D_A, D_A, D_A, IDX_HEADS * IDX_DIM, IDX_DIM, IDX_HEADS, D_B, D_B, D_B)
    offs = [0]
    for sz in splits:
        offs.append(offs[-1] + sz)
    qa, ka, va, iq, ik, iw, qb, kb, vb = (w[:, offs[i]:offs[i + 1]] for i in range(len(splits)))
    ik4 = jnp.tile(ik, (1, LANES // IDX_DIM))
    iw_p = jnp.pad(iw, ((0, 0), (0, LANES - IDX_HEADS)))
    return jnp.concatenate([qa, ka, va, iq, ik4, iw_p, qb, kb, vb], axis=1).astype(BF16)


def kernel(x, w_in, w_out, g_mix, lam_q1, lam_k1, lam_q2, lam_k2, g_subln, g_ffn, w_gate, w_up,
           w_down, g_final):
    b, s, d = x.shape
    depth = w_in.shape[0]
    tm_proj, tq, tm_ffn = 512, 512, 512
    tq_diff = 2 * tq if s % (2 * tq) == 0 else tq
    d_ff = w_gate.shape[-1]
    ff_chunk = d_ff // 2
    assert s % tm_proj == 0 and s % tq == 0 and (b * s) % tm_ffn == 0 and ff_chunk % LANES == 0

    tabs = _rope_tables(s, A_HEAD_DIM) + _rope_tables(s, IDX_DIM)
    for layer in range(depth):
        qa, ka, va, iq, ik, iw, qb, kb, vb = _proj_call(
            x, g_mix[layer][None, :], _pack_w_in(w_in[layer]), tabs, tm_proj)
        out_a = _dsa_call(qa, ka, va, iq, ik, iw, tq)
        lam_init = 0.8 - 0.6 * math.exp(-0.3 * layer)
        lam_p = jnp.stack([lam_q1[layer], lam_k1[layer], lam_q2[layer], lam_k2[layer]]).astype(F32)
        out_b = _diff_call(qb, kb, vb, lam_p, g_subln[layer][None, :], lam_init, tq_diff, tq_diff, 2)
        x = _ffn_call(
            x.reshape(b * s, d), out_a.reshape(b * s, D_A), out_b.reshape(b * s, D_B),
            w_out[layer].astype(BF16), g_ffn[layer][None, :], w_gate[layer].astype(BF16),
            w_up[layer].astype(BF16), w_down[layer].astype(BF16), g_final[None, :],
            layer == depth - 1, tm_ffn, ff_chunk).reshape(b, s, d)
    return x
```

```python
import functools
import math

import jax
import jax.numpy as jnp
import numpy as np
from jax import lax
from jax.experimental import pallas as pl
from jax.experimental.pallas import tpu as pltpu

F32 = jnp.float32
BF16 = jnp.bfloat16
I32 = jnp.int32

CHUNK = 64
CHUNK_SHIFT = CHUNK.bit_length() - 1
ROPE_THETA = 10000.0
RMS_EPS = 1e-6
A_HEADS = 8
A_HEAD_DIM = 64
D_A = A_HEADS * A_HEAD_DIM
IDX_HEADS = 8
IDX_DIM = 32
TOPK_MAX = 256
B_HEADS = 4
B_HEAD_DIM = 64
D_B = B_HEADS * 2 * B_HEAD_DIM

LANES = 128
VMEM_LIMIT = 60 * 1024 * 1024

LOG2E = 1.4426950408889634
NEG = -1e30
NEG_BITS = int(np.float32(NEG).view(np.int32))
KEY_MIN = -(2 ** 31)
KEY_MAX = 2 ** 31 - 1

_OFF_QA, _OFF_KA, _OFF_VA = 0, 512, 1024
_OFF_IQ, _OFF_IK, _OFF_IW = 1536, 1792, 1920
_OFF_QB, _OFF_KB, _OFF_VB = 2048, 2560, 3072
_W_CAT = 3584

_NT = (((1,), (1,)), ((), ()))


def _rms(x, g):
    r = lax.rsqrt(jnp.mean(x * x, axis=-1, keepdims=True) + RMS_EPS)
    return x * r * g


def _proj_kernel(x_ref, g_ref, w_ref, cos_a_ref, sin_a_ref, cos_i_ref, sin_i_ref,
                 qa_ref, ka_ref, va_ref, iq_ref, ik_ref, iw_ref, qb_ref, kb_ref, vb_ref):
    h = _rms(x_ref[...], g_ref[...]).astype(BF16)
    tm = h.shape[0]
    lane = lax.broadcasted_iota(I32, (tm, LANES), 1)

    def proj(off, width):
        return jnp.dot(h, w_ref[:, off:off + width], preferred_element_type=F32)

    def rope_tile(y, cos, sin_signed, half):
        first = (lane & (2 * half - 1)) < half
        rot = jnp.where(first, pltpu.roll(y, LANES - half, 1), pltpu.roll(y, half, 1))
        return y * cos + rot * sin_signed

    def rope_store(out_ref, off, width, cos_ref, sin_ref, half, scale):
        y = proj(off, width)
        cos, sin = cos_ref[...], sin_ref[...]
        for t in range(width // LANES):
            r = rope_tile(y[:, t * LANES:(t + 1) * LANES], cos, sin, half)
            if scale != 1.0:
                r = r * scale
            out_ref[:, t * LANES:(t + 1) * LANES] = r.astype(out_ref.dtype)

    half_a, half_i = A_HEAD_DIM // 2, IDX_DIM // 2
    rope_store(qa_ref, _OFF_QA, D_A, cos_a_ref, sin_a_ref, half_a, A_HEAD_DIM ** -0.5 * LOG2E)
    rope_store(ka_ref, _OFF_KA, D_A, cos_a_ref, sin_a_ref, half_a, 1.0)
    va_ref[...] = proj(_OFF_VA, D_A).astype(va_ref.dtype)
    rope_store(iq_ref, _OFF_IQ, IDX_HEADS * IDX_DIM, cos_i_ref, sin_i_ref, half_i, 1.0)
    rope_store(ik_ref, _OFF_IK, LANES, cos_i_ref, sin_i_ref, half_i, 1.0)
    iw_ref[...] = proj(_OFF_IW, LANES) * (IDX_HEADS ** -0.5 * IDX_DIM ** -0.5)
    rope_store(qb_ref, _OFF_QB, D_B, cos_a_ref, sin_a_ref, half_a, B_HEAD_DIM ** -0.5 * LOG2E)
    rope_store(kb_ref, _OFF_KB, D_B, cos_a_ref, sin_a_ref, half_a, 1.0)
    vb_ref[...] = proj(_OFF_VB, D_B).astype(vb_ref.dtype)


def _proj_call(x, g, w_cat, tabs, tm):
    b, s, d = x.shape
    cos_a, sin_a, cos_i, sin_i = tabs
    row = lambda width: pl.BlockSpec((None, tm, width), lambda bi, i: (bi, i, 0))
    tab = pl.BlockSpec((tm, LANES), lambda bi, i: (i, 0))
    const = lambda shape: pl.BlockSpec(shape, lambda bi, i: (0, 0), pipeline_mode=pl.Buffered(1))
    widths = (D_A, D_A, D_A, IDX_HEADS * IDX_DIM, LANES, LANES, D_B, D_B, D_B)
    dtypes = (BF16, BF16, BF16, BF16, BF16, F32, BF16, BF16, BF16)
    return pl.pallas_call(
        _proj_kernel,
        grid=(b, s // tm),
        in_specs=[row(d), const((1, d)), const((d, _W_CAT)), tab, tab, tab, tab],
        out_specs=[row(w) for w in widths],
        out_shape=[jax.ShapeDtypeStruct((b, s, w), dt) for w, dt in zip(widths, dtypes)],
        compiler_params=pltpu.CompilerParams(
            dimension_semantics=("arbitrary", "arbitrary"), vmem_limit_bytes=VMEM_LIMIT),
        name="proj",
    )(x, g, w_cat, cos_a, sin_a, cos_i, sin_i)


def _softmax_update(chunks, v_t, m_prev, l_prev, acc_prev, own=None):
    m_cur = functools.reduce(jnp.maximum, chunks)
    m_new = jnp.maximum(m_prev, jnp.max(m_cur, axis=1, keepdims=True))
    alpha = jnp.exp2(m_prev - m_new)
    p = [jnp.exp2(c - m_new) for c in chunks]
    l_new = alpha * l_prev + functools.reduce(jnp.add, p)
    pv = jnp.dot(jnp.concatenate(p, axis=1).astype(BF16), v_t, preferred_element_type=F32)
    if own is not None:
        pv = own(pv)
    return m_new, l_new, alpha * acc_prev + pv


def _online_softmax_step(chunks, v_t, m_ref, l_ref, acc_ref):
    m_ref[...], l_ref[...], acc_ref[...] = _softmax_update(
        chunks, v_t, m_ref[...], l_ref[...], acc_ref[...])


def _softmax_finish(l_ref, acc_ref):
    return acc_ref[...] / jnp.sum(l_ref[...], axis=1, keepdims=True)


def _chunk_end(pos):
    return ((pos >> CHUNK_SHIFT) + 1) << CHUNK_SHIFT


def _split_lane_halves(tile_f32, lane):
    lo = jnp.where(lane < LANES // 2, tile_f32, 0.0)
    hi = jnp.where(lane >= LANES // 2, tile_f32, 0.0)
    return jnp.concatenate([lo, hi], axis=0).astype(BF16)


def _dsa_kernel(q_ref, k_ref, v_ref, iq_ref, ik_ref, iw_ref, o_ref,
                key_ref, iqm_ref, wb_ref, mx2_ref, mx_ref, thr_ref, hi_ref, need_ref, cnt_ref,
                e1_ref, e2_ref, m_ref, l_ref, acc_ref,
                *, tq, topk, att_ratio):
    tk = tq
    nch = tk // LANES
    tki = 512
    rb = 128
    j = pl.program_id(1)
    q0 = j * tq
    n_tiles = j + 1
    lane = lax.broadcasted_iota(I32, (tq, LANES), 1)
    lane_rb = lax.broadcasted_iota(I32, (rb, LANES), 1)
    row = lax.broadcasted_iota(I32, (tq, 1), 0)
    q_end = _chunk_end(q0 + row)

    nb = tq // rb
    sub8 = lax.broadcasted_iota(I32, (8, LANES), 0)
    lane8 = lax.broadcasted_iota(I32, (8, LANES), 1)
    n_adm = _chunk_end(q0 + sub8 * rb + lane8).astype(F32)
    k_eff = jnp.minimum(n_adm, float(topk))
    ones8 = jnp.ones((8, LANES), BF16)
    tri_r = lax.broadcasted_iota(I32, (LANES, 2 * LANES), 0)
    tri_c = lax.broadcasted_iota(I32, (LANES, 2 * LANES), 1)
    tri_ones = jnp.where((tri_c >= LANES) | (tri_r <= tri_c), 1.0, 0.0).astype(BF16)

    def rows_of(x8, r):
        x = jnp.broadcast_to(x8[r:r + 1, :], (rb, LANES))
        if x8.dtype == I32:
            return lax.bitcast_convert_type(jnp.transpose(lax.bitcast_convert_type(x, F32)), I32)
        return jnp.transpose(x)

    def lanes_of(col):
        out = jnp.zeros((8, LANES), col.dtype)
        for r in range(nb):
            x = jnp.broadcast_to(col[r * rb:(r + 1) * rb], (rb, LANES))
            if col.dtype == I32:
                x = lax.bitcast_convert_type(jnp.transpose(lax.bitcast_convert_type(x, F32)), I32)
            else:
                x = jnp.transpose(x)
            out = jnp.where(sub8 == r, x[0:8, :], out)
        return out

    iw = iw_ref[...]
    for h in range(IDX_HEADS):
        g = h % 4
        tile = iq_ref[:, (h // 4) * LANES:(h // 4 + 1) * LANES].astype(F32)
        keep = (lane >= g * IDX_DIM) & (lane < (g + 1) * IDX_DIM)
        iq_h = jnp.where(keep, tile, 0.0).astype(BF16)
        w_h = jnp.broadcast_to(iw[:, h:h + 1], (tq, LANES))
        for r in range(nb):
            dst = slice((r * IDX_HEADS + h) * rb, (r * IDX_HEADS + h + 1) * rb)
            iqm_ref[dst, :] = iq_h[r * rb:(r + 1) * rb]
            wb_ref[dst, :] = w_h[r * rb:(r + 1) * rb]

    mx2_ref[...] = jnp.full((tq, LANES), KEY_MIN, I32)
    mx_ref[...] = jnp.full((tq, LANES), KEY_MIN, I32)

    def index_tile(it, masked):
        r0 = pl.multiple_of(it * tki, tki)
        ik_t = ik_ref[pl.ds(r0, tki), :]
        for r in range(nb):
            rs = slice(r * rb, (r + 1) * rb)
            base = r * IDX_HEADS * rb
            d = lax.dot_general(iqm_ref[base:base + IDX_HEADS * rb, :], ik_t, _NT,
                                preferred_element_type=F32)
            for c in range(tki // LANES):
                a = jnp.zeros((rb, LANES), F32)
                for h in range(IDX_HEADS):
                    hs = slice(h * rb, (h + 1) * rb)
                    w_h = wb_ref[base + h * rb:base + (h + 1) * rb, :]
                    a = a + w_h * jnp.maximum(d[hs, c * LANES:(c + 1) * LANES], 0.0)
                bits = lax.bitcast_convert_type(a, I32)
                key = bits ^ ((bits >> 31) & 0x7FFFFFFF)
                kidx = it * tki + c * LANES + lane_rb
                key = jnp.where(key == 0, -1 - kidx, key)
                if masked:
                    key = jnp.where(kidx < q_end[rs], key, KEY_MIN)
                top1 = mx_ref[rs, :]
                mx2_ref[rs, :] = jnp.maximum(mx2_ref[rs, :], jnp.minimum(top1, key))
                mx_ref[rs, :] = jnp.maximum(top1, key)
                key_ref[it * (tki // LANES) + c, rs, :] = key

    def full_tile(it, carry):
        index_tile(it, False)
        return carry

    n_full = q0 // tki
    lax.fori_loop(0, n_full, full_tile, 0)
    for t in range(tq // tki):
        index_tile(n_full + t, True)

    def count_pass(per_block):
        cnt_ref[...] = jnp.zeros((tq, LANES), F32)

        def body(kt, carry):
            for r in range(nb):
                rs = slice(r * rb, (r + 1) * rb)
                pred = per_block(rs)
                acc = cnt_ref[rs, :]
                for c in range(nch):
                    hit = pred(key_ref[kt * nch + c, rs, :], kt * tk + c * LANES)
                    acc = acc + jnp.where(hit, 1.0, 0.0)
                cnt_ref[rs, :] = acc
            return carry

        lax.fori_loop(0, n_tiles, body, 0)
        c8 = lax.dot_general(ones8, cnt_ref[...].astype(BF16), _NT, preferred_element_type=F32)
        cnt = jnp.zeros((8, LANES), F32)
        for r in range(nb):
            cnt = jnp.where(sub8 == r, c8[:, r * rb:(r + 1) * rb], cnt)
        return cnt

    def set_rows(ref, x8):
        for r in range(nb):
            ref[r * rb:(r + 1) * rb, :] = rows_of(x8, r)

    def count_ge(t8):
        set_rows(thr_ref, t8)

        def per_block(rs):
            tb = thr_ref[rs, :]
            return lambda keys, kidx0: keys >= tb

        return count_pass(per_block)

    def flip(k):
        return k ^ ((k >> 31) & 0x7FFFFFFF)

    live = sub8 < nb
    many = 2.0 ** 30
    small = n_adm <= float(topk)
    lo0 = lanes_of(jnp.min(mx2_ref[...], axis=1, keepdims=True))
    lo0 = jnp.where(live, jnp.where(small, KEY_MIN + 1, lo0), 0)
    hi0 = jnp.where(live, lanes_of(jnp.max(mx_ref[...], axis=1, keepdims=True)), 0)
    c_lo0 = jnp.where(small, n_adm, many)
    c_hi0 = jnp.zeros((8, LANES), F32)
    value_steps = 26
    zero_lo = -(key_ref.shape[0] * LANES)

    def bis_cond(st):
        it, lo, hi, c_lo, c_above = st
        open_rows = jnp.where((c_lo > k_eff) & (lo < hi) & (c_lo - c_above > 2.0), 1.0, 0.0)
        return jnp.max(open_rows) > 0.0

    def bis_body(st):
        it, lo, hi, c_lo, c_above = st
        mid_k = (lo >> 1) + (hi >> 1) + ((lo | hi) & 1)
        v_lo = lax.bitcast_convert_type(flip(lo), F32)
        v_hi = lax.bitcast_convert_type(flip(hi), F32)
        mid_v = flip(lax.bitcast_convert_type(0.5 * v_lo + 0.5 * v_hi, I32))
        probe_lo = (lo < zero_lo) & (hi >= zero_lo)
        probe_hi = (lo >= zero_lo) & (lo < 1) & (hi >= 1)
        in_zero = (lo >= zero_lo) & (hi < 1)
        mid_v = jnp.where(probe_lo, zero_lo, jnp.where(probe_hi, 1, jnp.where(in_zero, mid_k, mid_v)))
        mid_v = jnp.minimum(jnp.maximum(mid_v, lo + 1), hi)
        mid = jnp.where(it < value_steps, mid_v, mid_k)
        active = (c_lo > k_eff) & (lo < hi)
        c_mid = count_ge(mid)
        go_up = active & (c_mid >= k_eff)
        go_dn = active & (c_mid < k_eff)
        lo = jnp.where(go_up, mid, lo)
        c_lo = jnp.where(go_up, c_mid, c_lo)
        hi = jnp.where(go_dn, mid - 1, hi)
        c_above = jnp.where(go_dn, c_mid, c_above)
        return it + 1, lo, hi, c_lo, c_above

    _, lo_f, hi_f, c_lo_f, c_above = lax.while_loop(
        bis_cond, bis_body, (jnp.int32(0), lo0, hi0, c_lo0, c_hi0))

    pending = live & (c_lo_f > k_eff) & (lo_f < hi_f)
    e1_ref[...] = lo_f
    e2_ref[...] = lo_f

    @pl.when(jnp.max(jnp.where(pending, 1.0, 0.0)) > 0.0)
    def _():
        set_rows(thr_ref, lo_f)
        set_rows(hi_ref, hi_f)
        mx_ref[...] = jnp.full((tq, LANES), KEY_MIN, I32)
        mx2_ref[...] = jnp.full((tq, LANES), KEY_MAX, I32)

        def body(kt, carry):
            for r in range(nb):
                rs = slice(r * rb, (r + 1) * rb)
                lo_b, hi_b = thr_ref[rs, :], hi_ref[rs, :]
                top, bot = mx_ref[rs, :], mx2_ref[rs, :]
                for c in range(nch):
                    keys = key_ref[kt * nch + c, rs, :]
                    cand = (keys >= lo_b) & (keys <= hi_b)
                    top = jnp.maximum(top, jnp.where(cand, keys, KEY_MIN))
                    bot = jnp.minimum(bot, jnp.where(cand, keys, KEY_MAX))
                mx_ref[rs, :] = top
                mx2_ref[rs, :] = bot
            return carry

        lax.fori_loop(0, n_tiles, body, 0)
        e1_ref[...] = lanes_of(jnp.max(mx_ref[...], axis=1, keepdims=True))
        e2_ref[...] = lanes_of(jnp.min(mx2_ref[...], axis=1, keepdims=True))

    e1, e2 = e1_ref[...], e2_ref[...]
    thr = jnp.where(pending, e1, lo_f)
    c_thr = jnp.where(pending, c_above + jnp.where(e1 == e2, 2.0, 1.0), c_lo_f)
    set_rows(thr_ref, thr)
    need = k_eff - c_above
    has_tie = live & (c_thr > k_eff)
    any_tie = jnp.max(jnp.where(has_tie, 1.0, 0.0)) > 0.0

    def bias_pass(with_ties):
        def bias_tile(kt, carry):
            for c in range(nch):
                keys = key_ref[kt * nch + c]
                thr_b = thr_ref[...]
                tie = keys == thr_b
                if with_ties:
                    hits = jnp.dot(jnp.where(tie, 1.0, 0.0).astype(BF16), tri_ones,
                                   preferred_element_type=F32)
                    seen = cnt_ref[...]
                    tie = tie & (hits[:, :LANES] + seen <= need_ref[...])
                    cnt_ref[...] = seen + hits[:, LANES:]
                key_ref[kt * nch + c] = jnp.where((keys > thr_b) | tie, 0, NEG_BITS)
            return carry

        lax.fori_loop(0, n_tiles, bias_tile, 0)

    @pl.when(any_tie)
    def _():
        set_rows(need_ref, jnp.where(has_tie, need, 2.0 ** 30))
        cnt_ref[...] = jnp.zeros((tq, LANES), F32)
        bias_pass(True)

    @pl.when(jnp.logical_not(any_tie))
    def _():
        bias_pass(False)

    n_wide = j >> (att_ratio.bit_length() - 1)

    lane2 = lax.broadcasted_iota(I32, (tq, 2 * LANES), 1)
    hd = A_HEAD_DIM
    for hq in range(A_HEADS // 4):
        ls = slice(hq * 2 * LANES, (hq + 1) * 2 * LANES)
        q2 = q_ref[:, ls].astype(F32)
        q_st = jnp.concatenate(
            [jnp.where((lane2 >= g * hd) & (lane2 < (g + 1) * hd), q2, 0.0) for g in range(4)],
            axis=0).astype(BF16)
        m_ref[...] = jnp.full(m_ref.shape, NEG, F32)
        l_ref[...] = jnp.zeros(l_ref.shape, F32)
        acc_ref[...] = jnp.zeros(acc_ref.shape, F32)

        def att_step(kt, width):
            r0 = pl.multiple_of(kt * (width * tk), width * tk)
            s = lax.dot_general(q_st, k_ref[pl.ds(r0, width * tk), ls], _NT,
                                preferred_element_type=F32)
            chunks = []
            for c in range(width * nch):
                bias = lax.bitcast_convert_type(key_ref[kt * (width * nch) + c], F32)
                sc = s[:, c * LANES:(c + 1) * LANES]
                chunks.append(jnp.concatenate(
                    [sc[g * tq:(g + 1) * tq] + bias for g in range(4)], axis=0))
            own = lambda pv: jnp.concatenate([pv[:2 * tq, :LANES], pv[2 * tq:, LANES:]], axis=0)
            m_ref[...], l_ref[...], acc_ref[...] = _softmax_update(
                chunks, v_ref[pl.ds(r0, width * tk), ls], m_ref[...], l_ref[...], acc_ref[...],
                own)

        def wide_step(kt, carry):
            att_step(kt, att_ratio)
            return carry

        def single_step(kt, carry):
            att_step(kt, 1)
            return carry

        def diag_step(key0, width, r_lo):
            rowsets = tuple(slice(g * tq + r_lo, (g + 1) * tq) for g in range(4))
            nrow = tq - r_lo
            k0 = pl.multiple_of(q0 + key0, width)
            q_sub = jnp.concatenate([q_st[rs] for rs in rowsets], axis=0)
            s = lax.dot_general(q_sub, k_ref[pl.ds(k0, width), ls], _NT,
                                preferred_element_type=F32)
            chunks = []
            for c in range(width // LANES):
                bias = lax.bitcast_convert_type(
                    key_ref[j * nch + key0 // LANES + c, r_lo:, :], F32)
                sc = s[:, c * LANES:(c + 1) * LANES]
                chunks.append(jnp.concatenate(
                    [sc[g * nrow:(g + 1) * nrow] + bias for g in range(4)], axis=0))
            own = lambda pv: jnp.concatenate(
                [pv[:2 * nrow, :LANES], pv[2 * nrow:, LANES:]], axis=0)
            gather = lambda ref: jnp.concatenate([ref[rs, :] for rs in rowsets], axis=0)
            new = _softmax_update(chunks, v_ref[pl.ds(k0, width), ls],
                                  gather(m_ref), gather(l_ref), gather(acc_ref), own)
            for ref, val in zip((m_ref, l_ref, acc_ref), new):
                for g, rs in enumerate(rowsets):
                    ref[rs, :] = val[g * nrow:(g + 1) * nrow]

        lax.fori_loop(0, n_wide, wide_step, 0)
        lax.fori_loop(n_wide * att_ratio, j, single_step, 0)
        diag_step(0, tk // 2, 0)
        diag_step(tk // 2, tk // 2, tq // 2)
        o = _softmax_finish(l_ref, acc_ref)
        for pr in range(2):
            lo_h, hi_h = o[2 * pr * tq:(2 * pr + 1) * tq], o[(2 * pr + 1) * tq:(2 * pr + 2) * tq]
            o_ref[:, (2 * hq + pr) * LANES:(2 * hq + pr + 1) * LANES] = jnp.where(
                lane < LANES // 2, lo_h, hi_h).astype(o_ref.dtype)


def _dsa_call(qa, ka, va, iq, ik, iw, tq):
    b, s, _ = qa.shape
    topk = min(TOPK_MAX, s // 4)
    qrow = lambda width: pl.BlockSpec((None, tq, width), lambda bi, i: (bi, i, 0))
    seq = lambda width: pl.BlockSpec((None, s, width), lambda bi, i: (bi, 0, 0),
                                     pipeline_mode=pl.Buffered(1))
    att_ratio = 2 if s % (2 * tq) == 0 else 1
    assert s % tq == 0 and tq % LANES == 0 and tq // LANES <= 8 and s // LANES <= 256
    return pl.pallas_call(
        functools.partial(_dsa_kernel, tq=tq, topk=topk, att_ratio=att_ratio),
        grid=(b, s // tq),
        in_specs=[qrow(D_A), seq(D_A), seq(D_A), qrow(IDX_HEADS * IDX_DIM), seq(LANES), qrow(LANES)],
        out_specs=qrow(D_A),
        out_shape=jax.ShapeDtypeStruct((b, s, D_A), BF16),
        scratch_shapes=[
            pltpu.VMEM((s // LANES, tq, LANES), I32),
            pltpu.VMEM((IDX_HEADS * tq, LANES), BF16),
            pltpu.VMEM((IDX_HEADS * tq, LANES), F32),
            pltpu.VMEM((tq, LANES), I32),
            pltpu.VMEM((tq, LANES), I32),
            pltpu.VMEM((tq, LANES), I32),
            pltpu.VMEM((tq, LANES), I32),
            pltpu.VMEM((tq, LANES), F32),
            pltpu.VMEM((tq, LANES), F32),
            pltpu.VMEM((8, LANES), I32),
            pltpu.VMEM((8, LANES), I32),
            pltpu.VMEM((4 * tq, LANES), F32),
            pltpu.VMEM((4 * tq, LANES), F32),
            pltpu.VMEM((4 * tq, LANES), F32),
        ],
        compiler_params=pltpu.CompilerParams(
            dimension_semantics=("arbitrary", "arbitrary"), vmem_limit_bytes=VMEM_LIMIT),
        name="dsa",
    )(qa, ka, va, iq, ik, iw)


def _diff_kernel(q_ref, k_ref, v_ref, lam_ref, g_ref, o_ref, m_ref, l_ref, acc_ref,
                 *, tq, tk, nh, lam_init):
    j = pl.program_id(2)
    q0 = j * tq
    ng = 2 * nh
    lane = lax.broadcasted_iota(I32, (tq, LANES), 1)
    lane_w = lax.broadcasted_iota(I32, (tq, nh * LANES), 1)
    row = lax.broadcasted_iota(I32, (tq, 1), 0)
    q_end = _chunk_end(q0 + row)
    hd = B_HEAD_DIM

    q_w = q_ref[...].astype(F32)
    q_st = jnp.concatenate(
        [jnp.where((lane_w >= g * hd) & (lane_w < (g + 1) * hd), q_w, 0.0) for g in range(ng)],
        axis=0).astype(BF16)
    m_ref[...] = jnp.full(m_ref.shape, NEG, F32)
    l_ref[...] = jnp.zeros(l_ref.shape, F32)
    acc_ref[...] = jnp.zeros(acc_ref.shape, F32)

    def own(nrow):
        return lambda pv: jnp.concatenate(
            [pv[2 * h * nrow:(2 * h + 2) * nrow, h * LANES:(h + 1) * LANES] for h in range(nh)],
            axis=0)

    def att_tile(kt, masked):
        r0 = pl.multiple_of(kt * tk, tk)
        s = lax.dot_general(q_st, k_ref[pl.ds(r0, tk), :], _NT, preferred_element_type=F32)
        chunks = []
        for c in range(tk // LANES):
            sc = s[:, c * LANES:(c + 1) * LANES]
            if masked:
                adm = (kt * tk + c * LANES + lane) < q_end
                sc = jnp.concatenate(
                    [jnp.where(adm, sc[g * tq:(g + 1) * tq], NEG) for g in range(ng)], axis=0)
            chunks.append(sc)
        m_ref[...], l_ref[...], acc_ref[...] = _softmax_update(
            chunks, v_ref[pl.ds(r0, tk), :], m_ref[...], l_ref[...], acc_ref[...], own(tq))

    def full_tile(kt, carry):
        att_tile(kt, False)
        return carry

    def diag_step(key0, width, r_lo):
        rowsets = tuple(slice(g * tq + r_lo, (g + 1) * tq) for g in range(ng))
        nrow = tq - r_lo
        k0 = pl.multiple_of(q0 + key0, width)
        q_sub = jnp.concatenate([q_st[rs] for rs in rowsets], axis=0)
        s = lax.dot_general(q_sub, k_ref[pl.ds(k0, width), :], _NT, preferred_element_type=F32)
        chunks = []
        for c in range(width // LANES):
            sc = s[:, c * LANES:(c + 1) * LANES]
            lane_sub = lax.broadcasted_iota(I32, (nrow, LANES), 1)
            adm = (q0 + key0 + c * LANES + lane_sub) < q_end[r_lo:]
            chunks.append(jnp.concatenate(
                [jnp.where(adm, sc[g * nrow:(g + 1) * nrow], NEG) for g in range(ng)], axis=0))
        gather = lambda ref: jnp.concatenate([ref[rs, :] for rs in rowsets], axis=0)
        new = _softmax_update(chunks, v_ref[pl.ds(k0, width), :],
                              gather(m_ref), gather(l_ref), gather(acc_ref), own(nrow))
        for ref, val in zip((m_ref, l_ref, acc_ref), new):
            for g, rs in enumerate(rowsets):
                ref[rs, :] = val[g * nrow:(g + 1) * nrow]

    n_full = j * (tq // tk)
    lax.fori_loop(0, n_full, full_tile, 0)
    diag_step(0, tq // 2, 0)
    diag_step(tq // 2, tq // 2, tq // 2)

    lam_p = lam_ref[...]
    lam = (jnp.exp(jnp.sum(lam_p[0:1] * lam_p[1:2], axis=1, keepdims=True))
           - jnp.exp(jnp.sum(lam_p[2:3] * lam_p[3:4], axis=1, keepdims=True)) + lam_init)
    o = _softmax_finish(l_ref, acc_ref)
    for h in range(nh):
        oh = o[2 * h * tq:(2 * h + 1) * tq] - lam * o[(2 * h + 1) * tq:(2 * h + 2) * tq]
        o_ref[:, h * LANES:(h + 1) * LANES] = (
            _rms(oh, g_ref[...]) * (1.0 - lam_init)).astype(o_ref.dtype)


def _diff_call(qb, kb, vb, lam_p, g_subln, lam_init, tq, tk, nh):
    b, s, _ = qb.shape
    w = nh * LANES
    return pl.pallas_call(
        functools.partial(_diff_kernel, tq=tq, tk=tk, nh=nh, lam_init=lam_init),
        grid=(b, B_HEADS // nh, s // tq),
        in_specs=[
            pl.BlockSpec((None, tq, w), lambda bi, h, i: (bi, i, h)),
            pl.BlockSpec((None, s, w), lambda bi, h, i: (bi, 0, h)),
            pl.BlockSpec((None, s, w), lambda bi, h, i: (bi, 0, h)),
            pl.BlockSpec((4, B_HEAD_DIM), lambda bi, h, i: (0, 0)),
            pl.BlockSpec((1, LANES), lambda bi, h, i: (0, 0)),
        ],
        out_specs=pl.BlockSpec((None, tq, w), lambda bi, h, i: (bi, i, h)),
        out_shape=jax.ShapeDtypeStruct((b, s, D_B), BF16),
        scratch_shapes=[
            pltpu.VMEM((2 * nh * tq, LANES), F32),
            pltpu.VMEM((2 * nh * tq, LANES), F32),
            pltpu.VMEM((2 * nh * tq, LANES), F32),
        ],
        compiler_params=pltpu.CompilerParams(
            dimension_semantics=("arbitrary", "arbitrary", "arbitrary"),
            vmem_limit_bytes=VMEM_LIMIT),
        name="diff",
    )(qb, kb, vb, lam_p, g_subln)


def _ffn_kernel(x_ref, a_ref, b_ref, wo_ref, g_ref, wg_ref, wu_ref, wd_ref, gf_ref, o_ref,
                *, ff_chunk, final_norm):
    d_a = a_ref.shape[1]
    x1 = (x_ref[...]
          + jnp.dot(a_ref[...], wo_ref[:d_a, :], preferred_element_type=F32)
          + jnp.dot(b_ref[...], wo_ref[d_a:, :], preferred_element_type=F32))
    h2 = _rms(x1, g_ref[...]).astype(BF16)
    y = None
    for c in range(wg_ref.shape[1] // ff_chunk):
        cs = slice(c * ff_chunk, (c + 1) * ff_chunk)
        gate = jnp.dot(h2, wg_ref[:, cs], preferred_element_type=F32)
        up = jnp.dot(h2, wu_ref[:, cs], preferred_element_type=F32)
        act = (gate / (1.0 + jnp.exp(-gate)) * up).astype(BF16)
        down = jnp.dot(act, wd_ref[cs, :], preferred_element_type=F32)
        y = down if y is None else y + down
    x2 = x1 + y
    if final_norm:
        x2 = _rms(x2, gf_ref[...])
    o_ref[...] = x2


def _ffn_call(x, out_a, out_b, wo, g, wg, wu, wd, g_final, final_norm, tm, ff_chunk):
    t, d = x.shape
    d_ff = wg.shape[1]
    row = lambda width: pl.BlockSpec((tm, width), lambda i: (i, 0))
    const = lambda shape: pl.BlockSpec(shape, lambda i: (0, 0), pipeline_mode=pl.Buffered(1))
    return pl.pallas_call(
        functools.partial(_ffn_kernel, ff_chunk=ff_chunk, final_norm=final_norm),
        grid=(t // tm,),
        in_specs=[row(d), row(out_a.shape[1]), row(out_b.shape[1]), const(wo.shape), const((1, d)),
                  const((d, d_ff)), const((d, d_ff)), const((d_ff, d)), const((1, d))],
        out_specs=row(d),
        out_shape=jax.ShapeDtypeStruct((t, d), F32),
        compiler_params=pltpu.CompilerParams(
            dimension_semantics=("arbitrary",), vmem_limit_bytes=VMEM_LIMIT),
        name="ffn",
    )(x, out_a, out_b, wo, g, wg, wu, wd, g_final)


def _rope_tables(seq_len, dim):
    pos = jnp.arange(seq_len, dtype=F32)
    inv = ROPE_THETA ** (-jnp.arange(0, dim, 2, dtype=F32) / dim)
    ang = pos[:, None] * inv[None, :]
    cos, sin = jnp.cos(ang), jnp.sin(ang)
    reps = LANES // dim
    return (jnp.tile(jnp.concatenate([cos, cos], axis=1), (1, reps)),
            jnp.tile(jnp.concatenate([-sin, sin], axis=1), (1, reps)))


def _pack_w_in(w):
    splits = (D_A, D_A, D_A, IDX_HEADS * IDX_DIM, IDX_DIM, IDX_HEADS, D_B, D_B, D_B)
    offs = [0]
    for sz in splits:
        offs.append(offs[-1] + sz)
    qa, ka, va, iq, ik, iw, qb, kb, vb = (w[:, offs[i]:offs[i + 1]] for i in range(len(splits)))
    ik4 = jnp.tile(ik, (1, LANES // IDX_DIM))
    iw_p = jnp.pad(iw, ((0, 0), (0, LANES - IDX_HEADS)))
    return jnp.concatenate([qa, ka, va, iq, ik4, iw_p, qb, kb, vb], axis=1).astype(BF16)


def kernel(x, w_in, w_out, g_mix, lam_q1, lam_k1, lam_q2, lam_k2, g_subln, g_ffn, w_gate, w_up,
           w_down, g_final):
    b, s, d = x.shape
    depth = w_in.shape[0]
    tm_proj, tq, tm_ffn = 512, 512, 512
    tq_diff = 2 * tq if s % (2 * tq) == 0 else tq
    d_ff = w_gate.shape[-1]
    ff_chunk = d_ff // 2
    assert s % tm_proj == 0 and s % tq == 0 and (b * s) % tm_ffn == 0 and ff_chunk % LANES == 0

    tabs = _rope_tables(s, A_HEAD_DIM) + _rope_tables(s, IDX_DIM)
    for layer in range(depth):
        qa, ka, va, iq, ik, iw, qb, kb, vb = _proj_call(
            x, g_mix[layer][None, :], _pack_w_in(w_in[layer]), tabs, tm_proj)
        out_a = _dsa_call(qa, ka, va, iq, ik, iw, tq)
        lam_init = 0.8 - 0.6 * math.exp(-0.3 * layer)
        lam_p = jnp.stack([lam_q1[layer], lam_k1[layer], lam_q2[layer], lam_k2[layer]]).astype(F32)
        out_b = _diff_call(qb, kb, vb, lam_p, g_subln[layer][None, :], lam_init, tq_diff, tq_diff, 2)
        x = _ffn_call(
            x.reshape(b * s, d), out_a.reshape(b * s, D_A), out_b.reshape(b * s, D_B),
            w_out[layer].astype(BF16), g_ffn[layer][None, :], w_gate[layer].astype(BF16),
            w_up[layer].astype(BF16), w_down[layer].astype(BF16), g_final[None, :],
            layer == depth - 1, tm_ffn, ff_chunk).reshape(b, s, d)
    return x
```
